```python
import math
import jax, jax.numpy as jnp
from jax import lax
import numpy as np

D_MODEL = 1024
BATCH = 8
SEQ = 4096
DEPTH = 1

HEAD_DIM = 64
MIX_WIDTH = D_MODEL
POOL_WIDTH = MIX_WIDTH // 4
POOL_GROUPS = 4
POOL_GROUP_DIM = POOL_WIDTH // POOL_GROUPS
POOL_WINDOWS = (2, 4, 8, 16)
FOX_WIDTH = MIX_WIDTH // 2
FOX_HEADS = FOX_WIDTH // HEAD_DIM
MEM_HEADS = 4
MEM_WIDTH = MEM_HEADS * HEAD_DIM
N_MEM = 256
Q_BLOCK = 128
EPS = 1e-6
SPLIT_SIZES = (POOL_WIDTH, POOL_WIDTH, FOX_WIDTH, FOX_WIDTH, FOX_WIDTH, FOX_HEADS, FOX_WIDTH, MEM_WIDTH, MEM_WIDTH)
IN_WIDTH = sum(SPLIT_SIZES)

kernel_name = "hymba_pool_fox_memory_layer"


def rms_norm(x, g):
    xf = x.astype(jnp.float32)
    y = xf * lax.rsqrt(jnp.mean(xf * xf, axis=-1, keepdims=True) + EPS)
    return (y * g.astype(jnp.float32)).astype(x.dtype)


def split_cols(proj):
    parts, off = [], 0
    for w in SPLIT_SIZES:
        parts.append(proj[..., off:off + w])
        off += w
    return parts


def to_heads(t, n_heads):
    b, s, _ = t.shape
    return t.reshape(b, s, n_heads, HEAD_DIM).transpose(0, 2, 1, 3)


def from_heads(t):
    b, h, s, d = t.shape
    return t.transpose(0, 2, 1, 3).reshape(b, s, h * d)


def pool_mixer(u, w_pool, scale):
    b, s, _ = u.shape
    uf = u.reshape(b, s, POOL_GROUPS, POOL_GROUP_DIM).astype(jnp.float32)
    csum = jnp.cumsum(uf, axis=1)
    pos = jnp.arange(1, s + 1, dtype=jnp.float32)
    pooled = []
    for g, w in enumerate(POOL_WINDOWS):
        cg = csum[:, :, g]
        lag = jnp.pad(cg, ((0, 0), (w, 0), (0, 0)))[:, :s]
        count = jnp.minimum(pos, float(w))
        pooled.append((cg - lag) / count[None, :, None])
    pooled = jnp.stack(pooled, axis=2)
    d = (pooled - uf).astype(u.dtype)
    y = jnp.einsum('bsgc,gce->bsge', d, w_pool).reshape(b, s, POOL_WIDTH)
    return y * scale


def fox_attention(q, k, v, logf):
    b, h, s, dh = q.shape
    n_blocks = s // Q_BLOCK
    F = jnp.cumsum(logf.astype(jnp.float32), axis=-1)
    qb = q.reshape(b, h, n_blocks, Q_BLOCK, dh).transpose(2, 0, 1, 3, 4)
    Fb = F.reshape(b, h, n_blocks, Q_BLOCK).transpose(2, 0, 1, 3)
    pos_k = jnp.arange(s)
    scale = 1.0 / math.sqrt(dh)

    def block(args):
        qi, Fi, i = args
        logits = jnp.einsum('bhqd,bhkd->bhqk', qi, k).astype(jnp.float32) * scale
        logits = logits + Fi[..., None] - F[:, :, None, :]
        pos_q = i * Q_BLOCK + jnp.arange(Q_BLOCK)
        causal = pos_k[None, :] <= pos_q[:, None]
        logits = jnp.where(causal, logits, -1e30)
        p = jax.nn.softmax(logits, axis=-1)
        return jnp.einsum('bhqk,bhkd->bhqd', p.astype(v.dtype), v)

    out = lax.map(block, (qb, Fb, jnp.arange(n_blocks)))
    return out.transpose(1, 2, 0, 3, 4).reshape(b, h, s, dh)


def memory_attention(q, k, v):
    scale = 1.0 / math.sqrt(q.shape[-1])
    logits = jnp.einsum('bhqd,bhmd->bhqm', q, k).astype(jnp.float32) * scale
    p = jax.nn.softmax(logits, axis=-1)
    return jnp.einsum('bhqm,bhmd->bhqd', p.astype(v.dtype), v)


def setup_inputs(seed: int = 0) -> dict:
    key = jax.random.key(seed)
    ks = jax.random.split(key, 16)
    f32 = jnp.float32
    x = jax.random.normal(ks[0], (BATCH, SEQ, D_MODEL), f32)
    mem = jax.random.normal(ks[1], (BATCH, N_MEM, D_MODEL), f32)
    norm_g = 1.0 + 0.02 * jax.random.normal(ks[2], (DEPTH, D_MODEL), f32)
    w_in = jax.random.normal(ks[3], (DEPTH, D_MODEL, IN_WIDTH), f32) * D_MODEL ** -0.5
    b_f = 1.0 + 4.0 * jax.random.uniform(ks[4], (DEPTH, FOX_HEADS), f32)
    w_pool = jax.random.normal(ks[5], (DEPTH, POOL_GROUPS, POOL_GROUP_DIM, POOL_GROUP_DIM), f32) * POOL_GROUP_DIM ** -0.5
    pool_scale = 1.0 + 0.02 * jax.random.normal(ks[6], (DEPTH, POOL_WIDTH), f32)
    fox_q_g = 1.0 + 0.02 * jax.random.normal(ks[7], (DEPTH, HEAD_DIM), f32)
    fox_k_g = 1.0 + 0.02 * jax.random.normal(ks[8], (DEPTH, HEAD_DIM), f32)
    mem_norm_g = 1.0 + 0.02 * jax.random.normal(ks[9], (DEPTH, D_MODEL), f32)
    w_mem_kv = jax.random.normal(ks[10], (DEPTH, D_MODEL, 2 * MEM_WIDTH), f32) * D_MODEL ** -0.5
    mem_q_g = 1.0 + 0.02 * jax.random.normal(ks[11], (DEPTH, HEAD_DIM), f32)
    mem_k_g = 1.0 + 0.02 * jax.random.normal(ks[12], (DEPTH, HEAD_DIM), f32)
    w_out = jax.random.normal(ks[13], (DEPTH, MIX_WIDTH, D_MODEL), f32) * MIX_WIDTH ** -0.5
    return {"x": x, "mem": mem, "norm_g": norm_g, "w_in": w_in, "b_f": b_f,
            "w_pool": w_pool, "pool_scale": pool_scale, "fox_q_g": fox_q_g,
            "fox_k_g": fox_k_g, "mem_norm_g": mem_norm_g, "w_mem_kv": w_mem_kv,
            "mem_q_g": mem_q_g, "mem_k_g": mem_k_g, "w_out": w_out}


def reference(x, mem, norm_g, w_in, b_f, w_pool, pool_scale, fox_q_g, fox_k_g,
              mem_norm_g, w_mem_kv, mem_q_g, mem_k_g, w_out):
    for l in range(DEPTH):
        h = rms_norm(x, norm_g[l])
        proj = jnp.einsum('bsd,de->bse', h, w_in[l])
        u_a, g_a, q_b, k_b, v_b, f_b, g_b, q_m, g_m = split_cols(proj)

        y_a = pool_mixer(u_a, w_pool[l], pool_scale[l])

        q = rms_norm(to_heads(q_b, FOX_HEADS), fox_q_g[l])
        k = rms_norm(to_heads(k_b, FOX_HEADS), fox_k_g[l])
        v = to_heads(v_b, FOX_HEADS)
        logf = jax.nn.log_sigmoid((f_b + b_f[l]).astype(jnp.float32)).transpose(0, 2, 1)
        y_b = from_heads(fox_attention(q, k, v, logf))

        mem_n = rms_norm(mem, mem_norm_g[l])
        kv = jnp.einsum('bmd,de->bme', mem_n, w_mem_kv[l])
        k_m = rms_norm(to_heads(kv[..., :MEM_WIDTH], MEM_HEADS), mem_k_g[l])
        v_m = to_heads(kv[..., MEM_WIDTH:], MEM_HEADS)
        q_mh = rms_norm(to_heads(q_m, MEM_HEADS), mem_q_g[l])
        y_m = from_heads(memory_attention(q_mh, k_m, v_m))

        mixed = jnp.concatenate([y_a * jax.nn.silu(g_a),
                                 y_b * jax.nn.silu(g_b),
                                 y_m * jax.nn.silu(g_m)], axis=-1)
        x = x + jnp.einsum('bse,ed->bsd', mixed, w_out[l])
    return x
```

```python
import functools
import math

import jax
import jax.numpy as jnp
from jax import lax
from jax.experimental import pallas as pl
from jax.experimental.pallas import tpu as pltpu

F32 = jnp.float32
BF16 = jnp.bfloat16

HEAD_DIM = 64
EPS = 1e-6
POOL_WINDOWS = (2, 4, 8, 16)
POOL_GROUP_DIM = 64
MEM_HEADS = 4
SCALE = 1.0 / math.sqrt(HEAD_DIM)
MASK_VALUE = -1e30

LANES = 128
BF16_ROWS = 16
QK_ROWS = 128
AUG_ROWS = BF16_ROWS
V_ROWS = HEAD_DIM + BF16_ROWS

SEQ_TILE = 512
Q_TILE = 512
KV_TILE = 256
VMEM_LIMIT_BYTES = 56 * 1024 * 1024

NT_DIMS = (((1,), (1,)), ((), ()))


def _dot(a, b):
    return jnp.dot(a, b, preferred_element_type=F32)


def _dot_nt(a, b):
    return lax.dot_general(a, b, NT_DIMS, preferred_element_type=F32)


def _silu(g):
    return g * jax.nn.sigmoid(g)


def _log_sigmoid(z):
    return jnp.minimum(z, 0.0) - jnp.log1p(jnp.exp(-jnp.abs(z)))


def _head_rms_scale(t):
    return lax.rsqrt(jnp.mean(t * t, axis=0, keepdims=True) + EPS)


def _ones_row_block(n):
    row = lax.broadcasted_iota(jnp.int32, (BF16_ROWS, n), 0)
    return (row == 0).astype(F32)


def _mem_kv_kernel(mem_ref, g_ref, wkvT_ref, gk_ref, km_ref, vm_ref):
    x = mem_ref[0]
    n_mem = x.shape[0]
    h = (x * lax.rsqrt(jnp.mean(x * x, axis=-1, keepdims=True) + EPS) * g_ref[...]).astype(BF16)
    kvT = _dot_nt(wkvT_ref[...], h)
    width = MEM_HEADS * HEAD_DIM
    ones_blk = _ones_row_block(n_mem)
    zero_pad = jnp.zeros((QK_ROWS - HEAD_DIM, n_mem), F32)
    for hm in range(MEM_HEADS):
        kh = kvT[hm * HEAD_DIM:(hm + 1) * HEAD_DIM]
        kn = kh * _head_rms_scale(kh) * gk_ref[...]
        km_ref[0, hm] = jnp.concatenate([kn, zero_pad], axis=0).T.astype(BF16)
        vh = kvT[width + hm * HEAD_DIM:width + (hm + 1) * HEAD_DIM]
        vm_ref[0, hm] = jnp.concatenate([vh, ones_blk], axis=0).astype(BF16)


def _proj_kernel(x_ref, ng_ref, wT_ref, bf_ref, tri_ref, gq_ref, gk_ref, wp_ref, ps_ref,
                 km_ref, vm_ref, gmq_ref,
                 qT_ref, kp_ref, vT_ref, gbT_ref, mam_ref,
                 fcarry_ref, halo_ref, *, offs, n_fox, pool_w, fox_w, mem_w):
    ts = x_ref.shape[1]
    s_idx = pl.program_id(1)

    @pl.when(s_idx == 0)
    def _():
        fcarry_ref[...] = jnp.zeros_like(fcarry_ref)
        halo_ref[...] = jnp.zeros_like(halo_ref)

    x = x_ref[0]
    h = (x * lax.rsqrt(jnp.mean(x * x, axis=-1, keepdims=True) + EPS) * ng_ref[...]).astype(BF16)

    def proj_t(lo, n):
        return _dot_nt(wT_ref[lo:lo + n, :], h)

    sec = proj_t(offs["ua"], 2 * pool_w + BF16_ROWS)
    u = sec[0:pool_w]
    ga = sec[pool_w:2 * pool_w]
    z = sec[2 * pool_w:2 * pool_w + BF16_ROWS] + bf_ref[...]

    logf = _log_sigmoid(z)
    hi = logf.astype(BF16)
    r1 = logf - hi.astype(F32)
    mid = r1.astype(BF16)
    lo = (r1 - mid.astype(F32)).astype(BF16)
    cs = _dot(jnp.concatenate([hi, mid, lo], axis=0), tri_ref[...])
    fcum = (cs[0:BF16_ROWS] + cs[BF16_ROWS:2 * BF16_ROWS] + cs[2 * BF16_ROWS:3 * BF16_ROWS]
            + fcarry_ref[:, LANES - 1:LANES])
    fcarry_ref[...] = fcum[:, ts - LANES:]
    f_hi = fcum.astype(BF16).astype(F32)
    f_r = fcum - f_hi
    f_mid = f_r.astype(BF16).astype(F32)
    f_lo = (f_r - f_mid).astype(BF16).astype(F32)

    arow = lax.broadcasted_iota(jnp.int32, (AUG_ROWS, ts), 0)

    def aug_q(hd):
        a, b, c = f_hi[hd:hd + 1], f_mid[hd:hd + 1], f_lo[hd:hd + 1]
        return jnp.where(arow == 0, a, jnp.where(arow == 1, b, jnp.where(
            arow == 2, c, jnp.where(arow < 6, 1.0, 0.0))))

    def aug_k(hd):
        a, b, c = f_hi[hd:hd + 1], f_mid[hd:hd + 1], f_lo[hd:hd + 1]
        return jnp.where(arow < 3, 1.0, jnp.where(arow == 3, -a, jnp.where(
            arow == 4, -b, jnp.where(arow == 5, -c, 0.0))))

    uext = jnp.concatenate([halo_ref[...], u], axis=1)
    halo_ref[...] = u[:, ts - LANES:]
    pos1 = (s_idx * ts + lax.broadcasted_iota(jnp.int32, (1, ts), 1) + 1).astype(F32)
    gd = POOL_GROUP_DIM
    acc = uext
    pooled = []
    shift = 1
    for g, w in enumerate(POOL_WINDOWS):
        while shift < w:
            acc = acc + pltpu.roll(acc, shift, 1)
            shift *= 2
        pooled.append(acc[0:gd, LANES:] / jnp.minimum(pos1, float(w)))
        acc = acc[gd:]
    d = (jnp.concatenate(pooled, axis=0) - u).astype(BF16)
    ya = _dot(wp_ref[...], d) * ps_ref[...]
    mam_ref[0, 0:pool_w, :] = (ya * _silu(ga)).astype(BF16)

    zero_rows = jnp.zeros((QK_ROWS - HEAD_DIM - AUG_ROWS, ts), F32)
    qT = proj_t(offs["q"], fox_w)
    for hd in range(n_fox):
        qh = qT[hd * HEAD_DIM:(hd + 1) * HEAD_DIM]
        qn = qh * _head_rms_scale(qh) * gq_ref[...] * SCALE
        qT_ref[0, hd, 0:HEAD_DIM, :] = qn.astype(BF16)
        qT_ref[0, hd, HEAD_DIM:HEAD_DIM + AUG_ROWS, :] = aug_q(hd).astype(BF16)
        qT_ref[0, hd, HEAD_DIM + AUG_ROWS:, :] = zero_rows.astype(BF16)

    kT = proj_t(offs["k"], fox_w)
    for hd in range(n_fox):
        kh = kT[hd * HEAD_DIM:(hd + 1) * HEAD_DIM]
        kn = kh * _head_rms_scale(kh) * gk_ref[...]
        kfull = jnp.concatenate([kn, aug_k(hd), zero_rows], axis=0)
        kp_ref[0, hd] = kfull.T.astype(BF16)

    vT = proj_t(offs["v"], fox_w)
    ones_blk = _ones_row_block(ts)
    tk = vT_ref.shape[4]
    for hd in range(n_fox):
        va = jnp.concatenate([vT[hd * HEAD_DIM:(hd + 1) * HEAD_DIM], ones_blk], axis=0).astype(BF16)
        for c in range(ts // tk):
            vT_ref[0, hd, c] = va[:, c * tk:(c + 1) * tk]

    gbT_ref[0] = proj_t(offs["gb"], fox_w)

    sec = proj_t(offs["qm"], 2 * mem_w)
    qm = sec[0:mem_w]
    gm = sec[mem_w:2 * mem_w]
    zero_pad = jnp.zeros((QK_ROWS - HEAD_DIM, ts), F32)
    for hm in range(MEM_HEADS):
        qh = qm[hm * HEAD_DIM:(hm + 1) * HEAD_DIM]
        qn = qh * _head_rms_scale(qh) * gmq_ref[...] * SCALE
        qpad = jnp.concatenate([qn, zero_pad], axis=0).astype(BF16)
        lg = _dot(km_ref[0, hm], qpad)
        p = jnp.exp(lg - jnp.max(lg, axis=0, keepdims=True)).astype(BF16)
        yv = _dot(vm_ref[0, hm], p)
        y = yv[0:HEAD_DIM] / yv[HEAD_DIM:HEAD_DIM + 1]
        mm = y * _silu(gm[hm * HEAD_DIM:(hm + 1) * HEAD_DIM])
        mam_ref[0, pool_w + hm * HEAD_DIM:pool_w + (hm + 1) * HEAD_DIM, :] = mm.astype(BF16)


def _fox_kernel(qT_ref, kp_ref, vT_ref, yT_ref, acc_ref):
    tq = qT_ref.shape[3]
    tk = vT_ref.shape[4]
    i = pl.program_id(2)
    q = qT_ref[0, 0]
    n_diag = tq // tk
    n_full = i * n_diag

    acc_ref[...] = jnp.zeros_like(acc_ref)
    rel = (lax.broadcasted_iota(jnp.int32, (tk, tq), 0)
           - lax.broadcasted_iota(jnp.int32, (tk, tq), 1))

    def step(j, m, masked):
        start = pl.multiple_of(j * tk, tk)
        kt = kp_ref[0, 0, pl.ds(start, tk), :]
        s = _dot(kt, q)
        if masked:
            s = jnp.where(rel <= i * tq - j * tk, s, MASK_VALUE)
        m_new = jnp.maximum(m, jnp.max(s, axis=0, keepdims=True))
        p = jnp.exp(s - m_new).astype(BF16)
        alpha = jnp.exp(m - m_new)
        acc_ref[...] = alpha * acc_ref[...] + _dot(vT_ref[0, 0, j], p)
        return m_new

    m = jnp.full((1, tq), MASK_VALUE, F32)
    m = lax.fori_loop(0, n_full, lambda j, mm: step(j, mm, False), m)
    for c in range(n_diag):
        m = step(n_full + c, m, True)
    acc = acc_ref[...]
    yT_ref[0] = acc[0:HEAD_DIM] / acc[HEAD_DIM:HEAD_DIM + 1]


def _out_kernel(x_ref, yT_ref, gbT_ref, mam_ref, woT_ref, o_ref, *, pool_w):
    yb = (yT_ref[0] * _silu(gbT_ref[0])).astype(BF16)
    mam = mam_ref[0]
    mixT = jnp.concatenate([mam[0:pool_w], yb, mam[pool_w:]], axis=0)
    outT = _dot(woT_ref[...], mixT)
    o_ref[0] = x_ref[0] + outT.T


def _col(v, rows=None):
    v = v.astype(F32)
    if rows is not None and rows > v.shape[0]:
        v = jnp.concatenate([v, jnp.zeros((rows - v.shape[0],), F32)])
    return v[:, None]


def _params(sem):
    return pltpu.CompilerParams(dimension_semantics=sem, vmem_limit_bytes=VMEM_LIMIT_BYTES)


def _layer(x, mem, norm_g, w_in, b_f, w_pool, pool_scale, fox_q_g, fox_k_g,
           mem_norm_g, w_mem_kv, mem_q_g, mem_k_g, w_out):
    bsz, seq, dm = x.shape
    n_mem = mem.shape[1]
    n_fox = b_f.shape[0]
    pool_w = pool_scale.shape[0]
    fox_w = n_fox * HEAD_DIM
    mem_w = MEM_HEADS * HEAD_DIM
    assert n_fox <= BF16_ROWS and pool_w == len(POOL_WINDOWS) * POOL_GROUP_DIM
    assert w_in.shape[1] == 2 * pool_w + 4 * fox_w + n_fox + 2 * mem_w
    assert w_out.shape[0] == pool_w + fox_w + mem_w
    ts, tq, tk = SEQ_TILE, Q_TILE, KV_TILE
    assert seq % ts == 0 and seq % tq == 0 and tq % tk == 0 and ts % tk == 0

    sizes = (pool_w, pool_w, fox_w, fox_w, fox_w, n_fox, fox_w, mem_w, mem_w)
    parts, off = [], 0
    for w in sizes:
        parts.append(w_in[:, off:off + w])
        off += w
    w_ua, w_ga, w_q, w_k, w_v, w_f, w_gb, w_qm, w_gm = parts
    w_f = jnp.concatenate([w_f, jnp.zeros((dm, BF16_ROWS - n_fox), w_in.dtype)], axis=1)
    order = [("ua", w_ua), ("ga", w_ga), ("f", w_f), ("q", w_q), ("k", w_k), ("v", w_v),
             ("gb", w_gb), ("qm", w_qm), ("gm", w_gm)]
    offs, off = {}, 0
    for name, w in order:
        offs[name] = off
        off += w.shape[1]
    wT = jnp.concatenate([w for _, w in order], axis=1).T.astype(BF16)
    n_rows = wT.shape[0]
    tri = (lax.broadcasted_iota(jnp.int32, (ts, ts), 0)
           <= lax.broadcasted_iota(jnp.int32, (ts, ts), 1)).astype(BF16)
    wp_bd = jax.scipy.linalg.block_diag(*[w_pool[g].T for g in range(len(POOL_WINDOWS))]).astype(BF16)
    wkvT = w_mem_kv.T.astype(BF16)
    woT = w_out.T.astype(BF16)

    km, vm = pl.pallas_call(
        _mem_kv_kernel,
        grid=(bsz,),
        in_specs=[
            pl.BlockSpec((1, n_mem, dm), lambda b: (b, 0, 0)),
            pl.BlockSpec((1, dm), lambda b: (0, 0)),
            pl.BlockSpec((2 * mem_w, dm), lambda b: (0, 0)),
            pl.BlockSpec((HEAD_DIM, 1), lambda b: (0, 0)),
        ],
        out_specs=[
            pl.BlockSpec((1, MEM_HEADS, n_mem, QK_ROWS), lambda b: (b, 0, 0, 0)),
            pl.BlockSpec((1, MEM_HEADS, V_ROWS, n_mem), lambda b: (b, 0, 0, 0)),
        ],
        out_shape=[
            jax.ShapeDtypeStruct((bsz, MEM_HEADS, n_mem, QK_ROWS), BF16),
            jax.ShapeDtypeStruct((bsz, MEM_HEADS, V_ROWS, n_mem), BF16),
        ],
        compiler_params=_params(("arbitrary",)),
        name="mem_kv",
    )(mem, mem_norm_g[None, :], wkvT, _col(mem_k_g))

    const2 = lambda b, s: (0, 0)
    qT, kp, vT, gbT, mam = pl.pallas_call(
        functools.partial(_proj_kernel, offs=offs, n_fox=n_fox, pool_w=pool_w,
                          fox_w=fox_w, mem_w=mem_w),
        grid=(bsz, seq // ts),
        in_specs=[
            pl.BlockSpec((1, ts, dm), lambda b, s: (b, s, 0)),
            pl.BlockSpec((1, dm), const2),
            pl.BlockSpec((n_rows, dm), const2),
            pl.BlockSpec((BF16_ROWS, 1), const2),
            pl.BlockSpec((ts, ts), const2),
            pl.BlockSpec((HEAD_DIM, 1), const2),
            pl.BlockSpec((HEAD_DIM, 1), const2),
            pl.BlockSpec((pool_w, pool_w), const2),
            pl.BlockSpec((pool_w, 1), const2),
            pl.BlockSpec((1, MEM_HEADS, n_mem, QK_ROWS), lambda b, s: (b, 0, 0, 0)),
            pl.BlockSpec((1, MEM_HEADS, V_ROWS, n_mem), lambda b, s: (b, 0, 0, 0)),
            pl.BlockSpec((HEAD_DIM, 1), const2),
        ],
        out_specs=[
            pl.BlockSpec((1, n_fox, QK_ROWS, ts), lambda b, s: (b, 0, 0, s)),
            pl.BlockSpec((1, n_fox, ts, QK_ROWS), lambda b, s: (b, 0, s, 0)),
            pl.BlockSpec((1, n_fox, ts // tk, V_ROWS, tk), lambda b, s: (b, 0, s, 0, 0)),
            pl.BlockSpec((1, fox_w, ts), lambda b, s: (b, 0, s)),
            pl.BlockSpec((1, pool_w + mem_w, ts), lambda b, s: (b, 0, s)),
        ],
        out_shape=[
            jax.ShapeDtypeStruct((bsz, n_fox, QK_ROWS, seq), BF16),
            jax.ShapeDtypeStruct((bsz, n_fox, seq, QK_ROWS), BF16),
            jax.ShapeDtypeStruct((bsz, n_fox, seq // tk, V_ROWS, tk), BF16),
            jax.ShapeDtypeStruct((bsz, fox_w, seq), F32),
            jax.ShapeDtypeStruct((bsz, pool_w + mem_w, seq), BF16),
        ],
        scratch_shapes=[
            pltpu.VMEM((BF16_ROWS, LANES), F32),
            pltpu.VMEM((pool_w, LANES), F32),
        ],
        compiler_params=_params(("arbitrary", "arbitrary")),
        name="proj",
    )(x, norm_g[None, :], wT, _col(b_f, BF16_ROWS), tri, _col(fox_q_g), _col(fox_k_g),
      wp_bd, _col(pool_scale), km, vm, _col(mem_q_g))

    yT = pl.pallas_call(
        _fox_kernel,
        grid=(bsz, n_fox, seq // tq),
        in_specs=[
            pl.BlockSpec((1, 1, QK_ROWS, tq), lambda b, h, i: (b, h, 0, i)),
            pl.BlockSpec((1, 1, seq, QK_ROWS), lambda b, h, i: (b, h, 0, 0)),
            pl.BlockSpec((1, 1, seq // tk, V_ROWS, tk), lambda b, h, i: (b, h, 0, 0, 0)),
        ],
        out_specs=pl.BlockSpec((1, HEAD_DIM, tq), lambda b, h, i: (b, h, i)),
        out_shape=jax.ShapeDtypeStruct((bsz, fox_w, seq), F32),
        scratch_shapes=[pltpu.VMEM((V_ROWS, tq), F32)],
        compiler_params=_params(("arbitrary", "arbitrary", "arbitrary")),
        name="fox",
    )(qT, kp, vT)

    return pl.pallas_call(
        functools.partial(_out_kernel, pool_w=pool_w),
        grid=(bsz, seq // ts),
        in_specs=[
            pl.BlockSpec((1, ts, dm), lambda b, s: (b, s, 0)),
            pl.BlockSpec((1, fox_w, ts), lambda b, s: (b, 0, s)),
            pl.BlockSpec((1, fox_w, ts), lambda b, s: (b, 0, s)),
            pl.BlockSpec((1, pool_w + mem_w, ts), lambda b, s: (b, 0, s)),
            pl.BlockSpec((dm, pool_w + fox_w + mem_w), const2),
        ],
        out_specs=pl.BlockSpec((1, ts, dm), lambda b, s: (b, s, 0)),
        out_shape=jax.ShapeDtypeStruct((bsz, seq, dm), x.dtype),
        compiler_params=_params(("arbitrary", "arbitrary")),
        name="out_proj",
    )(x, yT, gbT, mam, woT)


def kernel(x, mem, norm_g, w_in, b_f, w_pool, pool_scale, fox_q_g, fox_k_g, mem_norm_g,
           w_mem_kv, mem_q_g, mem_k_g, w_out):
    for l in range(norm_g.shape[0]):
        x = _layer(x, mem, norm_g[l], w_in[l], b_f[l], w_pool[l], pool_scale[l], fox_q_g[l],
                   fox_k_g[l], mem_norm_g[l], w_mem_kv[l], mem_q_g[l], mem_k_g[l], w_out[l])
    return x
```

```python
import functools
import math

import jax
import jax.numpy as jnp
from jax import lax
from jax.experimental import pallas as pl
from jax.experimental.pallas import tpu as pltpu

F32 = jnp.float32
BF16 = jnp.bfloat16

HEAD_DIM = 64
EPS = 1e-6
POOL_WINDOWS = (2, 4, 8, 16)
POOL_GROUP_DIM = 64
MEM_HEADS = 4
SCALE = 1.0 / math.sqrt(HEAD_DIM)
MASK_VALUE = -1e30

LANES = 128
BF16_ROWS = 16
QK_ROWS = 128
AUG_ROWS = BF16_ROWS
V_ROWS = HEAD_DIM + BF16_ROWS

SEQ_TILE = 512
Q_TILE = 512
KV_TILE = 256
HEADS_PER_STEP = 8
VMEM_LIMIT_BYTES = 56 * 1024 * 1024

NT_DIMS = (((1,), (1,)), ((), ()))


def _dot(a, b):
    return jnp.dot(a, b, preferred_element_type=F32)


def _dot_nt(a, b):
    return lax.dot_general(a, b, NT_DIMS, preferred_element_type=F32)


def _silu(g):
    return g * jax.nn.sigmoid(g)


def _log_sigmoid(z):
    return jnp.minimum(z, 0.0) - jnp.log1p(jnp.exp(-jnp.abs(z)))


def _head_rms_scale(t):
    return lax.rsqrt(jnp.mean(t * t, axis=0, keepdims=True) + EPS)


def _ones_row_block(n):
    row = lax.broadcasted_iota(jnp.int32, (BF16_ROWS, n), 0)
    return (row == 0).astype(F32)


def _mem_kv_kernel(mem_ref, g_ref, wkvT_ref, gk_ref, km_ref, vm_ref):
    x = mem_ref[0]
    n_mem = x.shape[0]
    h = (x * lax.rsqrt(jnp.mean(x * x, axis=-1, keepdims=True) + EPS) * g_ref[...]).astype(BF16)
    kvT = _dot_nt(wkvT_ref[...], h)
    width = MEM_HEADS * HEAD_DIM
    ones_blk = _ones_row_block(n_mem)
    zero_pad = jnp.zeros((QK_ROWS - HEAD_DIM, n_mem), F32)
    for hm in range(MEM_HEADS):
        kh = kvT[hm * HEAD_DIM:(hm + 1) * HEAD_DIM]
        kn = kh * _head_rms_scale(kh) * gk_ref[...]
        km_ref[0, hm] = jnp.concatenate([kn, zero_pad], axis=0).T.astype(BF16)
        vh = kvT[width + hm * HEAD_DIM:width + (hm + 1) * HEAD_DIM]
        vm_ref[0, hm] = jnp.concatenate([vh, ones_blk], axis=0).astype(BF16)


def _proj_kernel(x_ref, ng_ref, wT_ref, bf_ref, tri_ref, gq_ref, gk_ref, wp_ref, ps_ref,
                 km_ref, vm_ref, gmq_ref,
                 qT_ref, kp_ref, vT_ref, gbT_ref, mam_ref,
                 fcarry_ref, halo_ref, *, offs, n_fox, pool_w, fox_w, mem_w):
    ts = x_ref.shape[1]
    s_idx = pl.program_id(1)

    @pl.when(s_idx == 0)
    def _():
        fcarry_ref[...] = jnp.zeros_like(fcarry_ref)
        halo_ref[...] = jnp.zeros_like(halo_ref)

    x = x_ref[0]
    h = (x * lax.rsqrt(jnp.mean(x * x, axis=-1, keepdims=True) + EPS) * ng_ref[...]).astype(BF16)

    def proj_t(lo, n):
        return _dot_nt(wT_ref[lo:lo + n, :], h)

    sec = proj_t(offs["ua"], 2 * pool_w + BF16_ROWS)
    u = sec[0:pool_w]
    ga = sec[pool_w:2 * pool_w]
    z = sec[2 * pool_w:2 * pool_w + BF16_ROWS] + bf_ref[...]

    logf = _log_sigmoid(z)
    hi = logf.astype(BF16)
    r1 = logf - hi.astype(F32)
    mid = r1.astype(BF16)
    lo = (r1 - mid.astype(F32)).astype(BF16)
    cs = _dot(jnp.concatenate([hi, mid, lo], axis=0), tri_ref[...])
    fcum = (cs[0:BF16_ROWS] + cs[BF16_ROWS:2 * BF16_ROWS] + cs[2 * BF16_ROWS:3 * BF16_ROWS]
            + fcarry_ref[:, LANES - 1:LANES])
    fcarry_ref[...] = fcum[:, ts - LANES:]
    f_hi = fcum.astype(BF16).astype(F32)
    f_r = fcum - f_hi
    f_mid = f_r.astype(BF16).astype(F32)
    f_lo = (f_r - f_mid).astype(BF16).astype(F32)

    arow = lax.broadcasted_iota(jnp.int32, (AUG_ROWS, ts), 0)

    def aug_q(hd):
        a, b, c = f_hi[hd:hd + 1], f_mid[hd:hd + 1], f_lo[hd:hd + 1]
        return jnp.where(arow == 0, a, jnp.where(arow == 1, b, jnp.where(
            arow == 2, c, jnp.where(arow < 6, 1.0, 0.0))))

    def aug_k(hd):
        a, b, c = f_hi[hd:hd + 1], f_mid[hd:hd + 1], f_lo[hd:hd + 1]
        return jnp.where(arow < 3, 1.0, jnp.where(arow == 3, -a, jnp.where(
            arow == 4, -b, jnp.where(arow == 5, -c, 0.0))))

    uext = jnp.concatenate([halo_ref[...], u], axis=1)
    halo_ref[...] = u[:, ts - LANES:]
    pos1 = (s_idx * ts + lax.broadcasted_iota(jnp.int32, (1, ts), 1) + 1).astype(F32)
    gd = POOL_GROUP_DIM
    acc = uext
    pooled = []
    shift = 1
    for g, w in enumerate(POOL_WINDOWS):
        while shift < w:
            acc = acc + pltpu.roll(acc, shift, 1)
            shift *= 2
        pooled.append(acc[0:gd, LANES:] / jnp.minimum(pos1, float(w)))
        acc = acc[gd:]
    d = (jnp.concatenate(pooled, axis=0) - u).astype(BF16)
    ya = _dot(wp_ref[...], d) * ps_ref[...]
    mam_ref[0, 0:pool_w, :] = (ya * _silu(ga)).astype(BF16)

    zero_rows = jnp.zeros((QK_ROWS - HEAD_DIM - AUG_ROWS, ts), F32)
    qT = proj_t(offs["q"], fox_w)
    for hd in range(n_fox):
        qh = qT[hd * HEAD_DIM:(hd + 1) * HEAD_DIM]
        qn = qh * _head_rms_scale(qh) * gq_ref[...] * SCALE
        qT_ref[0, hd, 0:HEAD_DIM, :] = qn.astype(BF16)
        qT_ref[0, hd, HEAD_DIM:HEAD_DIM + AUG_ROWS, :] = aug_q(hd).astype(BF16)
        qT_ref[0, hd, HEAD_DIM + AUG_ROWS:, :] = zero_rows.astype(BF16)

    kT = proj_t(offs["k"], fox_w)
    for hd in range(n_fox):
        kh = kT[hd * HEAD_DIM:(hd + 1) * HEAD_DIM]
        kn = kh * _head_rms_scale(kh) * gk_ref[...]
        kfull = jnp.concatenate([kn, aug_k(hd), zero_rows], axis=0)
        kp_ref[0, hd] = kfull.T.astype(BF16)

    vT = proj_t(offs["v"], fox_w)
    ones_blk = _ones_row_block(ts)
    tk = vT_ref.shape[4]
    for hd in range(n_fox):
        va = jnp.concatenate([vT[hd * HEAD_DIM:(hd + 1) * HEAD_DIM], ones_blk], axis=0).astype(BF16)
        for c in range(ts // tk):
            vT_ref[0, hd, c] = va[:, c * tk:(c + 1) * tk]

    gbT_ref[0] = proj_t(offs["gb"], fox_w)

    sec = proj_t(offs["qm"], 2 * mem_w)
    qm = sec[0:mem_w]
    gm = sec[mem_w:2 * mem_w]
    zero_pad = jnp.zeros((QK_ROWS - HEAD_DIM, ts), F32)
    for hm in range(MEM_HEADS):
        qh = qm[hm * HEAD_DIM:(hm + 1) * HEAD_DIM]
        qn = qh * _head_rms_scale(qh) * gmq_ref[...] * SCALE
        qpad = jnp.concatenate([qn, zero_pad], axis=0).astype(BF16)
        lg = _dot(km_ref[0, hm], qpad)
        p = jnp.exp(lg - jnp.max(lg, axis=0, keepdims=True)).astype(BF16)
        yv = _dot(vm_ref[0, hm], p)
        y = yv[0:HEAD_DIM] / yv[HEAD_DIM:HEAD_DIM + 1]
        mm = y * _silu(gm[hm * HEAD_DIM:(hm + 1) * HEAD_DIM])
        mam_ref[0, pool_w + hm * HEAD_DIM:pool_w + (hm + 1) * HEAD_DIM, :] = mm.astype(BF16)


def _fox_kernel(qT_ref, kp_ref, vT_ref, yT_ref, acc_ref):
    hg = qT_ref.shape[1]
    tq = qT_ref.shape[3]
    tk = vT_ref.shape[4]
    i = pl.program_id(2)
    n_diag = tq // tk
    n_full = i * n_diag

    acc_ref[...] = jnp.zeros_like(acc_ref)
    rel = (lax.broadcasted_iota(jnp.int32, (tk, tq), 0)
           - lax.broadcasted_iota(jnp.int32, (tk, tq), 1))

    def step(j, ms, masked):
        start = pl.multiple_of(j * tk, tk)
        scores = [_dot(kp_ref[0, hd, pl.ds(start, tk), :], qT_ref[0, hd]) for hd in range(hg)]
        out = []
        for hd in range(hg):
            s = scores[hd]
            if masked:
                s = jnp.where(rel <= i * tq - j * tk, s, MASK_VALUE)
            m_new = jnp.maximum(ms[hd], jnp.max(s, axis=0, keepdims=True))
            p = jnp.exp(s - m_new).astype(BF16)
            alpha = jnp.exp(ms[hd] - m_new)
            acc_ref[hd] = alpha * acc_ref[hd] + _dot(vT_ref[0, hd, j], p)
            out.append(m_new)
        return tuple(out)

    ms = tuple(jnp.full((1, tq), MASK_VALUE, F32) for _ in range(hg))
    ms = lax.fori_loop(0, n_full, lambda j, c: step(j, c, False), ms)
    for c in range(n_diag):
        ms = step(n_full + c, ms, True)
    for hd in range(hg):
        acc = acc_ref[hd]
        yT_ref[0, hd * HEAD_DIM:(hd + 1) * HEAD_DIM, :] = acc[0:HEAD_DIM] / acc[HEAD_DIM:HEAD_DIM + 1]


def _out_kernel(x_ref, yT_ref, gbT_ref, mam_ref, woT_ref, o_ref, *, pool_w):
    yb = (yT_ref[0] * _silu(gbT_ref[0])).astype(BF16)
    mam = mam_ref[0]
    mixT = jnp.concatenate([mam[0:pool_w], yb, mam[pool_w:]], axis=0)
    outT = _dot(woT_ref[...], mixT)
    o_ref[0] = x_ref[0] + outT.T


def _col(v, rows=None):
    v = v.astype(F32)
    if rows is not None and rows > v.shape[0]:
        v = jnp.concatenate([v, jnp.zeros((rows - v.shape[0],), F32)])
    return v[:, None]


def _params(sem):
    return pltpu.CompilerParams(dimension_semantics=sem, vmem_limit_bytes=VMEM_LIMIT_BYTES)


def _layer(x, mem, norm_g, w_in, b_f, w_pool, pool_scale, fox_q_g, fox_k_g,
           mem_norm_g, w_mem_kv, mem_q_g, mem_k_g, w_out):
    bsz, seq, dm = x.shape
    n_mem = mem.shape[1]
    n_fox = b_f.shape[0]
    pool_w = pool_scale.shape[0]
    fox_w = n_fox * HEAD_DIM
    mem_w = MEM_HEADS * HEAD_DIM
    assert n_fox <= BF16_ROWS and pool_w == len(POOL_WINDOWS) * POOL_GROUP_DIM
    assert w_in.shape[1] == 2 * pool_w + 4 * fox_w + n_fox + 2 * mem_w
    assert w_out.shape[0] == pool_w + fox_w + mem_w
    ts, tq, tk = SEQ_TILE, Q_TILE, KV_TILE
    hg = math.gcd(HEADS_PER_STEP, n_fox)
    assert seq % ts == 0 and seq % tq == 0 and tq % tk == 0 and ts % tk == 0

    sizes = (pool_w, pool_w, fox_w, fox_w, fox_w, n_fox, fox_w, mem_w, mem_w)
    parts, off = [], 0
    for w in sizes:
        parts.append(w_in[:, off:off + w])
        off += w
    w_ua, w_ga, w_q, w_k, w_v, w_f, w_gb, w_qm, w_gm = parts
    w_f = jnp.concatenate([w_f, jnp.zeros((dm, BF16_ROWS - n_fox), w_in.dtype)], axis=1)
    order = [("ua", w_ua), ("ga", w_ga), ("f", w_f), ("q", w_q), ("k", w_k), ("v", w_v),
             ("gb", w_gb), ("qm", w_qm), ("gm", w_gm)]
    offs, off = {}, 0
    for name, w in order:
        offs[name] = off
        off += w.shape[1]
    wT = jnp.concatenate([w for _, w in order], axis=1).T.astype(BF16)
    n_rows = wT.shape[0]
    tri = (lax.broadcasted_iota(jnp.int32, (ts, ts), 0)
           <= lax.broadcasted_iota(jnp.int32, (ts, ts), 1)).astype(BF16)
    wp_bd = jax.scipy.linalg.block_diag(*[w_pool[g].T for g in range(len(POOL_WINDOWS))]).astype(BF16)
    wkvT = w_mem_kv.T.astype(BF16)
    woT = w_out.T.astype(BF16)

    km, vm = pl.pallas_call(
        _mem_kv_kernel,
        grid=(bsz,),
        in_specs=[
            pl.BlockSpec((1, n_mem, dm), lambda b: (b, 0, 0)),
            pl.BlockSpec((1, dm), lambda b: (0, 0)),
            pl.BlockSpec((2 * mem_w, dm), lambda b: (0, 0)),
            pl.BlockSpec((HEAD_DIM, 1), lambda b: (0, 0)),
        ],
        out_specs=[
            pl.BlockSpec((1, MEM_HEADS, n_mem, QK_ROWS), lambda b: (b, 0, 0, 0)),
            pl.BlockSpec((1, MEM_HEADS, V_ROWS, n_mem), lambda b: (b, 0, 0, 0)),
        ],
        out_shape=[
            jax.ShapeDtypeStruct((bsz, MEM_HEADS, n_mem, QK_ROWS), BF16),
            jax.ShapeDtypeStruct((bsz, MEM_HEADS, V_ROWS, n_mem), BF16),
        ],
        compiler_params=_params(("arbitrary",)),
        name="mem_kv",
    )(mem, mem_norm_g[None, :], wkvT, _col(mem_k_g))

    const2 = lambda b, s: (0, 0)
    qT, kp, vT, gbT, mam = pl.pallas_call(
        functools.partial(_proj_kernel, offs=offs, n_fox=n_fox, pool_w=pool_w,
                          fox_w=fox_w, mem_w=mem_w),
        grid=(bsz, seq // ts),
        in_specs=[
            pl.BlockSpec((1, ts, dm), lambda b, s: (b, s, 0)),
            pl.BlockSpec((1, dm), const2),
            pl.BlockSpec((n_rows, dm), const2),
            pl.BlockSpec((BF16_ROWS, 1), const2),
            pl.BlockSpec((ts, ts), const2),
            pl.BlockSpec((HEAD_DIM, 1), const2),
            pl.BlockSpec((HEAD_DIM, 1), const2),
            pl.BlockSpec((pool_w, pool_w), const2),
            pl.BlockSpec((pool_w, 1), const2),
            pl.BlockSpec((1, MEM_HEADS, n_mem, QK_ROWS), lambda b, s: (b, 0, 0, 0)),
            pl.BlockSpec((1, MEM_HEADS, V_ROWS, n_mem), lambda b, s: (b, 0, 0, 0)),
            pl.BlockSpec((HEAD_DIM, 1), const2),
        ],
        out_specs=[
            pl.BlockSpec((1, n_fox, QK_ROWS, ts), lambda b, s: (b, 0, 0, s)),
            pl.BlockSpec((1, n_fox, ts, QK_ROWS), lambda b, s: (b, 0, s, 0)),
            pl.BlockSpec((1, n_fox, ts // tk, V_ROWS, tk), lambda b, s: (b, 0, s, 0, 0)),
            pl.BlockSpec((1, fox_w, ts), lambda b, s: (b, 0, s)),
            pl.BlockSpec((1, pool_w + mem_w, ts), lambda b, s: (b, 0, s)),
        ],
        out_shape=[
            jax.ShapeDtypeStruct((bsz, n_fox, QK_ROWS, seq), BF16),
            jax.ShapeDtypeStruct((bsz, n_fox, seq, QK_ROWS), BF16),
            jax.ShapeDtypeStruct((bsz, n_fox, seq // tk, V_ROWS, tk), BF16),
            jax.ShapeDtypeStruct((bsz, fox_w, seq), F32),
            jax.ShapeDtypeStruct((bsz, pool_w + mem_w, seq), BF16),
        ],
        scratch_shapes=[
            pltpu.VMEM((BF16_ROWS, LANES), F32),
            pltpu.VMEM((pool_w, LANES), F32),
        ],
        compiler_params=_params(("arbitrary", "arbitrary")),
        name="proj",
    )(x, norm_g[None, :], wT, _col(b_f, BF16_ROWS), tri, _col(fox_q_g), _col(fox_k_g),
      wp_bd, _col(pool_scale), km, vm, _col(mem_q_g))

    yT = pl.pallas_call(
        _fox_kernel,
        grid=(bsz, n_fox // hg, seq // tq),
        in_specs=[
            pl.BlockSpec((1, hg, QK_ROWS, tq), lambda b, h, i: (b, h, 0, i)),
            pl.BlockSpec((1, hg, seq, QK_ROWS), lambda b, h, i: (b, h, 0, 0)),
            pl.BlockSpec((1, hg, seq // tk, V_ROWS, tk), lambda b, h, i: (b, h, 0, 0, 0)),
        ],
        out_specs=pl.BlockSpec((1, hg * HEAD_DIM, tq), lambda b, h, i: (b, h, i)),
        out_shape=jax.ShapeDtypeStruct((bsz, fox_w, seq), F32),
        scratch_shapes=[pltpu.VMEM((hg, V_ROWS, tq), F32)],
        compiler_params=_params(("arbitrary", "arbitrary", "arbitrary")),
        name="fox",
    )(qT, kp, vT)

    return pl.pallas_call(
        functools.partial(_out_kernel, pool_w=pool_w),
        grid=(bsz, seq // ts),
        in_specs=[
            pl.BlockSpec((1, ts, dm), lambda b, s: (b, s, 0)),
            pl.BlockSpec((1, fox_w, ts), lambda b, s: (b, 0, s)),
            pl.BlockSpec((1, fox_w, ts), lambda b, s: (b, 0, s)),
            pl.BlockSpec((1, pool_w + mem_w, ts), lambda b, s: (b, 0, s)),
            pl.BlockSpec((dm, pool_w + fox_w + mem_w), const2),
        ],
        out_specs=pl.BlockSpec((1, ts, dm), lambda b, s: (b, s, 0)),
        out_shape=jax.ShapeDtypeStruct((bsz, seq, dm), x.dtype),
        compiler_params=_params(("arbitrary", "arbitrary")),
        name="out_proj",
    )(x, yT, gbT, mam, woT)


def kernel(x, mem, norm_g, w_in, b_f, w_pool, pool_scale, fox_q_g, fox_k_g, mem_norm_g,
           w_mem_kv, mem_q_g, mem_k_g, w_out):
    for l in range(norm_g.shape[0]):
        x = _layer(x, mem, norm_g[l], w_in[l], b_f[l], w_pool[l], pool_scale[l], fox_q_g[l],
                   fox_k_g[l], mem_norm_g[l], w_mem_kv[l], mem_q_g[l], mem_k_g[l], w_out[l])
    return x
```

```python
import functools
import math

import jax
import jax.numpy as jnp
from jax import lax
from jax.experimental import pallas as pl
from jax.experimental.pallas import tpu as pltpu

F32 = jnp.float32
BF16 = jnp.bfloat16

HEAD_DIM = 64
EPS = 1e-6
POOL_WINDOWS = (2, 4, 8, 16)
POOL_GROUP_DIM = 64
MEM_HEADS = 4
SCALE = 1.0 / math.sqrt(HEAD_DIM)
LOG2E = math.log2(math.e)
MASK_VALUE = -1e30

LANES = 128
BF16_ROWS = 16
QK_ROWS = 128
AUG_ROWS = BF16_ROWS
V_ROWS = HEAD_DIM + BF16_ROWS

SEQ_TILE = 512
Q_TILE = 512
KV_TILE = 256
HEADS_PER_STEP = 8
VMEM_LIMIT_BYTES = 56 * 1024 * 1024

NT_DIMS = (((1,), (1,)), ((), ()))


def _dot(a, b):
    return jnp.dot(a, b, preferred_element_type=F32)


def _dot_nt(a, b):
    return lax.dot_general(a, b, NT_DIMS, preferred_element_type=F32)


def _silu(g):
    return g * jax.nn.sigmoid(g)


def _log_sigmoid(z):
    return jnp.minimum(z, 0.0) - jnp.log1p(jnp.exp(-jnp.abs(z)))


def _head_rms_scale(t):
    return lax.rsqrt(jnp.mean(t * t, axis=0, keepdims=True) + EPS)


def _ones_row_block(n):
    row = lax.broadcasted_iota(jnp.int32, (BF16_ROWS, n), 0)
    return (row == 0).astype(F32)


def _mem_kv_kernel(mem_ref, g_ref, wkvT_ref, gk_ref, km_ref, vm_ref):
    x = mem_ref[0]
    n_mem = x.shape[0]
    h = (x * lax.rsqrt(jnp.mean(x * x, axis=-1, keepdims=True) + EPS) * g_ref[...]).astype(BF16)
    kvT = _dot_nt(wkvT_ref[...], h)
    width = MEM_HEADS * HEAD_DIM
    ones_blk = _ones_row_block(n_mem)
    zero_pad = jnp.zeros((QK_ROWS - HEAD_DIM, n_mem), F32)
    for hm in range(MEM_HEADS):
        kh = kvT[hm * HEAD_DIM:(hm + 1) * HEAD_DIM]
        kn = kh * _head_rms_scale(kh) * gk_ref[...]
        km_ref[0, hm] = jnp.concatenate([kn, zero_pad], axis=0).T.astype(BF16)
        vh = kvT[width + hm * HEAD_DIM:width + (hm + 1) * HEAD_DIM]
        vm_ref[0, hm] = jnp.concatenate([vh, ones_blk], axis=0).astype(BF16)


def _proj_kernel(x_ref, ng_ref, wT_ref, bf_ref, tri_ref, gq_ref, gk_ref, wp_ref, ps_ref,
                 km_ref, vm_ref, gmq_ref,
                 qT_ref, kp_ref, vT_ref, gbT_ref, mam_ref,
                 fcarry_ref, halo_ref, *, offs, n_fox, pool_w, fox_w, mem_w):
    ts = x_ref.shape[1]
    s_idx = pl.program_id(1)

    @pl.when(s_idx == 0)
    def _():
        fcarry_ref[...] = jnp.zeros_like(fcarry_ref)
        halo_ref[...] = jnp.zeros_like(halo_ref)

    x = x_ref[0]
    h = (x * lax.rsqrt(jnp.mean(x * x, axis=-1, keepdims=True) + EPS) * ng_ref[...]).astype(BF16)

    def proj_t(lo, n):
        return _dot_nt(wT_ref[lo:lo + n, :], h)

    sec = proj_t(offs["ua"], 2 * pool_w + BF16_ROWS)
    u = sec[0:pool_w]
    ga = sec[pool_w:2 * pool_w]
    z = sec[2 * pool_w:2 * pool_w + BF16_ROWS] + bf_ref[...]

    logf = _log_sigmoid(z)
    hi = logf.astype(BF16)
    r1 = logf - hi.astype(F32)
    mid = r1.astype(BF16)
    lo = (r1 - mid.astype(F32)).astype(BF16)
    cs = _dot(jnp.concatenate([hi, mid, lo], axis=0), tri_ref[...])
    fcum = (cs[0:BF16_ROWS] + cs[BF16_ROWS:2 * BF16_ROWS] + cs[2 * BF16_ROWS:3 * BF16_ROWS]
            + fcarry_ref[:, LANES - 1:LANES])
    fcarry_ref[...] = fcum[:, ts - LANES:]
    f2 = fcum * LOG2E
    f_hi = f2.astype(BF16).astype(F32)
    f_r = f2 - f_hi
    f_mid = f_r.astype(BF16).astype(F32)
    f_lo = (f_r - f_mid).astype(BF16).astype(F32)

    arow = lax.broadcasted_iota(jnp.int32, (AUG_ROWS, ts), 0)

    def aug_q(hd):
        a, b, c = f_hi[hd:hd + 1], f_mid[hd:hd + 1], f_lo[hd:hd + 1]
        return jnp.where(arow == 0, a, jnp.where(arow == 1, b, jnp.where(
            arow == 2, c, jnp.where(arow < 6, 1.0, 0.0))))

    def aug_k(hd):
        a, b, c = f_hi[hd:hd + 1], f_mid[hd:hd + 1], f_lo[hd:hd + 1]
        return jnp.where(arow < 3, 1.0, jnp.where(arow == 3, -a, jnp.where(
            arow == 4, -b, jnp.where(arow == 5, -c, 0.0))))

    uext = jnp.concatenate([halo_ref[...], u], axis=1)
    halo_ref[...] = u[:, ts - LANES:]
    pos1 = (s_idx * ts + lax.broadcasted_iota(jnp.int32, (1, ts), 1) + 1).astype(F32)
    gd = POOL_GROUP_DIM
    acc = uext
    pooled = []
    shift = 1
    for g, w in enumerate(POOL_WINDOWS):
        while shift < w:
            acc = acc + pltpu.roll(acc, shift, 1)
            shift *= 2
        pooled.append(acc[0:gd, LANES:] / jnp.minimum(pos1, float(w)))
        acc = acc[gd:]
    d = (jnp.concatenate(pooled, axis=0) - u).astype(BF16)
    ya = _dot(wp_ref[...], d) * ps_ref[...]
    mam_ref[0, 0:pool_w, :] = (ya * _silu(ga)).astype(BF16)

    zero_rows = jnp.zeros((QK_ROWS - HEAD_DIM - AUG_ROWS, ts), F32)
    qT = proj_t(offs["q"], fox_w)
    for hd in range(n_fox):
        qh = qT[hd * HEAD_DIM:(hd + 1) * HEAD_DIM]
        qn = qh * _head_rms_scale(qh) * gq_ref[...] * (SCALE * LOG2E)
        qT_ref[0, hd, 0:HEAD_DIM, :] = qn.astype(BF16)
        qT_ref[0, hd, HEAD_DIM:HEAD_DIM + AUG_ROWS, :] = aug_q(hd).astype(BF16)
        qT_ref[0, hd, HEAD_DIM + AUG_ROWS:, :] = zero_rows.astype(BF16)

    kT = proj_t(offs["k"], fox_w)
    for hd in range(n_fox):
        kh = kT[hd * HEAD_DIM:(hd + 1) * HEAD_DIM]
        kn = kh * _head_rms_scale(kh) * gk_ref[...]
        kfull = jnp.concatenate([kn, aug_k(hd), zero_rows], axis=0)
        kp_ref[0, hd] = kfull.T.astype(BF16)

    vT = proj_t(offs["v"], fox_w)
    ones_blk = _ones_row_block(ts)
    tk = vT_ref.shape[4]
    for hd in range(n_fox):
        va = jnp.concatenate([vT[hd * HEAD_DIM:(hd + 1) * HEAD_DIM], ones_blk], axis=0).astype(BF16)
        for c in range(ts // tk):
            vT_ref[0, hd, c] = va[:, c * tk:(c + 1) * tk]

    gbT_ref[0] = proj_t(offs["gb"], fox_w)

    sec = proj_t(offs["qm"], 2 * mem_w)
    qm = sec[0:mem_w]
    gm = sec[mem_w:2 * mem_w]
    zero_pad = jnp.zeros((QK_ROWS - HEAD_DIM, ts), F32)
    for hm in range(MEM_HEADS):
        qh = qm[hm * HEAD_DIM:(hm + 1) * HEAD_DIM]
        qn = qh * _head_rms_scale(qh) * gmq_ref[...] * (SCALE * LOG2E)
        qpad = jnp.concatenate([qn, zero_pad], axis=0).astype(BF16)
        lg = _dot(km_ref[0, hm], qpad)
        p = jnp.exp2(lg - jnp.max(lg, axis=0, keepdims=True)).astype(BF16)
        yv = _dot(vm_ref[0, hm], p)
        y = yv[0:HEAD_DIM] / yv[HEAD_DIM:HEAD_DIM + 1]
        mm = y * _silu(gm[hm * HEAD_DIM:(hm + 1) * HEAD_DIM])
        mam_ref[0, pool_w + hm * HEAD_DIM:pool_w + (hm + 1) * HEAD_DIM, :] = mm.astype(BF16)


def _fox_kernel(qT_ref, kp_ref, vT_ref, yT_ref, acc_ref):
    hg = qT_ref.shape[1]
    tq = qT_ref.shape[3]
    tk = vT_ref.shape[4]
    n_diag = tq // tk
    i = pl.program_id(2)

    acc_ref[...] = jnp.zeros_like(acc_ref)
    causal = (lax.broadcasted_iota(jnp.int32, (tk, tq), 0)
              <= lax.broadcasted_iota(jnp.int32, (tk, tq), 1))

    def step(j, ms, diag=None):
        lo = 0 if diag is None else diag * tk
        start = pl.multiple_of(j * tk, tk)
        scores = [_dot(kp_ref[0, hd, pl.ds(start, tk), :], qT_ref[0, hd, :, lo:])
                  for hd in range(hg)]
        out = []
        for hd in range(hg):
            s = scores[hd]
            if diag is not None:
                s = jnp.where(causal[:, :tq - lo], s, MASK_VALUE)
            m_old = ms[hd][:, lo:]
            m_new = jnp.maximum(m_old, jnp.max(s, axis=0, keepdims=True))
            p = jnp.exp2(s - m_new).astype(BF16)
            alpha = jnp.exp2(m_old - m_new)
            acc_ref[hd, :, lo:] = alpha * acc_ref[hd, :, lo:] + _dot(vT_ref[0, hd, j], p)
            out.append(m_new if lo == 0 else jnp.concatenate([ms[hd][:, :lo], m_new], axis=1))
        return tuple(out)

    ms = tuple(jnp.full((1, tq), MASK_VALUE, F32) for _ in range(hg))
    ms = lax.fori_loop(0, i * n_diag, lambda j, c: step(j, c), ms)
    for c in range(n_diag):
        ms = step(i * n_diag + c, ms, diag=c)
    for hd in range(hg):
        acc = acc_ref[hd]
        yT_ref[0, hd * HEAD_DIM:(hd + 1) * HEAD_DIM, :] = acc[0:HEAD_DIM] / acc[HEAD_DIM:HEAD_DIM + 1]


def _out_kernel(x_ref, yT_ref, gbT_ref, mam_ref, woT_ref, o_ref, *, pool_w):
    yb = (yT_ref[0] * _silu(gbT_ref[0])).astype(BF16)
    mam = mam_ref[0]
    mixT = jnp.concatenate([mam[0:pool_w], yb, mam[pool_w:]], axis=0)
    outT = _dot(woT_ref[...], mixT)
    o_ref[0] = x_ref[0] + outT.T


def _col(v, rows=None):
    v = v.astype(F32)
    if rows is not None and rows > v.shape[0]:
        v = jnp.concatenate([v, jnp.zeros((rows - v.shape[0],), F32)])
    return v[:, None]


def _params(sem, flags=None):
    return pltpu.CompilerParams(dimension_semantics=sem, vmem_limit_bytes=VMEM_LIMIT_BYTES, flags=flags)


def _layer(x, mem, norm_g, w_in, b_f, w_pool, pool_scale, fox_q_g, fox_k_g,
           mem_norm_g, w_mem_kv, mem_q_g, mem_k_g, w_out):
    bsz, seq, dm = x.shape
    n_mem = mem.shape[1]
    n_fox = b_f.shape[0]
    pool_w = pool_scale.shape[0]
    fox_w = n_fox * HEAD_DIM
    mem_w = MEM_HEADS * HEAD_DIM
    assert n_fox <= BF16_ROWS and pool_w == len(POOL_WINDOWS) * POOL_GROUP_DIM
    assert w_in.shape[1] == 2 * pool_w + 4 * fox_w + n_fox + 2 * mem_w
    assert w_out.shape[0] == pool_w + fox_w + mem_w
    ts, tq, tk = SEQ_TILE, Q_TILE, KV_TILE
    hg = math.gcd(HEADS_PER_STEP, n_fox)
    assert seq % ts == 0 and seq % tq == 0 and tq % tk == 0 and ts % tk == 0

    sizes = (pool_w, pool_w, fox_w, fox_w, fox_w, n_fox, fox_w, mem_w, mem_w)
    parts, off = [], 0
    for w in sizes:
        parts.append(w_in[:, off:off + w])
        off += w
    w_ua, w_ga, w_q, w_k, w_v, w_f, w_gb, w_qm, w_gm = parts
    w_f = jnp.concatenate([w_f, jnp.zeros((dm, BF16_ROWS - n_fox), w_in.dtype)], axis=1)
    order = [("ua", w_ua), ("ga", w_ga), ("f", w_f), ("q", w_q), ("k", w_k), ("v", w_v),
             ("gb", w_gb), ("qm", w_qm), ("gm", w_gm)]
    offs, off = {}, 0
    for name, w in order:
        offs[name] = off
        off += w.shape[1]
    wT = jnp.concatenate([w for _, w in order], axis=1).T.astype(BF16)
    n_rows = wT.shape[0]
    tri = (lax.broadcasted_iota(jnp.int32, (ts, ts), 0)
           <= lax.broadcasted_iota(jnp.int32, (ts, ts), 1)).astype(BF16)
    wp_bd = jax.scipy.linalg.block_diag(*[w_pool[g].T for g in range(len(POOL_WINDOWS))]).astype(BF16)
    wkvT = w_mem_kv.T.astype(BF16)
    woT = w_out.T.astype(BF16)

    km, vm = pl.pallas_call(
        _mem_kv_kernel,
        grid=(bsz,),
        in_specs=[
            pl.BlockSpec((1, n_mem, dm), lambda b: (b, 0, 0)),
            pl.BlockSpec((1, dm), lambda b: (0, 0)),
            pl.BlockSpec((2 * mem_w, dm), lambda b: (0, 0)),
            pl.BlockSpec((HEAD_DIM, 1), lambda b: (0, 0)),
        ],
        out_specs=[
            pl.BlockSpec((1, MEM_HEADS, n_mem, QK_ROWS), lambda b: (b, 0, 0, 0)),
            pl.BlockSpec((1, MEM_HEADS, V_ROWS, n_mem), lambda b: (b, 0, 0, 0)),
        ],
        out_shape=[
            jax.ShapeDtypeStruct((bsz, MEM_HEADS, n_mem, QK_ROWS), BF16),
            jax.ShapeDtypeStruct((bsz, MEM_HEADS, V_ROWS, n_mem), BF16),
        ],
        compiler_params=_params(("arbitrary",)),
        name="mem_kv",
    )(mem, mem_norm_g[None, :], wkvT, _col(mem_k_g))

    const2 = lambda b, s: (0, 0)
    qT, kp, vT, gbT, mam = pl.pallas_call(
        functools.partial(_proj_kernel, offs=offs, n_fox=n_fox, pool_w=pool_w,
                          fox_w=fox_w, mem_w=mem_w),
        grid=(bsz, seq // ts),
        in_specs=[
            pl.BlockSpec((1, ts, dm), lambda b, s: (b, s, 0)),
            pl.BlockSpec((1, dm), const2),
            pl.BlockSpec((n_rows, dm), const2),
            pl.BlockSpec((BF16_ROWS, 1), const2),
            pl.BlockSpec((ts, ts), const2),
            pl.BlockSpec((HEAD_DIM, 1), const2),
            pl.BlockSpec((HEAD_DIM, 1), const2),
            pl.BlockSpec((pool_w, pool_w), const2),
            pl.BlockSpec((pool_w, 1), const2),
            pl.BlockSpec((1, MEM_HEADS, n_mem, QK_ROWS), lambda b, s: (b, 0, 0, 0)),
            pl.BlockSpec((1, MEM_HEADS, V_ROWS, n_mem), lambda b, s: (b, 0, 0, 0)),
            pl.BlockSpec((HEAD_DIM, 1), const2),
        ],
        out_specs=[
            pl.BlockSpec((1, n_fox, QK_ROWS, ts), lambda b, s: (b, 0, 0, s)),
            pl.BlockSpec((1, n_fox, ts, QK_ROWS), lambda b, s: (b, 0, s, 0)),
            pl.BlockSpec((1, n_fox, ts // tk, V_ROWS, tk), lambda b, s: (b, 0, s, 0, 0)),
            pl.BlockSpec((1, fox_w, ts), lambda b, s: (b, 0, s)),
            pl.BlockSpec((1, pool_w + mem_w, ts), lambda b, s: (b, 0, s)),
        ],
        out_shape=[
            jax.ShapeDtypeStruct((bsz, n_fox, QK_ROWS, seq), BF16),
            jax.ShapeDtypeStruct((bsz, n_fox, seq, QK_ROWS), BF16),
            jax.ShapeDtypeStruct((bsz, n_fox, seq // tk, V_ROWS, tk), BF16),
            jax.ShapeDtypeStruct((bsz, fox_w, seq), F32),
            jax.ShapeDtypeStruct((bsz, pool_w + mem_w, seq), BF16),
        ],
        scratch_shapes=[
            pltpu.VMEM((BF16_ROWS, LANES), F32),
            pltpu.VMEM((pool_w, LANES), F32),
        ],
        compiler_params=_params(("arbitrary", "arbitrary")),
        name="proj",
    )(x, norm_g[None, :], wT, _col(b_f, BF16_ROWS), tri, _col(fox_q_g), _col(fox_k_g),
      wp_bd, _col(pool_scale), km, vm, _col(mem_q_g))

    yT = pl.pallas_call(
        _fox_kernel,
        grid=(bsz, n_fox // hg, seq // tq),
        in_specs=[
            pl.BlockSpec((1, hg, QK_ROWS, tq), lambda b, h, i: (b, h, 0, i)),
            pl.BlockSpec((1, hg, seq, QK_ROWS), lambda b, h, i: (b, h, 0, 0)),
            pl.BlockSpec((1, hg, seq // tk, V_ROWS, tk), lambda b, h, i: (b, h, 0, 0, 0)),
        ],
        out_specs=pl.BlockSpec((1, hg * HEAD_DIM, tq), lambda b, h, i: (b, h, i)),
        out_shape=jax.ShapeDtypeStruct((bsz, fox_w, seq), F32),
        scratch_shapes=[pltpu.VMEM((hg, V_ROWS, tq), F32)],
        compiler_params=_params(("arbitrary", "arbitrary", "arbitrary")),
        name="fox",
    )(qT, kp, vT)

    return pl.pallas_call(
        functools.partial(_out_kernel, pool_w=pool_w),
        grid=(bsz, seq // ts),
        in_specs=[
            pl.BlockSpec((1, ts, dm), lambda b, s: (b, s, 0)),
            pl.BlockSpec((1, fox_w, ts), lambda b, s: (b, 0, s)),
            pl.BlockSpec((1, fox_w, ts), lambda b, s: (b, 0, s)),
            pl.BlockSpec((1, pool_w + mem_w, ts), lambda b, s: (b, 0, s)),
            pl.BlockSpec((dm, pool_w + fox_w + mem_w), const2),
        ],
        out_specs=pl.BlockSpec((1, ts, dm), lambda b, s: (b, s, 0)),
        out_shape=jax.ShapeDtypeStruct((bsz, seq, dm), x.dtype),
        compiler_params=_params(("arbitrary", "arbitrary")),
        name="out_proj",
    )(x, yT, gbT, mam, woT)


def kernel(x, mem, norm_g, w_in, b_f, w_pool, pool_scale, fox_q_g, fox_k_g, mem_norm_g,
           w_mem_kv, mem_q_g, mem_k_g, w_out):
    for l in range(norm_g.shape[0]):
        x = _layer(x, mem, norm_g[l], w_in[l], b_f[l], w_pool[l], pool_scale[l], fox_q_g[l],
                   fox_k_g[l], mem_norm_g[l], w_mem_kv[l], mem_q_g[l], mem_k_g[l], w_out[l])
    return x
```

```python
import functools
import math

import jax
import jax.numpy as jnp
from jax import lax
from jax.experimental import pallas as pl
from jax.experimental.pallas import tpu as pltpu

F32 = jnp.float32
BF16 = jnp.bfloat16

HEAD_DIM = 64
EPS = 1e-6
POOL_WINDOWS = (2, 4, 8, 16)
POOL_GROUP_DIM = 64
MEM_HEADS = 4
SCALE = 1.0 / math.sqrt(HEAD_DIM)
LOG2E = math.log2(math.e)
MASK_VALUE = -1e30

LANES = 128
BF16_ROWS = 16
QK_ROWS = 128
AUG_ROWS = BF16_ROWS
V_ROWS = HEAD_DIM + BF16_ROWS

SEQ_TILE = 512
Q_TILE = 512
KV_TILE = 256
HEADS_PER_STEP = 8
VMEM_LIMIT_BYTES = 56 * 1024 * 1024

NT_DIMS = (((1,), (1,)), ((), ()))


def _dot(a, b):
    return jnp.dot(a, b, preferred_element_type=F32)


def _dot_nt(a, b):
    return lax.dot_general(a, b, NT_DIMS, preferred_element_type=F32)


def _silu(g):
    return g * jax.nn.sigmoid(g)


def _log_sigmoid(z):
    return jnp.minimum(z, 0.0) - jnp.log1p(jnp.exp(-jnp.abs(z)))


def _head_rms_scale(t):
    return lax.rsqrt(jnp.mean(t * t, axis=0, keepdims=True) + EPS)


def _ones_row_block(n):
    row = lax.broadcasted_iota(jnp.int32, (BF16_ROWS, n), 0)
    return (row == 0).astype(F32)


def _mem_kv_kernel(mem_ref, g_ref, wkvT_ref, gk_ref, km_ref, vm_ref):
    x = mem_ref[0]
    n_mem = x.shape[0]
    h = (x * lax.rsqrt(jnp.mean(x * x, axis=-1, keepdims=True) + EPS) * g_ref[...]).astype(BF16)
    kvT = _dot_nt(wkvT_ref[...], h)
    width = MEM_HEADS * HEAD_DIM
    ones_blk = _ones_row_block(n_mem)
    zero_pad = jnp.zeros((QK_ROWS - HEAD_DIM, n_mem), F32)
    for hm in range(MEM_HEADS):
        kh = kvT[hm * HEAD_DIM:(hm + 1) * HEAD_DIM]
        kn = kh * _head_rms_scale(kh) * gk_ref[...]
        km_ref[0, hm] = jnp.concatenate([kn, zero_pad], axis=0).T.astype(BF16)
        vh = kvT[width + hm * HEAD_DIM:width + (hm + 1) * HEAD_DIM]
        vm_ref[0, hm] = jnp.concatenate([vh, ones_blk], axis=0).astype(BF16)


def _proj_kernel(x_ref, ng_ref, wT_ref, bf_ref, tri_ref, gq_ref, gk_ref, wp_ref, ps_ref,
                 km_ref, vm_ref, gmq_ref,
                 qT_ref, kp_ref, vT_ref, gbT_ref, mam_ref,
                 fcarry_ref, halo_ref, *, offs, n_fox, pool_w, fox_w, mem_w):
    ts = x_ref.shape[1]
    s_idx = pl.program_id(1)

    @pl.when(s_idx == 0)
    def _():
        fcarry_ref[...] = jnp.zeros_like(fcarry_ref)
        halo_ref[...] = jnp.zeros_like(halo_ref)

    x = x_ref[0]
    h = (x * lax.rsqrt(jnp.mean(x * x, axis=-1, keepdims=True) + EPS) * ng_ref[...]).astype(BF16)

    def proj_t(lo, n):
        return _dot_nt(wT_ref[lo:lo + n, :], h)

    tk = vT_ref.shape[4]
    gd = POOL_GROUP_DIM

    sec_m = proj_t(offs["qm"], 2 * mem_w)
    qm = sec_m[0:mem_w]
    gm = sec_m[mem_w:2 * mem_w]
    sec_a = proj_t(offs["ua"], 2 * pool_w + BF16_ROWS)
    u = sec_a[0:pool_w]
    ga = sec_a[pool_w:2 * pool_w]
    z = sec_a[2 * pool_w:2 * pool_w + BF16_ROWS] + bf_ref[...]

    zero_pad = jnp.zeros((QK_ROWS - HEAD_DIM, ts), F32)
    lgs = []
    for hm in range(MEM_HEADS):
        qh = qm[hm * HEAD_DIM:(hm + 1) * HEAD_DIM]
        qn = qh * _head_rms_scale(qh) * gmq_ref[...] * (SCALE * LOG2E)
        qpad = jnp.concatenate([qn, zero_pad], axis=0).astype(BF16)
        lgs.append(_dot(km_ref[0, hm], qpad))

    qT = proj_t(offs["q"], fox_w)

    logf = _log_sigmoid(z)
    hi = logf.astype(BF16)
    r1 = logf - hi.astype(F32)
    mid = r1.astype(BF16)
    lo = (r1 - mid.astype(F32)).astype(BF16)
    cs = _dot(jnp.concatenate([hi, mid, lo], axis=0), tri_ref[...])

    kT = proj_t(offs["k"], fox_w)

    for hm in range(MEM_HEADS):
        lg = lgs[hm]
        p = jnp.exp2(lg - jnp.max(lg, axis=0, keepdims=True)).astype(BF16)
        yv = _dot(vm_ref[0, hm], p)
        y = yv[0:HEAD_DIM] / yv[HEAD_DIM:HEAD_DIM + 1]
        mm = y * _silu(gm[hm * HEAD_DIM:(hm + 1) * HEAD_DIM])
        mam_ref[0, pool_w + hm * HEAD_DIM:pool_w + (hm + 1) * HEAD_DIM, :] = mm.astype(BF16)

    uext = jnp.concatenate([halo_ref[...], u], axis=1)
    halo_ref[...] = u[:, ts - LANES:]
    pos1 = (s_idx * ts + lax.broadcasted_iota(jnp.int32, (1, ts), 1) + 1).astype(F32)
    acc = uext
    pooled = []
    shift = 1
    for g, w in enumerate(POOL_WINDOWS):
        while shift < w:
            acc = acc + pltpu.roll(acc, shift, 1)
            shift *= 2
        pooled.append(acc[0:gd, LANES:] / jnp.minimum(pos1, float(w)))
        acc = acc[gd:]
    d = (jnp.concatenate(pooled, axis=0) - u).astype(BF16)
    ya = _dot(wp_ref[...], d) * ps_ref[...]
    mam_ref[0, 0:pool_w, :] = (ya * _silu(ga)).astype(BF16)

    vT = proj_t(offs["v"], fox_w)
    gbT_ref[0] = proj_t(offs["gb"], fox_w)

    fcum = (cs[0:BF16_ROWS] + cs[BF16_ROWS:2 * BF16_ROWS] + cs[2 * BF16_ROWS:3 * BF16_ROWS]
            + fcarry_ref[:, LANES - 1:LANES])
    fcarry_ref[...] = fcum[:, ts - LANES:]
    f2 = fcum * LOG2E
    f_hi = f2.astype(BF16).astype(F32)
    f_r = f2 - f_hi
    f_mid = f_r.astype(BF16).astype(F32)
    f_lo = (f_r - f_mid).astype(BF16).astype(F32)

    arow = lax.broadcasted_iota(jnp.int32, (AUG_ROWS, ts), 0)

    def aug_q(hd):
        a, b, c = f_hi[hd:hd + 1], f_mid[hd:hd + 1], f_lo[hd:hd + 1]
        return jnp.where(arow == 0, a, jnp.where(arow == 1, b, jnp.where(
            arow == 2, c, jnp.where(arow < 6, 1.0, 0.0))))

    def aug_k(hd):
        a, b, c = f_hi[hd:hd + 1], f_mid[hd:hd + 1], f_lo[hd:hd + 1]
        return jnp.where(arow < 3, 1.0, jnp.where(arow == 3, -a, jnp.where(
            arow == 4, -b, jnp.where(arow == 5, -c, 0.0))))

    zero_rows = jnp.zeros((QK_ROWS - HEAD_DIM - AUG_ROWS, ts), F32)
    for hd in range(n_fox):
        qh = qT[hd * HEAD_DIM:(hd + 1) * HEAD_DIM]
        qn = qh * _head_rms_scale(qh) * gq_ref[...] * (SCALE * LOG2E)
        qT_ref[0, hd, 0:HEAD_DIM, :] = qn.astype(BF16)
        qT_ref[0, hd, HEAD_DIM:HEAD_DIM + AUG_ROWS, :] = aug_q(hd).astype(BF16)
        qT_ref[0, hd, HEAD_DIM + AUG_ROWS:, :] = zero_rows.astype(BF16)

    for hd in range(n_fox):
        kh = kT[hd * HEAD_DIM:(hd + 1) * HEAD_DIM]
        kn = kh * _head_rms_scale(kh) * gk_ref[...]
        kfull = jnp.concatenate([kn, aug_k(hd), zero_rows], axis=0)
        kp_ref[0, hd] = kfull.T.astype(BF16)

    ones_blk = _ones_row_block(ts)
    for hd in range(n_fox):
        va = jnp.concatenate([vT[hd * HEAD_DIM:(hd + 1) * HEAD_DIM], ones_blk], axis=0).astype(BF16)
        for c in range(ts // tk):
            vT_ref[0, hd, c] = va[:, c * tk:(c + 1) * tk]


def _fox_kernel(qT_ref, kp_ref, vT_ref, yT_ref, acc_ref):
    hg = qT_ref.shape[1]
    tq = qT_ref.shape[3]
    tk = vT_ref.shape[4]
    n_diag = tq // tk
    i = pl.program_id(2)

    acc_ref[...] = jnp.zeros_like(acc_ref)
    causal = (lax.broadcasted_iota(jnp.int32, (tk, tq), 0)
              <= lax.broadcasted_iota(jnp.int32, (tk, tq), 1))

    def step(j, ms, diag=None):
        lo = 0 if diag is None else diag * tk
        start = pl.multiple_of(j * tk, tk)
        scores = [_dot(kp_ref[0, hd, pl.ds(start, tk), :], qT_ref[0, hd, :, lo:])
                  for hd in range(hg)]
        out = []
        for hd in range(hg):
            s = scores[hd]
            if diag is not None:
                s = jnp.where(causal[:, :tq - lo], s, MASK_VALUE)
            m_old = ms[hd][:, lo:]
            m_new = jnp.maximum(m_old, jnp.max(s, axis=0, keepdims=True))
            p = jnp.exp2(s - m_new).astype(BF16)
            alpha = jnp.exp2(m_old - m_new)
            acc_ref[hd, :, lo:] = alpha * acc_ref[hd, :, lo:] + _dot(vT_ref[0, hd, j], p)
            out.append(m_new if lo == 0 else jnp.concatenate([ms[hd][:, :lo], m_new], axis=1))
        return tuple(out)

    ms = tuple(jnp.full((1, tq), MASK_VALUE, F32) for _ in range(hg))
    ms = lax.fori_loop(0, i * n_diag, lambda j, c: step(j, c), ms)
    for c in range(n_diag):
        ms = step(i * n_diag + c, ms, diag=c)
    for hd in range(hg):
        acc = acc_ref[hd]
        yT_ref[0, hd * HEAD_DIM:(hd + 1) * HEAD_DIM, :] = acc[0:HEAD_DIM] / acc[HEAD_DIM:HEAD_DIM + 1]


def _out_kernel(x_ref, yT_ref, gbT_ref, mam_ref, woT_ref, o_ref, *, pool_w):
    yb = (yT_ref[0] * _silu(gbT_ref[0])).astype(BF16)
    mam = mam_ref[0]
    mixT = jnp.concatenate([mam[0:pool_w], yb, mam[pool_w:]], axis=0)
    outT = _dot(woT_ref[...], mixT)
    o_ref[0] = x_ref[0] + outT.T


def _col(v, rows=None):
    v = v.astype(F32)
    if rows is not None and rows > v.shape[0]:
        v = jnp.concatenate([v, jnp.zeros((rows - v.shape[0],), F32)])
    return v[:, None]


def _params(sem, flags=None):
    return pltpu.CompilerParams(dimension_semantics=sem, vmem_limit_bytes=VMEM_LIMIT_BYTES, flags=flags)


def _layer(x, mem, norm_g, w_in, b_f, w_pool, pool_scale, fox_q_g, fox_k_g,
           mem_norm_g, w_mem_kv, mem_q_g, mem_k_g, w_out):
    bsz, seq, dm = x.shape
    n_mem = mem.shape[1]
    n_fox = b_f.shape[0]
    pool_w = pool_scale.shape[0]
    fox_w = n_fox * HEAD_DIM
    mem_w = MEM_HEADS * HEAD_DIM
    assert n_fox <= BF16_ROWS and pool_w == len(POOL_WINDOWS) * POOL_GROUP_DIM
    assert w_in.shape[1] == 2 * pool_w + 4 * fox_w + n_fox + 2 * mem_w
    assert w_out.shape[0] == pool_w + fox_w + mem_w
    ts, tq, tk = SEQ_TILE, Q_TILE, KV_TILE
    hg = math.gcd(HEADS_PER_STEP, n_fox)
    assert seq % ts == 0 and seq % tq == 0 and tq % tk == 0 and ts % tk == 0

    sizes = (pool_w, pool_w, fox_w, fox_w, fox_w, n_fox, fox_w, mem_w, mem_w)
    parts, off = [], 0
    for w in sizes:
        parts.append(w_in[:, off:off + w])
        off += w
    w_ua, w_ga, w_q, w_k, w_v, w_f, w_gb, w_qm, w_gm = parts
    w_f = jnp.concatenate([w_f, jnp.zeros((dm, BF16_ROWS - n_fox), w_in.dtype)], axis=1)
    order = [("ua", w_ua), ("ga", w_ga), ("f", w_f), ("q", w_q), ("k", w_k), ("v", w_v),
             ("gb", w_gb), ("qm", w_qm), ("gm", w_gm)]
    offs, off = {}, 0
    for name, w in order:
        offs[name] = off
        off += w.shape[1]
    wT = jnp.concatenate([w for _, w in order], axis=1).T.astype(BF16)
    n_rows = wT.shape[0]
    tri = (lax.broadcasted_iota(jnp.int32, (ts, ts), 0)
           <= lax.broadcasted_iota(jnp.int32, (ts, ts), 1)).astype(BF16)
    wp_bd = jax.scipy.linalg.block_diag(*[w_pool[g].T for g in range(len(POOL_WINDOWS))]).astype(BF16)
    wkvT = w_mem_kv.T.astype(BF16)
    woT = w_out.T.astype(BF16)

    km, vm = pl.pallas_call(
        _mem_kv_kernel,
        grid=(bsz,),
        in_specs=[
            pl.BlockSpec((1, n_mem, dm), lambda b: (b, 0, 0)),
            pl.BlockSpec((1, dm), lambda b: (0, 0)),
            pl.BlockSpec((2 * mem_w, dm), lambda b: (0, 0)),
            pl.BlockSpec((HEAD_DIM, 1), lambda b: (0, 0)),
        ],
        out_specs=[
            pl.BlockSpec((1, MEM_HEADS, n_mem, QK_ROWS), lambda b: (b, 0, 0, 0)),
            pl.BlockSpec((1, MEM_HEADS, V_ROWS, n_mem), lambda b: (b, 0, 0, 0)),
        ],
        out_shape=[
            jax.ShapeDtypeStruct((bsz, MEM_HEADS, n_mem, QK_ROWS), BF16),
            jax.ShapeDtypeStruct((bsz, MEM_HEADS, V_ROWS, n_mem), BF16),
        ],
        compiler_params=_params(("arbitrary",)),
        name="mem_kv",
    )(mem, mem_norm_g[None, :], wkvT, _col(mem_k_g))

    const2 = lambda b, s: (0, 0)
    qT, kp, vT, gbT, mam = pl.pallas_call(
        functools.partial(_proj_kernel, offs=offs, n_fox=n_fox, pool_w=pool_w,
                          fox_w=fox_w, mem_w=mem_w),
        grid=(bsz, seq // ts),
        in_specs=[
            pl.BlockSpec((1, ts, dm), lambda b, s: (b, s, 0)),
            pl.BlockSpec((1, dm), const2),
            pl.BlockSpec((n_rows, dm), const2),
            pl.BlockSpec((BF16_ROWS, 1), const2),
            pl.BlockSpec((ts, ts), const2),
            pl.BlockSpec((HEAD_DIM, 1), const2),
            pl.BlockSpec((HEAD_DIM, 1), const2),
            pl.BlockSpec((pool_w, pool_w), const2),
            pl.BlockSpec((pool_w, 1), const2),
            pl.BlockSpec((1, MEM_HEADS, n_mem, QK_ROWS), lambda b, s: (b, 0, 0, 0)),
            pl.BlockSpec((1, MEM_HEADS, V_ROWS, n_mem), lambda b, s: (b, 0, 0, 0)),
            pl.BlockSpec((HEAD_DIM, 1), const2),
        ],
        out_specs=[
            pl.BlockSpec((1, n_fox, QK_ROWS, ts), lambda b, s: (b, 0, 0, s)),
            pl.BlockSpec((1, n_fox, ts, QK_ROWS), lambda b, s: (b, 0, s, 0)),
            pl.BlockSpec((1, n_fox, ts // tk, V_ROWS, tk), lambda b, s: (b, 0, s, 0, 0)),
            pl.BlockSpec((1, fox_w, ts), lambda b, s: (b, 0, s)),
            pl.BlockSpec((1, pool_w + mem_w, ts), lambda b, s: (b, 0, s)),
        ],
        out_shape=[
            jax.ShapeDtypeStruct((bsz, n_fox, QK_ROWS, seq), BF16),
            jax.ShapeDtypeStruct((bsz, n_fox, seq, QK_ROWS), BF16),
            jax.ShapeDtypeStruct((bsz, n_fox, seq // tk, V_ROWS, tk), BF16),
            jax.ShapeDtypeStruct((bsz, fox_w, seq), F32),
            jax.ShapeDtypeStruct((bsz, pool_w + mem_w, seq), BF16),
        ],
        scratch_shapes=[
            pltpu.VMEM((BF16_ROWS, LANES), F32),
            pltpu.VMEM((pool_w, LANES), F32),
        ],
        compiler_params=_params(("arbitrary", "arbitrary")),
        name="proj",
    )(x, norm_g[None, :], wT, _col(b_f, BF16_ROWS), tri, _col(fox_q_g), _col(fox_k_g),
      wp_bd, _col(pool_scale), km, vm, _col(mem_q_g))

    yT = pl.pallas_call(
        _fox_kernel,
        grid=(bsz, n_fox // hg, seq // tq),
        in_specs=[
            pl.BlockSpec((1, hg, QK_ROWS, tq), lambda b, h, i: (b, h, 0, i)),
            pl.BlockSpec((1, hg, seq, QK_ROWS), lambda b, h, i: (b, h, 0, 0)),
            pl.BlockSpec((1, hg, seq // tk, V_ROWS, tk), lambda b, h, i: (b, h, 0, 0, 0)),
        ],
        out_specs=pl.BlockSpec((1, hg * HEAD_DIM, tq), lambda b, h, i: (b, h, i)),
        out_shape=jax.ShapeDtypeStruct((bsz, fox_w, seq), F32),
        scratch_shapes=[pltpu.VMEM((hg, V_ROWS, tq), F32)],
        compiler_params=_params(("arbitrary", "arbitrary", "arbitrary")),
        name="fox",
    )(qT, kp, vT)

    return pl.pallas_call(
        functools.partial(_out_kernel, pool_w=pool_w),
        grid=(bsz, seq // ts),
        in_specs=[
            pl.BlockSpec((1, ts, dm), lambda b, s: (b, s, 0)),
            pl.BlockSpec((1, fox_w, ts), lambda b, s: (b, 0, s)),
            pl.BlockSpec((1, fox_w, ts), lambda b, s: (b, 0, s)),
            pl.BlockSpec((1, pool_w + mem_w, ts), lambda b, s: (b, 0, s)),
            pl.BlockSpec((dm, pool_w + fox_w + mem_w), const2),
        ],
        out_specs=pl.BlockSpec((1, ts, dm), lambda b, s: (b, s, 0)),
        out_shape=jax.ShapeDtypeStruct((bsz, seq, dm), x.dtype),
        compiler_params=_params(("arbitrary", "arbitrary")),
        name="out_proj",
    )(x, yT, gbT, mam, woT)


def kernel(x, mem, norm_g, w_in, b_f, w_pool, pool_scale, fox_q_g, fox_k_g, mem_norm_g,
           w_mem_kv, mem_q_g, mem_k_g, w_out):
    for l in range(norm_g.shape[0]):
        x = _layer(x, mem, norm_g[l], w_in[l], b_f[l], w_pool[l], pool_scale[l], fox_q_g[l],
                   fox_k_g[l], mem_norm_g[l], w_mem_kv[l], mem_q_g[l], mem_k_g[l], w_out[l])
    return x
```

```python
import functools
import math

import jax
import jax.numpy as jnp
from jax import lax
from jax.experimental import pallas as pl
from jax.experimental.pallas import tpu as pltpu

F32 = jnp.float32
BF16 = jnp.bfloat16

HEAD_DIM = 64
EPS = 1e-6
POOL_WINDOWS = (2, 4, 8, 16)
POOL_GROUP_DIM = 64
MEM_HEADS = 4
SCALE = 1.0 / math.sqrt(HEAD_DIM)
LOG2E = math.log2(math.e)
MASK_VALUE = -1e30

LANES = 128
BF16_ROWS = 16
QK_ROWS = 128
AUG_ROWS = BF16_ROWS
V_ROWS = HEAD_DIM + BF16_ROWS

SEQ_TILE = 512
Q_TILE = 512
KV_TILE = 256
VMEM_LIMIT_BYTES = 56 * 1024 * 1024

NT_DIMS = (((1,), (1,)), ((), ()))


def _dot(a, b):
    return jnp.dot(a, b, preferred_element_type=F32)


def _dot_nt(a, b):
    return lax.dot_general(a, b, NT_DIMS, preferred_element_type=F32)


def _silu(g):
    return g * jax.nn.sigmoid(g)


def _log_sigmoid(z):
    return jnp.minimum(z, 0.0) - jnp.log1p(jnp.exp(-jnp.abs(z)))


def _head_rms_scale(t):
    return lax.rsqrt(jnp.mean(t * t, axis=0, keepdims=True) + EPS)


def _ones_row_block(n):
    row = lax.broadcasted_iota(jnp.int32, (BF16_ROWS, n), 0)
    return (row == 0).astype(F32)


def _mem_kv_kernel(mem_ref, g_ref, wkvT_ref, gk_ref, km_ref, vm_ref):
    x = mem_ref[0]
    n_mem = x.shape[0]
    h = (x * lax.rsqrt(jnp.mean(x * x, axis=-1, keepdims=True) + EPS) * g_ref[...]).astype(BF16)
    kvT = _dot_nt(wkvT_ref[...], h)
    width = MEM_HEADS * HEAD_DIM
    ones_blk = _ones_row_block(n_mem)
    zero_pad = jnp.zeros((QK_ROWS - HEAD_DIM, n_mem), F32)
    for hm in range(MEM_HEADS):
        kh = kvT[hm * HEAD_DIM:(hm + 1) * HEAD_DIM]
        kn = kh * _head_rms_scale(kh) * gk_ref[...]
        km_ref[0, hm] = jnp.concatenate([kn, zero_pad], axis=0).T.astype(BF16)
        vh = kvT[width + hm * HEAD_DIM:width + (hm + 1) * HEAD_DIM]
        vm_ref[0, hm] = jnp.concatenate([vh, ones_blk], axis=0).astype(BF16)


def _proj_kernel(x_ref, ng_ref, wT_ref, bf_ref, tri_ref, gq_ref, gk_ref, wp_ref, ps_ref,
                 km_ref, vm_ref, gmq_ref,
                 qT_ref, kp_ref, vT_ref, gbT_ref, mam_ref,
                 fcarry_ref, halo_ref, *, offs, n_fox, pool_w, fox_w, mem_w):
    ts = x_ref.shape[1]
    s_idx = pl.program_id(1)

    @pl.when(s_idx == 0)
    def _():
        fcarry_ref[...] = jnp.zeros_like(fcarry_ref)
        halo_ref[...] = jnp.zeros_like(halo_ref)

    x = x_ref[0]
    h = (x * lax.rsqrt(jnp.mean(x * x, axis=-1, keepdims=True) + EPS) * ng_ref[...]).astype(BF16)

    def proj_t(lo, n):
        return _dot_nt(wT_ref[lo:lo + n, :], h)

    tk = vT_ref.shape[4]
    gd = POOL_GROUP_DIM

    sec_m = proj_t(offs["qm"], 2 * mem_w)
    qm = sec_m[0:mem_w]
    gm = sec_m[mem_w:2 * mem_w]
    sec_a = proj_t(offs["ua"], 2 * pool_w + BF16_ROWS)
    u = sec_a[0:pool_w]
    ga = sec_a[pool_w:2 * pool_w]
    z = sec_a[2 * pool_w:2 * pool_w + BF16_ROWS] + bf_ref[...]

    zero_pad = jnp.zeros((QK_ROWS - HEAD_DIM, ts), F32)
    lgs = []
    for hm in range(MEM_HEADS):
        qh = qm[hm * HEAD_DIM:(hm + 1) * HEAD_DIM]
        qn = qh * _head_rms_scale(qh) * gmq_ref[...] * (SCALE * LOG2E)
        qpad = jnp.concatenate([qn, zero_pad], axis=0).astype(BF16)
        lgs.append(_dot(km_ref[0, hm], qpad))

    qT = proj_t(offs["q"], fox_w)

    logf = _log_sigmoid(z)
    hi = logf.astype(BF16)
    r1 = logf - hi.astype(F32)
    mid = r1.astype(BF16)
    lo = (r1 - mid.astype(F32)).astype(BF16)
    cs = _dot(jnp.concatenate([hi, mid, lo], axis=0), tri_ref[...])

    kT = proj_t(offs["k"], fox_w)

    for hm in range(MEM_HEADS):
        lg = lgs[hm]
        p = jnp.exp2(lg - jnp.max(lg, axis=0, keepdims=True)).astype(BF16)
        yv = _dot(vm_ref[0, hm], p)
        y = yv[0:HEAD_DIM] / yv[HEAD_DIM:HEAD_DIM + 1]
        mm = y * _silu(gm[hm * HEAD_DIM:(hm + 1) * HEAD_DIM])
        mam_ref[0, pool_w + hm * HEAD_DIM:pool_w + (hm + 1) * HEAD_DIM, :] = mm.astype(BF16)

    uext = jnp.concatenate([halo_ref[...], u], axis=1)
    halo_ref[...] = u[:, ts - LANES:]
    pos1 = (s_idx * ts + lax.broadcasted_iota(jnp.int32, (1, ts), 1) + 1).astype(F32)
    acc = uext
    pooled = []
    shift = 1
    for g, w in enumerate(POOL_WINDOWS):
        while shift < w:
            acc = acc + pltpu.roll(acc, shift, 1)
            shift *= 2
        pooled.append(acc[0:gd, LANES:] / jnp.minimum(pos1, float(w)))
        acc = acc[gd:]
    d = (jnp.concatenate(pooled, axis=0) - u).astype(BF16)
    ya = _dot(wp_ref[...], d) * ps_ref[...]
    mam_ref[0, 0:pool_w, :] = (ya * _silu(ga)).astype(BF16)

    vT = proj_t(offs["v"], fox_w)
    gbT_ref[0] = proj_t(offs["gb"], fox_w)

    fcum = (cs[0:BF16_ROWS] + cs[BF16_ROWS:2 * BF16_ROWS] + cs[2 * BF16_ROWS:3 * BF16_ROWS]
            + fcarry_ref[:, LANES - 1:LANES])
    fcarry_ref[...] = fcum[:, ts - LANES:]
    f2 = fcum * LOG2E
    f_hi = f2.astype(BF16).astype(F32)
    f_r = f2 - f_hi
    f_mid = f_r.astype(BF16).astype(F32)
    f_lo = (f_r - f_mid).astype(BF16).astype(F32)

    arow = lax.broadcasted_iota(jnp.int32, (AUG_ROWS, ts), 0)

    def aug_q(hd):
        a, b, c = f_hi[hd:hd + 1], f_mid[hd:hd + 1], f_lo[hd:hd + 1]
        return jnp.where(arow == 0, a, jnp.where(arow == 1, b, jnp.where(
            arow == 2, c, jnp.where(arow < 6, 1.0, 0.0))))

    def aug_k(hd):
        a, b, c = f_hi[hd:hd + 1], f_mid[hd:hd + 1], f_lo[hd:hd + 1]
        return jnp.where(arow < 3, 1.0, jnp.where(arow == 3, -a, jnp.where(
            arow == 4, -b, jnp.where(arow == 5, -c, 0.0))))

    zero_rows = jnp.zeros((QK_ROWS - HEAD_DIM - AUG_ROWS, ts), F32)
    for hd in range(n_fox):
        qh = qT[hd * HEAD_DIM:(hd + 1) * HEAD_DIM]
        qn = qh * _head_rms_scale(qh) * gq_ref[...] * (SCALE * LOG2E)
        qT_ref[0, hd, 0:HEAD_DIM, :] = qn.astype(BF16)
        qT_ref[0, hd, HEAD_DIM:HEAD_DIM + AUG_ROWS, :] = aug_q(hd).astype(BF16)
        qT_ref[0, hd, HEAD_DIM + AUG_ROWS:, :] = zero_rows.astype(BF16)

    for hd in range(n_fox):
        kh = kT[hd * HEAD_DIM:(hd + 1) * HEAD_DIM]
        kn = kh * _head_rms_scale(kh) * gk_ref[...]
        kfull = jnp.concatenate([kn, aug_k(hd), zero_rows], axis=0)
        kp_ref[0, hd] = kfull.T.astype(BF16)

    ones_blk = _ones_row_block(ts)
    for hd in range(n_fox):
        va = jnp.concatenate([vT[hd * HEAD_DIM:(hd + 1) * HEAD_DIM], ones_blk], axis=0).astype(BF16)
        for c in range(ts // tk):
            vT_ref[0, hd, c] = va[:, c * tk:(c + 1) * tk]


def _fox_out_kernel(qT_ref, kp_ref, vT_ref, gbT_ref, mam_ref, x_ref, woT_ref, o_ref, acc_ref, *, pool_w):
    hg = qT_ref.shape[1]
    tq = qT_ref.shape[3]
    tk = vT_ref.shape[4]
    n_diag = tq // tk
    i = pl.program_id(1)

    acc_ref[...] = jnp.zeros_like(acc_ref)
    causal = (lax.broadcasted_iota(jnp.int32, (tk, tq), 0)
              <= lax.broadcasted_iota(jnp.int32, (tk, tq), 1))

    def step(j, ms, diag=None):
        lo = 0 if diag is None else diag * tk
        start = pl.multiple_of(j * tk, tk)
        scores = [_dot(kp_ref[0, hd, pl.ds(start, tk), :], qT_ref[0, hd, :, lo:])
                  for hd in range(hg)]
        out = []
        for hd in range(hg):
            s = scores[hd]
            if diag is not None:
                s = jnp.where(causal[:, :tq - lo], s, MASK_VALUE)
            m_old = ms[hd][:, lo:]
            m_new = jnp.maximum(m_old, jnp.max(s, axis=0, keepdims=True))
            p = jnp.exp2(s - m_new).astype(BF16)
            alpha = jnp.exp2(m_old - m_new)
            acc_ref[hd, :, lo:] = alpha * acc_ref[hd, :, lo:] + _dot(vT_ref[0, hd, j], p)
            out.append(m_new if lo == 0 else jnp.concatenate([ms[hd][:, :lo], m_new], axis=1))
        return tuple(out)

    ms = tuple(jnp.full((1, tq), MASK_VALUE, F32) for _ in range(hg))
    ms = lax.fori_loop(0, i * n_diag, lambda j, c: step(j, c), ms)
    for c in range(n_diag):
        ms = step(i * n_diag + c, ms, diag=c)
    ys = []
    for hd in range(hg):
        acc = acc_ref[hd]
        ys.append(acc[0:HEAD_DIM] / acc[HEAD_DIM:HEAD_DIM + 1])
    yb = (jnp.concatenate(ys, axis=0) * _silu(gbT_ref[0])).astype(BF16)
    mam = mam_ref[0]
    mixT = jnp.concatenate([mam[0:pool_w], yb, mam[pool_w:]], axis=0)
    outT = _dot(woT_ref[...], mixT)
    o_ref[0] = x_ref[0] + outT.T


def _col(v, rows=None):
    v = v.astype(F32)
    if rows is not None and rows > v.shape[0]:
        v = jnp.concatenate([v, jnp.zeros((rows - v.shape[0],), F32)])
    return v[:, None]


def _params(sem, flags=None):
    return pltpu.CompilerParams(dimension_semantics=sem, vmem_limit_bytes=VMEM_LIMIT_BYTES, flags=flags)


def _layer(x, mem, norm_g, w_in, b_f, w_pool, pool_scale, fox_q_g, fox_k_g,
           mem_norm_g, w_mem_kv, mem_q_g, mem_k_g, w_out):
    bsz, seq, dm = x.shape
    n_mem = mem.shape[1]
    n_fox = b_f.shape[0]
    pool_w = pool_scale.shape[0]
    fox_w = n_fox * HEAD_DIM
    mem_w = MEM_HEADS * HEAD_DIM
    assert n_fox <= BF16_ROWS and pool_w == len(POOL_WINDOWS) * POOL_GROUP_DIM
    assert w_in.shape[1] == 2 * pool_w + 4 * fox_w + n_fox + 2 * mem_w
    assert w_out.shape[0] == pool_w + fox_w + mem_w
    ts, tq, tk = SEQ_TILE, Q_TILE, KV_TILE
    assert seq % ts == 0 and seq % tq == 0 and tq % tk == 0 and ts % tk == 0

    sizes = (pool_w, pool_w, fox_w, fox_w, fox_w, n_fox, fox_w, mem_w, mem_w)
    parts, off = [], 0
    for w in sizes:
        parts.append(w_in[:, off:off + w])
        off += w
    w_ua, w_ga, w_q, w_k, w_v, w_f, w_gb, w_qm, w_gm = parts
    w_f = jnp.concatenate([w_f, jnp.zeros((dm, BF16_ROWS - n_fox), w_in.dtype)], axis=1)
    order = [("ua", w_ua), ("ga", w_ga), ("f", w_f), ("q", w_q), ("k", w_k), ("v", w_v),
             ("gb", w_gb), ("qm", w_qm), ("gm", w_gm)]
    offs, off = {}, 0
    for name, w in order:
        offs[name] = off
        off += w.shape[1]
    wT = jnp.concatenate([w for _, w in order], axis=1).T.astype(BF16)
    n_rows = wT.shape[0]
    tri = (lax.broadcasted_iota(jnp.int32, (ts, ts), 0)
           <= lax.broadcasted_iota(jnp.int32, (ts, ts), 1)).astype(BF16)
    wp_bd = jax.scipy.linalg.block_diag(*[w_pool[g].T for g in range(len(POOL_WINDOWS))]).astype(BF16)
    wkvT = w_mem_kv.T.astype(BF16)
    woT = w_out.T.astype(BF16)

    km, vm = pl.pallas_call(
        _mem_kv_kernel,
        grid=(bsz,),
        in_specs=[
            pl.BlockSpec((1, n_mem, dm), lambda b: (b, 0, 0)),
            pl.BlockSpec((1, dm), lambda b: (0, 0)),
            pl.BlockSpec((2 * mem_w, dm), lambda b: (0, 0)),
            pl.BlockSpec((HEAD_DIM, 1), lambda b: (0, 0)),
        ],
        out_specs=[
            pl.BlockSpec((1, MEM_HEADS, n_mem, QK_ROWS), lambda b: (b, 0, 0, 0)),
            pl.BlockSpec((1, MEM_HEADS, V_ROWS, n_mem), lambda b: (b, 0, 0, 0)),
        ],
        out_shape=[
            jax.ShapeDtypeStruct((bsz, MEM_HEADS, n_mem, QK_ROWS), BF16),
            jax.ShapeDtypeStruct((bsz, MEM_HEADS, V_ROWS, n_mem), BF16),
        ],
        compiler_params=_params(("arbitrary",)),
        name="mem_kv",
    )(mem, mem_norm_g[None, :], wkvT, _col(mem_k_g))

    const2 = lambda b, s: (0, 0)
    qT, kp, vT, gbT, mam = pl.pallas_call(
        functools.partial(_proj_kernel, offs=offs, n_fox=n_fox, pool_w=pool_w,
                          fox_w=fox_w, mem_w=mem_w),
        grid=(bsz, seq // ts),
        in_specs=[
            pl.BlockSpec((1, ts, dm), lambda b, s: (b, s, 0)),
            pl.BlockSpec((1, dm), const2),
            pl.BlockSpec((n_rows, dm), const2),
            pl.BlockSpec((BF16_ROWS, 1), const2),
            pl.BlockSpec((ts, ts), const2),
            pl.BlockSpec((HEAD_DIM, 1), const2),
            pl.BlockSpec((HEAD_DIM, 1), const2),
            pl.BlockSpec((pool_w, pool_w), const2),
            pl.BlockSpec((pool_w, 1), const2),
            pl.BlockSpec((1, MEM_HEADS, n_mem, QK_ROWS), lambda b, s: (b, 0, 0, 0)),
            pl.BlockSpec((1, MEM_HEADS, V_ROWS, n_mem), lambda b, s: (b, 0, 0, 0)),
            pl.BlockSpec((HEAD_DIM, 1), const2),
        ],
        out_specs=[
            pl.BlockSpec((1, n_fox, QK_ROWS, ts), lambda b, s: (b, 0, 0, s)),
            pl.BlockSpec((1, n_fox, ts, QK_ROWS), lambda b, s: (b, 0, s, 0)),
            pl.BlockSpec((1, n_fox, ts // tk, V_ROWS, tk), lambda b, s: (b, 0, s, 0, 0)),
            pl.BlockSpec((1, fox_w, ts), lambda b, s: (b, 0, s)),
            pl.BlockSpec((1, pool_w + mem_w, ts), lambda b, s: (b, 0, s)),
        ],
        out_shape=[
            jax.ShapeDtypeStruct((bsz, n_fox, QK_ROWS, seq), BF16),
            jax.ShapeDtypeStruct((bsz, n_fox, seq, QK_ROWS), BF16),
            jax.ShapeDtypeStruct((bsz, n_fox, seq // tk, V_ROWS, tk), BF16),
            jax.ShapeDtypeStruct((bsz, fox_w, seq), F32),
            jax.ShapeDtypeStruct((bsz, pool_w + mem_w, seq), BF16),
        ],
        scratch_shapes=[
            pltpu.VMEM((BF16_ROWS, LANES), F32),
            pltpu.VMEM((pool_w, LANES), F32),
        ],
        compiler_params=_params(("arbitrary", "arbitrary")),
        name="proj",
    )(x, norm_g[None, :], wT, _col(b_f, BF16_ROWS), tri, _col(fox_q_g), _col(fox_k_g),
      wp_bd, _col(pool_scale), km, vm, _col(mem_q_g))

    return pl.pallas_call(
        functools.partial(_fox_out_kernel, pool_w=pool_w),
        grid=(bsz, seq // tq),
        in_specs=[
            pl.BlockSpec((1, n_fox, QK_ROWS, tq), lambda b, i: (b, 0, 0, i)),
            pl.BlockSpec((1, n_fox, seq, QK_ROWS), lambda b, i: (b, 0, 0, 0)),
            pl.BlockSpec((1, n_fox, seq // tk, V_ROWS, tk), lambda b, i: (b, 0, 0, 0, 0)),
            pl.BlockSpec((1, fox_w, tq), lambda b, i: (b, 0, i)),
            pl.BlockSpec((1, pool_w + mem_w, tq), lambda b, i: (b, 0, i)),
            pl.BlockSpec((1, tq, dm), lambda b, i: (b, i, 0)),
            pl.BlockSpec((dm, pool_w + fox_w + mem_w), const2),
        ],
        out_specs=pl.BlockSpec((1, tq, dm), lambda b, i: (b, i, 0)),
        out_shape=jax.ShapeDtypeStruct((bsz, seq, dm), x.dtype),
        scratch_shapes=[pltpu.VMEM((n_fox, V_ROWS, tq), F32)],
        compiler_params=_params(("arbitrary", "arbitrary")),
        name="fox_out",
    )(qT, kp, vT, gbT, mam, x, woT)


def kernel(x, mem, norm_g, w_in, b_f, w_pool, pool_scale, fox_q_g, fox_k_g, mem_norm_g,
           w_mem_kv, mem_q_g, mem_k_g, w_out):
    for l in range(norm_g.shape[0]):
        x = _layer(x, mem, norm_g[l], w_in[l], b_f[l], w_pool[l], pool_scale[l], fox_q_g[l],
                   fox_k_g[l], mem_norm_g[l], w_mem_kv[l], mem_q_g[l], mem_k_g[l], w_out[l])
    return x
```

```python
import functools
import math

import jax
import jax.numpy as jnp
from jax import lax
from jax.experimental import pallas as pl
from jax.experimental.pallas import tpu as pltpu

F32 = jnp.float32
BF16 = jnp.bfloat16

HEAD_DIM = 64
EPS = 1e-6
POOL_WINDOWS = (2, 4, 8, 16)
POOL_GROUP_DIM = 64
MEM_HEADS = 4
SCALE = 1.0 / math.sqrt(HEAD_DIM)
LOG2E = math.log2(math.e)
MASK_VALUE = -1e30

LANES = 128
BF16_ROWS = 16
QK_ROWS = 128
AUG_ROWS = BF16_ROWS
V_ROWS = HEAD_DIM + BF16_ROWS

SEQ_TILE = 512
Q_TILE = 512
KV_TILE = 256
FAST_LOOKAHEAD = 3
FAST_MAX_SHIFT = 50.0
BOUND_SLACK = 1.01
VMEM_LIMIT_BYTES = 56 * 1024 * 1024

NT_DIMS = (((1,), (1,)), ((), ()))


def _dot(a, b):
    return jnp.dot(a, b, preferred_element_type=F32)


def _dot_nt(a, b):
    return lax.dot_general(a, b, NT_DIMS, preferred_element_type=F32)


def _silu(g):
    return g * jax.nn.sigmoid(g)


def _log_sigmoid(z):
    return jnp.minimum(z, 0.0) - jnp.log1p(jnp.exp(-jnp.abs(z)))


def _head_rms_scale(t):
    return lax.rsqrt(jnp.mean(t * t, axis=0, keepdims=True) + EPS)


def _ones_row_block(n):
    row = lax.broadcasted_iota(jnp.int32, (BF16_ROWS, n), 0)
    return (row == 0).astype(F32)


def _mem_kv_kernel(mem_ref, g_ref, wkvT_ref, gk_ref, km_ref, vm_ref):
    x = mem_ref[0]
    n_mem = x.shape[0]
    h = (x * lax.rsqrt(jnp.mean(x * x, axis=-1, keepdims=True) + EPS) * g_ref[...]).astype(BF16)
    kvT = _dot_nt(wkvT_ref[...], h)
    width = MEM_HEADS * HEAD_DIM
    ones_blk = _ones_row_block(n_mem)
    zero_pad = jnp.zeros((QK_ROWS - HEAD_DIM, n_mem), F32)
    for hm in range(MEM_HEADS):
        kh = kvT[hm * HEAD_DIM:(hm + 1) * HEAD_DIM]
        kn = kh * _head_rms_scale(kh) * gk_ref[...]
        km_ref[0, hm] = jnp.concatenate([kn, zero_pad], axis=0).T.astype(BF16)
        vh = kvT[width + hm * HEAD_DIM:width + (hm + 1) * HEAD_DIM]
        vm_ref[0, hm] = jnp.concatenate([vh, ones_blk], axis=0).astype(BF16)


def _proj_kernel(kb_ref, x_ref, ng_ref, wT_ref, bf_ref, tri_ref, gq_ref, gk_ref, wp_ref, ps_ref,
                 km_ref, vm_ref, gmq_ref,
                 qT_ref, kp_ref, vT_ref, gbT_ref, mam_ref, shift_ref,
                 fcarry_ref, halo_ref, *, offs, n_fox, pool_w, fox_w, mem_w):
    ts = x_ref.shape[1]
    s_idx = pl.program_id(1)

    @pl.when(s_idx == 0)
    def _():
        fcarry_ref[...] = jnp.zeros_like(fcarry_ref)
        halo_ref[...] = jnp.zeros_like(halo_ref)

    x = x_ref[0]
    h = (x * lax.rsqrt(jnp.mean(x * x, axis=-1, keepdims=True) + EPS) * ng_ref[...]).astype(BF16)

    def proj_t(lo, n):
        return _dot_nt(wT_ref[lo:lo + n, :], h)

    tk = vT_ref.shape[4]
    gd = POOL_GROUP_DIM

    sec_m = proj_t(offs["qm"], 2 * mem_w)
    qm = sec_m[0:mem_w]
    gm = sec_m[mem_w:2 * mem_w]
    sec_a = proj_t(offs["ua"], 2 * pool_w + BF16_ROWS)
    u = sec_a[0:pool_w]
    ga = sec_a[pool_w:2 * pool_w]
    z = sec_a[2 * pool_w:2 * pool_w + BF16_ROWS] + bf_ref[...]

    zero_pad = jnp.zeros((QK_ROWS - HEAD_DIM, ts), F32)
    lgs = []
    for hm in range(MEM_HEADS):
        qh = qm[hm * HEAD_DIM:(hm + 1) * HEAD_DIM]
        qn = qh * _head_rms_scale(qh) * gmq_ref[...] * (SCALE * LOG2E)
        qpad = jnp.concatenate([qn, zero_pad], axis=0).astype(BF16)
        lgs.append(_dot(km_ref[0, hm], qpad))

    qT = proj_t(offs["q"], fox_w)

    logf = _log_sigmoid(z)
    hi = logf.astype(BF16)
    r1 = logf - hi.astype(F32)
    mid = r1.astype(BF16)
    lo = (r1 - mid.astype(F32)).astype(BF16)
    cs = _dot(jnp.concatenate([hi, mid, lo], axis=0), tri_ref[...])

    kT = proj_t(offs["k"], fox_w)

    for hm in range(MEM_HEADS):
        lg = lgs[hm]
        p = jnp.exp2(lg - jnp.max(lg, axis=0, keepdims=True)).astype(BF16)
        yv = _dot(vm_ref[0, hm], p)
        y = yv[0:HEAD_DIM] / yv[HEAD_DIM:HEAD_DIM + 1]
        mm = y * _silu(gm[hm * HEAD_DIM:(hm + 1) * HEAD_DIM])
        mam_ref[0, pool_w + hm * HEAD_DIM:pool_w + (hm + 1) * HEAD_DIM, :] = mm.astype(BF16)

    uext = jnp.concatenate([halo_ref[...], u], axis=1)
    halo_ref[...] = u[:, ts - LANES:]
    pos1 = (s_idx * ts + lax.broadcasted_iota(jnp.int32, (1, ts), 1) + 1).astype(F32)
    acc = uext
    pooled = []
    shift = 1
    for g, w in enumerate(POOL_WINDOWS):
        while shift < w:
            acc = acc + pltpu.roll(acc, shift, 1)
            shift *= 2
        pooled.append(acc[0:gd, LANES:] / jnp.minimum(pos1, float(w)))
        acc = acc[gd:]
    d = (jnp.concatenate(pooled, axis=0) - u).astype(BF16)
    ya = _dot(wp_ref[...], d) * ps_ref[...]
    mam_ref[0, 0:pool_w, :] = (ya * _silu(ga)).astype(BF16)

    vT = proj_t(offs["v"], fox_w)
    gbT_ref[0] = proj_t(offs["gb"], fox_w)

    fcum = (cs[0:BF16_ROWS] + cs[BF16_ROWS:2 * BF16_ROWS] + cs[2 * BF16_ROWS:3 * BF16_ROWS]
            + fcarry_ref[:, LANES - 1:LANES])
    fcarry_ref[...] = fcum[:, ts - LANES:]
    f2 = fcum * LOG2E

    arow = lax.broadcasted_iota(jnp.int32, (AUG_ROWS, ts), 0)

    def split3(v):
        v_hi = v.astype(BF16).astype(F32)
        r = v - v_hi
        v_mid = r.astype(BF16).astype(F32)
        return v_hi, v_mid, (r - v_mid).astype(BF16).astype(F32)

    def rows(*vals):
        out = jnp.zeros((AUG_ROWS, ts), F32)
        for r, v in enumerate(vals):
            out = jnp.where(arow == r, v, out)
        return out

    zero_rows = jnp.zeros((QK_ROWS - HEAD_DIM - AUG_ROWS, ts), F32)
    shifts = jnp.zeros((BF16_ROWS, ts), F32)
    for hd in range(n_fox):
        qh = qT[hd * HEAD_DIM:(hd + 1) * HEAD_DIM]
        qn = qh * _head_rms_scale(qh) * gq_ref[...] * (SCALE * LOG2E)
        m_t = jnp.sqrt(jnp.sum(qn * qn, axis=0, keepdims=True)) * (kb_ref[0] * BOUND_SLACK)
        shifts = jnp.where(arow == hd, m_t, shifts)
        f3, m3 = split3(f2[hd:hd + 1]), split3(m_t)
        qT_ref[0, hd, 0:HEAD_DIM, :] = qn.astype(BF16)
        qT_ref[0, hd, HEAD_DIM:HEAD_DIM + AUG_ROWS, :] = rows(
            f3[0], f3[1], f3[2], 1.0, 1.0, 1.0, -m3[0], -m3[1], -m3[2]).astype(BF16)
        qT_ref[0, hd, HEAD_DIM + AUG_ROWS:, :] = zero_rows.astype(BF16)

        kh = kT[hd * HEAD_DIM:(hd + 1) * HEAD_DIM]
        kn = kh * _head_rms_scale(kh) * gk_ref[...]
        k_aug = rows(1.0, 1.0, 1.0, -f3[0], -f3[1], -f3[2], 1.0, 1.0, 1.0)
        kfull = jnp.concatenate([kn, k_aug, zero_rows], axis=0)
        kp_ref[0, hd] = kfull.T.astype(BF16)
    shift_ref[0] = shifts

    ones_blk = _ones_row_block(ts)
    for hd in range(n_fox):
        va = jnp.concatenate([vT[hd * HEAD_DIM:(hd + 1) * HEAD_DIM], ones_blk], axis=0).astype(BF16)
        for c in range(ts // tk):
            vT_ref[0, hd, c] = va[:, c * tk:(c + 1) * tk]


def _fox_out_kernel(fast_ref, qT_ref, kp_ref, vT_ref, gbT_ref, mam_ref, x_ref, woT_ref, o_ref, acc_ref, *,
                    pool_w):
    hg = qT_ref.shape[1]
    tq = qT_ref.shape[3]
    tk = vT_ref.shape[4]
    n_diag = tq // tk
    i = pl.program_id(1)

    acc_ref[...] = jnp.zeros_like(acc_ref)
    causal = (lax.broadcasted_iota(jnp.int32, (tk, tq), 0)
              <= lax.broadcasted_iota(jnp.int32, (tk, tq), 1))

    def head_scores(hd, j, lo=0):
        start = pl.multiple_of(j * tk, tk)
        return _dot(kp_ref[0, hd, pl.ds(start, tk), :], qT_ref[0, hd, :, lo:])

    def mask(s, diag, lo):
        return s if diag is None else jnp.where(causal[:, :tq - lo], s, MASK_VALUE)

    def fast_tiles(tiles):
        units = [(j, diag, hd) for (j, diag) in tiles for hd in range(hg)]
        scores = {}

        def issue(u):
            j, diag, hd = units[u]
            scores[u] = head_scores(hd, j, 0 if diag is None else diag * tk)

        for u in range(min(FAST_LOOKAHEAD, len(units))):
            issue(u)
        for u, (j, diag, hd) in enumerate(units):
            lo = 0 if diag is None else diag * tk
            p = jnp.exp2(mask(scores.pop(u), diag, lo)).astype(BF16)
            acc_ref[hd, :, lo:] += _dot(vT_ref[0, hd, j], p)
            if u + FAST_LOOKAHEAD < len(units):
                issue(u + FAST_LOOKAHEAD)

    def safe_step(j, ms, diag=None):
        lo = 0 if diag is None else diag * tk
        scores = [head_scores(hd, j, lo) for hd in range(hg)]
        out = []
        for hd in range(hg):
            s = mask(scores[hd], diag, lo)
            m_old = ms[hd][:, lo:]
            m_new = jnp.maximum(m_old, jnp.max(s, axis=0, keepdims=True))
            p = jnp.exp2(s - m_new).astype(BF16)
            alpha = jnp.exp2(m_old - m_new)
            acc_ref[hd, :, lo:] = alpha * acc_ref[hd, :, lo:] + _dot(vT_ref[0, hd, j], p)
            out.append(m_new if lo == 0 else jnp.concatenate([ms[hd][:, :lo], m_new], axis=1))
        return tuple(out)

    @pl.when(fast_ref[0] != 0)
    def _():
        def body(c, carry):
            fast_tiles([(c * n_diag + d, None) for d in range(n_diag)])
            return carry

        lax.fori_loop(0, i, body, 0)
        fast_tiles([(i * n_diag + c, c) for c in range(n_diag)])

    @pl.when(fast_ref[0] == 0)
    def _():
        ms = tuple(jnp.full((1, tq), MASK_VALUE, F32) for _ in range(hg))
        ms = lax.fori_loop(0, i * n_diag, lambda j, c: safe_step(j, c), ms)
        for c in range(n_diag):
            ms = safe_step(i * n_diag + c, ms, diag=c)

    ys = []
    for hd in range(hg):
        acc = acc_ref[hd]
        ys.append(acc[0:HEAD_DIM] / acc[HEAD_DIM:HEAD_DIM + 1])
    yb = (jnp.concatenate(ys, axis=0) * _silu(gbT_ref[0])).astype(BF16)
    mam = mam_ref[0]
    mixT = jnp.concatenate([mam[0:pool_w], yb, mam[pool_w:]], axis=0)
    outT = _dot(woT_ref[...], mixT)
    o_ref[0] = x_ref[0] + outT.T


def _col(v, rows=None):
    v = v.astype(F32)
    if rows is not None and rows > v.shape[0]:
        v = jnp.concatenate([v, jnp.zeros((rows - v.shape[0],), F32)])
    return v[:, None]


def _params(sem, flags=None):
    return pltpu.CompilerParams(dimension_semantics=sem, vmem_limit_bytes=VMEM_LIMIT_BYTES, flags=flags)


def _layer(x, mem, norm_g, w_in, b_f, w_pool, pool_scale, fox_q_g, fox_k_g,
           mem_norm_g, w_mem_kv, mem_q_g, mem_k_g, w_out):
    bsz, seq, dm = x.shape
    n_mem = mem.shape[1]
    n_fox = b_f.shape[0]
    pool_w = pool_scale.shape[0]
    fox_w = n_fox * HEAD_DIM
    mem_w = MEM_HEADS * HEAD_DIM
    assert n_fox <= BF16_ROWS and pool_w == len(POOL_WINDOWS) * POOL_GROUP_DIM
    assert w_in.shape[1] == 2 * pool_w + 4 * fox_w + n_fox + 2 * mem_w
    assert w_out.shape[0] == pool_w + fox_w + mem_w
    ts, tq, tk = SEQ_TILE, Q_TILE, KV_TILE
    assert seq % ts == 0 and seq % tq == 0 and tq % tk == 0 and ts % tk == 0

    sizes = (pool_w, pool_w, fox_w, fox_w, fox_w, n_fox, fox_w, mem_w, mem_w)
    parts, off = [], 0
    for w in sizes:
        parts.append(w_in[:, off:off + w])
        off += w
    w_ua, w_ga, w_q, w_k, w_v, w_f, w_gb, w_qm, w_gm = parts
    w_f = jnp.concatenate([w_f, jnp.zeros((dm, BF16_ROWS - n_fox), w_in.dtype)], axis=1)
    order = [("ua", w_ua), ("ga", w_ga), ("f", w_f), ("q", w_q), ("k", w_k), ("v", w_v),
             ("gb", w_gb), ("qm", w_qm), ("gm", w_gm)]
    offs, off = {}, 0
    for name, w in order:
        offs[name] = off
        off += w.shape[1]
    wT = jnp.concatenate([w for _, w in order], axis=1).T.astype(BF16)
    n_rows = wT.shape[0]
    tri = (lax.broadcasted_iota(jnp.int32, (ts, ts), 0)
           <= lax.broadcasted_iota(jnp.int32, (ts, ts), 1)).astype(BF16)
    wp_bd = jax.scipy.linalg.block_diag(*[w_pool[g].T for g in range(len(POOL_WINDOWS))]).astype(BF16)
    wkvT = w_mem_kv.T.astype(BF16)
    woT = w_out.T.astype(BF16)

    km, vm = pl.pallas_call(
        _mem_kv_kernel,
        grid=(bsz,),
        in_specs=[
            pl.BlockSpec((1, n_mem, dm), lambda b: (b, 0, 0)),
            pl.BlockSpec((1, dm), lambda b: (0, 0)),
            pl.BlockSpec((2 * mem_w, dm), lambda b: (0, 0)),
            pl.BlockSpec((HEAD_DIM, 1), lambda b: (0, 0)),
        ],
        out_specs=[
            pl.BlockSpec((1, MEM_HEADS, n_mem, QK_ROWS), lambda b: (b, 0, 0, 0)),
            pl.BlockSpec((1, MEM_HEADS, V_ROWS, n_mem), lambda b: (b, 0, 0, 0)),
        ],
        out_shape=[
            jax.ShapeDtypeStruct((bsz, MEM_HEADS, n_mem, QK_ROWS), BF16),
            jax.ShapeDtypeStruct((bsz, MEM_HEADS, V_ROWS, n_mem), BF16),
        ],
        compiler_params=_params(("arbitrary",)),
        name="mem_kv",
    )(mem, mem_norm_g[None, :], wkvT, _col(mem_k_g))

    const2 = lambda b, s: (0, 0)
    k_norm_bound = (math.sqrt(HEAD_DIM) * jnp.max(jnp.abs(fox_k_g))).astype(F32).reshape(1)
    qT, kp, vT, gbT, mam, shift = pl.pallas_call(
        functools.partial(_proj_kernel, offs=offs, n_fox=n_fox, pool_w=pool_w,
                          fox_w=fox_w, mem_w=mem_w),
        grid=(bsz, seq // ts),
        in_specs=[
            pl.BlockSpec(memory_space=pltpu.SMEM),
            pl.BlockSpec((1, ts, dm), lambda b, s: (b, s, 0)),
            pl.BlockSpec((1, dm), const2),
            pl.BlockSpec((n_rows, dm), const2),
            pl.BlockSpec((BF16_ROWS, 1), const2),
            pl.BlockSpec((ts, ts), const2),
            pl.BlockSpec((HEAD_DIM, 1), const2),
            pl.BlockSpec((HEAD_DIM, 1), const2),
            pl.BlockSpec((pool_w, pool_w), const2),
            pl.BlockSpec((pool_w, 1), const2),
            pl.BlockSpec((1, MEM_HEADS, n_mem, QK_ROWS), lambda b, s: (b, 0, 0, 0)),
            pl.BlockSpec((1, MEM_HEADS, V_ROWS, n_mem), lambda b, s: (b, 0, 0, 0)),
            pl.BlockSpec((HEAD_DIM, 1), const2),
        ],
        out_specs=[
            pl.BlockSpec((1, n_fox, QK_ROWS, ts), lambda b, s: (b, 0, 0, s)),
            pl.BlockSpec((1, n_fox, ts, QK_ROWS), lambda b, s: (b, 0, s, 0)),
            pl.BlockSpec((1, n_fox, ts // tk, V_ROWS, tk), lambda b, s: (b, 0, s, 0, 0)),
            pl.BlockSpec((1, fox_w, ts), lambda b, s: (b, 0, s)),
            pl.BlockSpec((1, pool_w + mem_w, ts), lambda b, s: (b, 0, s)),
            pl.BlockSpec((1, BF16_ROWS, ts), lambda b, s: (b, 0, s)),
        ],
        out_shape=[
            jax.ShapeDtypeStruct((bsz, n_fox, QK_ROWS, seq), BF16),
            jax.ShapeDtypeStruct((bsz, n_fox, seq, QK_ROWS), BF16),
            jax.ShapeDtypeStruct((bsz, n_fox, seq // tk, V_ROWS, tk), BF16),
            jax.ShapeDtypeStruct((bsz, fox_w, seq), F32),
            jax.ShapeDtypeStruct((bsz, pool_w + mem_w, seq), BF16),
            jax.ShapeDtypeStruct((bsz, BF16_ROWS, seq), F32),
        ],
        scratch_shapes=[
            pltpu.VMEM((BF16_ROWS, LANES), F32),
            pltpu.VMEM((pool_w, LANES), F32),
        ],
        compiler_params=_params(("arbitrary", "arbitrary")),
        name="proj",
    )(k_norm_bound, x, norm_g[None, :], wT, _col(b_f, BF16_ROWS), tri, _col(fox_q_g), _col(fox_k_g),
      wp_bd, _col(pool_scale), km, vm, _col(mem_q_g))

    fast = (jnp.max(shift) < FAST_MAX_SHIFT).astype(jnp.int32).reshape(1)
    return pl.pallas_call(
        functools.partial(_fox_out_kernel, pool_w=pool_w),
        grid=(bsz, seq // tq),
        in_specs=[
            pl.BlockSpec(memory_space=pltpu.SMEM),
            pl.BlockSpec((1, n_fox, QK_ROWS, tq), lambda b, i: (b, 0, 0, i)),
            pl.BlockSpec((1, n_fox, seq, QK_ROWS), lambda b, i: (b, 0, 0, 0)),
            pl.BlockSpec((1, n_fox, seq // tk, V_ROWS, tk), lambda b, i: (b, 0, 0, 0, 0)),
            pl.BlockSpec((1, fox_w, tq), lambda b, i: (b, 0, i)),
            pl.BlockSpec((1, pool_w + mem_w, tq), lambda b, i: (b, 0, i)),
            pl.BlockSpec((1, tq, dm), lambda b, i: (b, i, 0)),
            pl.BlockSpec((dm, pool_w + fox_w + mem_w), const2),
        ],
        out_specs=pl.BlockSpec((1, tq, dm), lambda b, i: (b, i, 0)),
        out_shape=jax.ShapeDtypeStruct((bsz, seq, dm), x.dtype),
        scratch_shapes=[pltpu.VMEM((n_fox, V_ROWS, tq), F32)],
        compiler_params=_params(("arbitrary", "arbitrary")),
        name="fox_out",
    )(fast, qT, kp, vT, gbT, mam, x, woT)


def kernel(x, mem, norm_g, w_in, b_f, w_pool, pool_scale, fox_q_g, fox_k_g, mem_norm_g,
           w_mem_kv, mem_q_g, mem_k_g, w_out):
    for l in range(norm_g.shape[0]):
        x = _layer(x, mem, norm_g[l], w_in[l], b_f[l], w_pool[l], pool_scale[l], fox_q_g[l],
                   fox_k_g[l], mem_norm_g[l], w_mem_kv[l], mem_q_g[l], mem_k_g[l], w_out[l])
    return x
```

```python
import functools
import math

import jax
import jax.numpy as jnp
from jax import lax
from jax.experimental import pallas as pl
from jax.experimental.pallas import tpu as pltpu

F32 = jnp.float32
BF16 = jnp.bfloat16

HEAD_DIM = 64
EPS = 1e-6
POOL_WINDOWS = (2, 4, 8, 16)
POOL_GROUP_DIM = 64
MEM_HEADS = 4
SCALE = 1.0 / math.sqrt(HEAD_DIM)
LOG2E = math.log2(math.e)
MASK_VALUE = -1e30

LANES = 128
BF16_ROWS = 16
QK_ROWS = 128
AUG_ROWS = BF16_ROWS
V_ROWS = HEAD_DIM + BF16_ROWS

SEQ_TILE = 512
Q_TILE = 512
KV_TILE = 256
FAST_LOOKAHEAD = 3
FAST_MAX_SHIFT = 50.0
BOUND_SLACK = 1.01
VMEM_LIMIT_BYTES = 56 * 1024 * 1024

NT_DIMS = (((1,), (1,)), ((), ()))


def _dot(a, b):
    return jnp.dot(a, b, preferred_element_type=F32)


def _dot_nt(a, b):
    return lax.dot_general(a, b, NT_DIMS, preferred_element_type=F32)


def _silu(g):
    return g * jax.nn.sigmoid(g)


def _log_sigmoid(z):
    return jnp.minimum(z, 0.0) - jnp.log1p(jnp.exp(-jnp.abs(z)))


def _head_rms_scale(t):
    return lax.rsqrt(jnp.mean(t * t, axis=0, keepdims=True) + EPS)


def _ones_row_block(n):
    row = lax.broadcasted_iota(jnp.int32, (BF16_ROWS, n), 0)
    return (row == 0).astype(F32)


def _mem_kv_kernel(mem_ref, g_ref, wkvT_ref, gk_ref, km_ref, vm_ref):
    x = mem_ref[0]
    n_mem = x.shape[0]
    h = (x * lax.rsqrt(jnp.mean(x * x, axis=-1, keepdims=True) + EPS) * g_ref[...]).astype(BF16)
    kvT = _dot_nt(wkvT_ref[...], h)
    width = MEM_HEADS * HEAD_DIM
    ones_blk = _ones_row_block(n_mem)
    zero_pad = jnp.zeros((QK_ROWS - HEAD_DIM, n_mem), F32)
    for hm in range(MEM_HEADS):
        kh = kvT[hm * HEAD_DIM:(hm + 1) * HEAD_DIM]
        kn = kh * _head_rms_scale(kh) * gk_ref[...]
        km_ref[0, hm] = jnp.concatenate([kn, zero_pad], axis=0).T.astype(BF16)
        vh = kvT[width + hm * HEAD_DIM:width + (hm + 1) * HEAD_DIM]
        vm_ref[0, hm] = jnp.concatenate([vh, ones_blk], axis=0).astype(BF16)


def _proj_kernel(kb_ref, x_ref, ng_ref, wT_ref, bf_ref, tri_ref, gq_ref, gk_ref, wp_ref, ps_ref,
                 km_ref, vm_ref, gmq_ref,
                 qT_ref, kp_ref, vT_ref, gbT_ref, mam_ref, shift_ref,
                 fcarry_ref, halo_ref, *, offs, n_fox, pool_w, fox_w, mem_w):
    ts = x_ref.shape[1]
    s_idx = pl.program_id(1)

    @pl.when(s_idx == 0)
    def _():
        fcarry_ref[...] = jnp.zeros_like(fcarry_ref)
        halo_ref[...] = jnp.zeros_like(halo_ref)

    x = x_ref[0]
    h = (x * lax.rsqrt(jnp.mean(x * x, axis=-1, keepdims=True) + EPS) * ng_ref[...]).astype(BF16)

    def proj_t(lo, n):
        return _dot_nt(wT_ref[lo:lo + n, :], h)

    tk = vT_ref.shape[4]
    gd = POOL_GROUP_DIM

    sec_m = proj_t(offs["qm"], 2 * mem_w)
    qm = sec_m[0:mem_w]
    gm = sec_m[mem_w:2 * mem_w]
    sec_a = proj_t(offs["ua"], 2 * pool_w + BF16_ROWS)
    u = sec_a[0:pool_w]
    ga = sec_a[pool_w:2 * pool_w]
    z = sec_a[2 * pool_w:2 * pool_w + BF16_ROWS] + bf_ref[...]

    zero_pad = jnp.zeros((QK_ROWS - HEAD_DIM, ts), F32)
    lgs = []
    for hm in range(MEM_HEADS):
        qh = qm[hm * HEAD_DIM:(hm + 1) * HEAD_DIM]
        qn = qh * _head_rms_scale(qh) * gmq_ref[...] * (SCALE * LOG2E)
        qpad = jnp.concatenate([qn, zero_pad], axis=0).astype(BF16)
        lgs.append(_dot(km_ref[0, hm], qpad))

    qT = proj_t(offs["q"], fox_w)

    logf = _log_sigmoid(z)
    hi = logf.astype(BF16)
    r1 = logf - hi.astype(F32)
    mid = r1.astype(BF16)
    lo = (r1 - mid.astype(F32)).astype(BF16)
    cs = _dot(jnp.concatenate([hi, mid, lo], axis=0), tri_ref[...])

    kT = proj_t(offs["k"], fox_w)
    vT = proj_t(offs["v"], fox_w)

    for hm in range(MEM_HEADS):
        lg = lgs[hm]
        p = jnp.exp2(lg - jnp.max(lg, axis=0, keepdims=True)).astype(BF16)
        yv = _dot(vm_ref[0, hm], p)
        y = yv[0:HEAD_DIM] / yv[HEAD_DIM:HEAD_DIM + 1]
        mm = y * _silu(gm[hm * HEAD_DIM:(hm + 1) * HEAD_DIM])
        mam_ref[0, pool_w + hm * HEAD_DIM:pool_w + (hm + 1) * HEAD_DIM, :] = mm.astype(BF16)

    uext = jnp.concatenate([halo_ref[...], u], axis=1)
    halo_ref[...] = u[:, ts - LANES:]
    pos1 = (s_idx * ts + lax.broadcasted_iota(jnp.int32, (1, ts), 1) + 1).astype(F32)
    acc = uext
    pooled = []
    shift = 1
    for g, w in enumerate(POOL_WINDOWS):
        while shift < w:
            acc = acc + pltpu.roll(acc, shift, 1)
            shift *= 2
        pooled.append(acc[0:gd, LANES:] / jnp.minimum(pos1, float(w)))
        acc = acc[gd:]
    d = (jnp.concatenate(pooled, axis=0) - u).astype(BF16)
    ya = _dot(wp_ref[...], d) * ps_ref[...]
    mam_ref[0, 0:pool_w, :] = (ya * _silu(ga)).astype(BF16)

    gbT_ref[0] = proj_t(offs["gb"], fox_w)

    fcum = (cs[0:BF16_ROWS] + cs[BF16_ROWS:2 * BF16_ROWS] + cs[2 * BF16_ROWS:3 * BF16_ROWS]
            + fcarry_ref[:, LANES - 1:LANES])
    fcarry_ref[...] = fcum[:, ts - LANES:]
    f2 = fcum * LOG2E

    arow = lax.broadcasted_iota(jnp.int32, (AUG_ROWS, ts), 0)

    def split3(v):
        v_hi = v.astype(BF16).astype(F32)
        r = v - v_hi
        v_mid = r.astype(BF16).astype(F32)
        return v_hi, v_mid, (r - v_mid).astype(BF16).astype(F32)

    def rows(*vals):
        out = jnp.zeros((AUG_ROWS, ts), F32)
        for r, v in enumerate(vals):
            out = jnp.where(arow == r, v, out)
        return out

    zero_rows = jnp.zeros((QK_ROWS - HEAD_DIM - AUG_ROWS, ts), F32)
    shifts = jnp.zeros((BF16_ROWS, ts), F32)
    for hd in range(n_fox):
        qh = qT[hd * HEAD_DIM:(hd + 1) * HEAD_DIM]
        qn = qh * _head_rms_scale(qh) * gq_ref[...] * (SCALE * LOG2E)
        m_t = jnp.sqrt(jnp.sum(qn * qn, axis=0, keepdims=True)) * (kb_ref[0] * BOUND_SLACK)
        shifts = jnp.where(arow == hd, m_t, shifts)
        f3, m3 = split3(f2[hd:hd + 1]), split3(m_t)
        qT_ref[0, hd, 0:HEAD_DIM, :] = qn.astype(BF16)
        qT_ref[0, hd, HEAD_DIM:HEAD_DIM + AUG_ROWS, :] = rows(
            f3[0], f3[1], f3[2], 1.0, 1.0, 1.0, -m3[0], -m3[1], -m3[2]).astype(BF16)
        qT_ref[0, hd, HEAD_DIM + AUG_ROWS:, :] = zero_rows.astype(BF16)

        kh = kT[hd * HEAD_DIM:(hd + 1) * HEAD_DIM]
        kn = kh * _head_rms_scale(kh) * gk_ref[...]
        k_aug = rows(1.0, 1.0, 1.0, -f3[0], -f3[1], -f3[2], 1.0, 1.0, 1.0)
        kfull = jnp.concatenate([kn, k_aug, zero_rows], axis=0)
        kp_ref[0, hd] = kfull.T.astype(BF16)
    shift_ref[0] = shifts

    ones_blk = _ones_row_block(ts)
    for hd in range(n_fox):
        va = jnp.concatenate([vT[hd * HEAD_DIM:(hd + 1) * HEAD_DIM], ones_blk], axis=0).astype(BF16)
        for c in range(ts // tk):
            vT_ref[0, hd, c] = va[:, c * tk:(c + 1) * tk]


def _fox_out_kernel(fast_ref, qT_ref, kp_ref, vT_ref, gbT_ref, mam_ref, x_ref, woT_ref, o_ref, acc_ref, *,
                    pool_w):
    hg = qT_ref.shape[1]
    tq = qT_ref.shape[3]
    tk = vT_ref.shape[4]
    n_diag = tq // tk
    i = pl.program_id(1)

    acc_ref[...] = jnp.zeros_like(acc_ref)
    causal = (lax.broadcasted_iota(jnp.int32, (tk, tq), 0)
              <= lax.broadcasted_iota(jnp.int32, (tk, tq), 1))

    def head_scores(hd, j, lo=0):
        start = pl.multiple_of(j * tk, tk)
        return _dot(kp_ref[0, hd, pl.ds(start, tk), :], qT_ref[0, hd, :, lo:])

    def mask(s, diag, lo):
        return s if diag is None else jnp.where(causal[:, :tq - lo], s, MASK_VALUE)

    def fast_tiles(tiles):
        units = [(j, diag, hd) for (j, diag) in tiles for hd in range(hg)]
        scores = {}

        def issue(u):
            j, diag, hd = units[u]
            scores[u] = head_scores(hd, j, 0 if diag is None else diag * tk)

        for u in range(min(FAST_LOOKAHEAD, len(units))):
            issue(u)
        for u, (j, diag, hd) in enumerate(units):
            lo = 0 if diag is None else diag * tk
            p = jnp.exp2(mask(scores.pop(u), diag, lo)).astype(BF16)
            acc_ref[hd, :, lo:] += _dot(vT_ref[0, hd, j], p)
            if u + FAST_LOOKAHEAD < len(units):
                issue(u + FAST_LOOKAHEAD)

    def safe_step(j, ms, diag=None):
        lo = 0 if diag is None else diag * tk
        scores = [head_scores(hd, j, lo) for hd in range(hg)]
        out = []
        for hd in range(hg):
            s = mask(scores[hd], diag, lo)
            m_old = ms[hd][:, lo:]
            m_new = jnp.maximum(m_old, jnp.max(s, axis=0, keepdims=True))
            p = jnp.exp2(s - m_new).astype(BF16)
            alpha = jnp.exp2(m_old - m_new)
            acc_ref[hd, :, lo:] = alpha * acc_ref[hd, :, lo:] + _dot(vT_ref[0, hd, j], p)
            out.append(m_new if lo == 0 else jnp.concatenate([ms[hd][:, :lo], m_new], axis=1))
        return tuple(out)

    def finish():
        mam = mam_ref[0]
        fox_w = hg * HEAD_DIM
        part = (_dot(woT_ref[:, 0:pool_w], mam[0:pool_w])
                + _dot(woT_ref[:, pool_w + fox_w:], mam[pool_w:]))
        ys = []
        for hd in range(hg):
            acc = acc_ref[hd]
            ys.append(acc[0:HEAD_DIM] / acc[HEAD_DIM:HEAD_DIM + 1])
        yb = (jnp.concatenate(ys, axis=0) * _silu(gbT_ref[0])).astype(BF16)
        outT = part + _dot(woT_ref[:, pool_w:pool_w + fox_w], yb)
        o_ref[0] = x_ref[0] + outT.T

    @pl.when(fast_ref[0] != 0)
    def _():
        def body(c, carry):
            fast_tiles([(c * n_diag + d, None) for d in range(n_diag)])
            return carry

        lax.fori_loop(0, i, body, 0)
        fast_tiles([(i * n_diag + c, c) for c in range(n_diag)])
        finish()

    @pl.when(fast_ref[0] == 0)
    def _():
        ms = tuple(jnp.full((1, tq), MASK_VALUE, F32) for _ in range(hg))
        ms = lax.fori_loop(0, i * n_diag, lambda j, c: safe_step(j, c), ms)
        for c in range(n_diag):
            ms = safe_step(i * n_diag + c, ms, diag=c)
        finish()


def _col(v, rows=None):
    v = v.astype(F32)
    if rows is not None and rows > v.shape[0]:
        v = jnp.concatenate([v, jnp.zeros((rows - v.shape[0],), F32)])
    return v[:, None]


def _params(sem, flags=None):
    return pltpu.CompilerParams(dimension_semantics=sem, vmem_limit_bytes=VMEM_LIMIT_BYTES, flags=flags)


def _layer(x, mem, norm_g, w_in, b_f, w_pool, pool_scale, fox_q_g, fox_k_g,
           mem_norm_g, w_mem_kv, mem_q_g, mem_k_g, w_out):
    bsz, seq, dm = x.shape
    n_mem = mem.shape[1]
    n_fox = b_f.shape[0]
    pool_w = pool_scale.shape[0]
    fox_w = n_fox * HEAD_DIM
    mem_w = MEM_HEADS * HEAD_DIM
    assert n_fox <= BF16_ROWS and pool_w == len(POOL_WINDOWS) * POOL_GROUP_DIM
    assert w_in.shape[1] == 2 * pool_w + 4 * fox_w + n_fox + 2 * mem_w
    assert w_out.shape[0] == pool_w + fox_w + mem_w
    ts, tq, tk = SEQ_TILE, Q_TILE, KV_TILE
    assert seq % ts == 0 and seq % tq == 0 and tq % tk == 0 and ts % tk == 0

    sizes = (pool_w, pool_w, fox_w, fox_w, fox_w, n_fox, fox_w, mem_w, mem_w)
    parts, off = [], 0
    for w in sizes:
        parts.append(w_in[:, off:off + w])
        off += w
    w_ua, w_ga, w_q, w_k, w_v, w_f, w_gb, w_qm, w_gm = parts
    w_f = jnp.concatenate([w_f, jnp.zeros((dm, BF16_ROWS - n_fox), w_in.dtype)], axis=1)
    order = [("ua", w_ua), ("ga", w_ga), ("f", w_f), ("q", w_q), ("k", w_k), ("v", w_v),
             ("gb", w_gb), ("qm", w_qm), ("gm", w_gm)]
    offs, off = {}, 0
    for name, w in order:
        offs[name] = off
        off += w.shape[1]
    wT = jnp.concatenate([w for _, w in order], axis=1).T.astype(BF16)
    n_rows = wT.shape[0]
    tri = (lax.broadcasted_iota(jnp.int32, (ts, ts), 0)
           <= lax.broadcasted_iota(jnp.int32, (ts, ts), 1)).astype(BF16)
    wp_bd = jax.scipy.linalg.block_diag(*[w_pool[g].T for g in range(len(POOL_WINDOWS))]).astype(BF16)
    wkvT = w_mem_kv.T.astype(BF16)
    woT = w_out.T.astype(BF16)

    km, vm = pl.pallas_call(
        _mem_kv_kernel,
        grid=(bsz,),
        in_specs=[
            pl.BlockSpec((1, n_mem, dm), lambda b: (b, 0, 0)),
            pl.BlockSpec((1, dm), lambda b: (0, 0)),
            pl.BlockSpec((2 * mem_w, dm), lambda b: (0, 0)),
            pl.BlockSpec((HEAD_DIM, 1), lambda b: (0, 0)),
        ],
        out_specs=[
            pl.BlockSpec((1, MEM_HEADS, n_mem, QK_ROWS), lambda b: (b, 0, 0, 0)),
            pl.BlockSpec((1, MEM_HEADS, V_ROWS, n_mem), lambda b: (b, 0, 0, 0)),
        ],
        out_shape=[
            jax.ShapeDtypeStruct((bsz, MEM_HEADS, n_mem, QK_ROWS), BF16),
            jax.ShapeDtypeStruct((bsz, MEM_HEADS, V_ROWS, n_mem), BF16),
        ],
        compiler_params=_params(("arbitrary",)),
        name="mem_kv",
    )(mem, mem_norm_g[None, :], wkvT, _col(mem_k_g))

    const2 = lambda b, s: (0, 0)
    k_norm_bound = (math.sqrt(HEAD_DIM) * jnp.max(jnp.abs(fox_k_g))).astype(F32).reshape(1)
    qT, kp, vT, gbT, mam, shift = pl.pallas_call(
        functools.partial(_proj_kernel, offs=offs, n_fox=n_fox, pool_w=pool_w,
                          fox_w=fox_w, mem_w=mem_w),
        grid=(bsz, seq // ts),
        in_specs=[
            pl.BlockSpec(memory_space=pltpu.SMEM),
            pl.BlockSpec((1, ts, dm), lambda b, s: (b, s, 0)),
            pl.BlockSpec((1, dm), const2),
            pl.BlockSpec((n_rows, dm), const2),
            pl.BlockSpec((BF16_ROWS, 1), const2),
            pl.BlockSpec((ts, ts), const2),
            pl.BlockSpec((HEAD_DIM, 1), const2),
            pl.BlockSpec((HEAD_DIM, 1), const2),
            pl.BlockSpec((pool_w, pool_w), const2),
            pl.BlockSpec((pool_w, 1), const2),
            pl.BlockSpec((1, MEM_HEADS, n_mem, QK_ROWS), lambda b, s: (b, 0, 0, 0)),
            pl.BlockSpec((1, MEM_HEADS, V_ROWS, n_mem), lambda b, s: (b, 0, 0, 0)),
            pl.BlockSpec((HEAD_DIM, 1), const2),
        ],
        out_specs=[
            pl.BlockSpec((1, n_fox, QK_ROWS, ts), lambda b, s: (b, 0, 0, s)),
            pl.BlockSpec((1, n_fox, ts, QK_ROWS), lambda b, s: (b, 0, s, 0)),
            pl.BlockSpec((1, n_fox, ts // tk, V_ROWS, tk), lambda b, s: (b, 0, s, 0, 0)),
            pl.BlockSpec((1, fox_w, ts), lambda b, s: (b, 0, s)),
            pl.BlockSpec((1, pool_w + mem_w, ts), lambda b, s: (b, 0, s)),
            pl.BlockSpec((1, BF16_ROWS, ts), lambda b, s: (b, 0, s)),
        ],
        out_shape=[
            jax.ShapeDtypeStruct((bsz, n_fox, QK_ROWS, seq), BF16),
            jax.ShapeDtypeStruct((bsz, n_fox, seq, QK_ROWS), BF16),
            jax.ShapeDtypeStruct((bsz, n_fox, seq // tk, V_ROWS, tk), BF16),
            jax.ShapeDtypeStruct((bsz, fox_w, seq), F32),
            jax.ShapeDtypeStruct((bsz, pool_w + mem_w, seq), BF16),
            jax.ShapeDtypeStruct((bsz, BF16_ROWS, seq), F32),
        ],
        scratch_shapes=[
            pltpu.VMEM((BF16_ROWS, LANES), F32),
            pltpu.VMEM((pool_w, LANES), F32),
        ],
        compiler_params=_params(("arbitrary", "arbitrary")),
        name="proj",
    )(k_norm_bound, x, norm_g[None, :], wT, _col(b_f, BF16_ROWS), tri, _col(fox_q_g), _col(fox_k_g),
      wp_bd, _col(pool_scale), km, vm, _col(mem_q_g))

    fast = (jnp.max(shift) < FAST_MAX_SHIFT).astype(jnp.int32).reshape(1)
    return pl.pallas_call(
        functools.partial(_fox_out_kernel, pool_w=pool_w),
        grid=(bsz, seq // tq),
        in_specs=[
            pl.BlockSpec(memory_space=pltpu.SMEM),
            pl.BlockSpec((1, n_fox, QK_ROWS, tq), lambda b, i: (b, 0, 0, i)),
            pl.BlockSpec((1, n_fox, seq, QK_ROWS), lambda b, i: (b, 0, 0, 0)),
            pl.BlockSpec((1, n_fox, seq // tk, V_ROWS, tk), lambda b, i: (b, 0, 0, 0, 0)),
            pl.BlockSpec((1, fox_w, tq), lambda b, i: (b, 0, i)),
            pl.BlockSpec((1, pool_w + mem_w, tq), lambda b, i: (b, 0, i)),
            pl.BlockSpec((1, tq, dm), lambda b, i: (b, i, 0)),
            pl.BlockSpec((dm, pool_w + fox_w + mem_w), const2),
        ],
        out_specs=pl.BlockSpec((1, tq, dm), lambda b, i: (b, i, 0)),
        out_shape=jax.ShapeDtypeStruct((bsz, seq, dm), x.dtype),
        scratch_shapes=[pltpu.VMEM((n_fox, V_ROWS, tq), F32)],
        compiler_params=_params(("arbitrary", "arbitrary")),
        name="fox_out",
    )(fast, qT, kp, vT, gbT, mam, x, woT)


def kernel(x, mem, norm_g, w_in, b_f, w_pool, pool_scale, fox_q_g, fox_k_g, mem_norm_g,
           w_mem_kv, mem_q_g, mem_k_g, w_out):
    for l in range(norm_g.shape[0]):
        x = _layer(x, mem, norm_g[l], w_in[l], b_f[l], w_pool[l], pool_scale[l], fox_q_g[l],
                   fox_k_g[l], mem_norm_g[l], w_mem_kv[l], mem_q_g[l], mem_k_g[l], w_out[l])
    return x
```

```python
import functools
import math

import jax
import jax.numpy as jnp
from jax import lax
from jax.experimental import pallas as pl
from jax.experimental.pallas import tpu as pltpu

F32 = jnp.float32
BF16 = jnp.bfloat16

HEAD_DIM = 64
EPS = 1e-6
POOL_WINDOWS = (2, 4, 8, 16)
POOL_GROUP_DIM = 64
MEM_HEADS = 4
SCALE = 1.0 / math.sqrt(HEAD_DIM)
LOG2E = math.log2(math.e)
MASK_VALUE = -1e30

LANES = 128
BF16_ROWS = 16
QK_ROWS = 128
AUG_ROWS = BF16_ROWS
V_ROWS = HEAD_DIM + BF16_ROWS

SEQ_TILE = 512
Q_TILE = 512
KV_TILE = 256
FAST_TILES_PER_TRIP = 4
FAST_LOOKAHEAD = 3
FAST_MAX_SHIFT = 50.0
BOUND_SLACK = 1.01
VMEM_LIMIT_BYTES = 56 * 1024 * 1024

NT_DIMS = (((1,), (1,)), ((), ()))


def _dot(a, b):
    return jnp.dot(a, b, preferred_element_type=F32)


def _dot_nt(a, b):
    return lax.dot_general(a, b, NT_DIMS, preferred_element_type=F32)


def _silu(g):
    return g * jax.nn.sigmoid(g)


def _log_sigmoid(z):
    return jnp.minimum(z, 0.0) - jnp.log1p(jnp.exp(-jnp.abs(z)))


def _head_rms_scale(t):
    return lax.rsqrt(jnp.mean(t * t, axis=0, keepdims=True) + EPS)


def _ones_row_block(n):
    row = lax.broadcasted_iota(jnp.int32, (BF16_ROWS, n), 0)
    return (row == 0).astype(F32)


def _mem_kv_kernel(mem_ref, g_ref, wkvT_ref, gk_ref, km_ref, vm_ref):
    x = mem_ref[0]
    n_mem = x.shape[0]
    h = (x * lax.rsqrt(jnp.mean(x * x, axis=-1, keepdims=True) + EPS) * g_ref[...]).astype(BF16)
    kvT = _dot_nt(wkvT_ref[...], h)
    width = MEM_HEADS * HEAD_DIM
    ones_blk = _ones_row_block(n_mem)
    zero_pad = jnp.zeros((QK_ROWS - HEAD_DIM, n_mem), F32)
    for hm in range(MEM_HEADS):
        kh = kvT[hm * HEAD_DIM:(hm + 1) * HEAD_DIM]
        kn = kh * _head_rms_scale(kh) * gk_ref[...]
        km_ref[0, hm] = jnp.concatenate([kn, zero_pad], axis=0).T.astype(BF16)
        vh = kvT[width + hm * HEAD_DIM:width + (hm + 1) * HEAD_DIM]
        vm_ref[0, hm] = jnp.concatenate([vh, ones_blk], axis=0).astype(BF16)


def _proj_kernel(kb_ref, x_ref, ng_ref, wT_ref, bf_ref, tri_ref, gq_ref, gk_ref, wp_ref, ps_ref,
                 km_ref, vm_ref, gmq_ref,
                 qT_ref, kp_ref, vT_ref, gbT_ref, mam_ref, shift_ref,
                 fcarry_ref, halo_ref, *, offs, n_fox, pool_w, fox_w, mem_w):
    ts = x_ref.shape[1]
    s_idx = pl.program_id(1)

    @pl.when(s_idx == 0)
    def _():
        fcarry_ref[...] = jnp.zeros_like(fcarry_ref)
        halo_ref[...] = jnp.zeros_like(halo_ref)

    x = x_ref[0]
    h = (x * lax.rsqrt(jnp.mean(x * x, axis=-1, keepdims=True) + EPS) * ng_ref[...]).astype(BF16)

    def proj_t(lo, n):
        return _dot_nt(wT_ref[lo:lo + n, :], h)

    tk = vT_ref.shape[4]
    gd = POOL_GROUP_DIM

    sec_m = proj_t(offs["qm"], 2 * mem_w)
    qm = sec_m[0:mem_w]
    gm = sec_m[mem_w:2 * mem_w]
    sec_a = proj_t(offs["ua"], 2 * pool_w + BF16_ROWS)
    u = sec_a[0:pool_w]
    ga = sec_a[pool_w:2 * pool_w]
    z = sec_a[2 * pool_w:2 * pool_w + BF16_ROWS] + bf_ref[...]

    zero_pad = jnp.zeros((QK_ROWS - HEAD_DIM, ts), F32)
    lgs = []
    for hm in range(MEM_HEADS):
        qh = qm[hm * HEAD_DIM:(hm + 1) * HEAD_DIM]
        qn = qh * _head_rms_scale(qh) * gmq_ref[...] * (SCALE * LOG2E)
        qpad = jnp.concatenate([qn, zero_pad], axis=0).astype(BF16)
        lgs.append(_dot(km_ref[0, hm], qpad))

    qT = proj_t(offs["q"], fox_w)

    logf = _log_sigmoid(z)
    hi = logf.astype(BF16)
    r1 = logf - hi.astype(F32)
    mid = r1.astype(BF16)
    lo = (r1 - mid.astype(F32)).astype(BF16)
    cs = _dot(jnp.concatenate([hi, mid, lo], axis=0), tri_ref[...])

    kT = proj_t(offs["k"], fox_w)
    vT = proj_t(offs["v"], fox_w)

    for hm in range(MEM_HEADS):
        lg = lgs[hm]
        p = jnp.exp2(lg - jnp.max(lg, axis=0, keepdims=True)).astype(BF16)
        yv = _dot(vm_ref[0, hm], p)
        y = yv[0:HEAD_DIM] / yv[HEAD_DIM:HEAD_DIM + 1]
        mm = y * _silu(gm[hm * HEAD_DIM:(hm + 1) * HEAD_DIM])
        mam_ref[0, pool_w + hm * HEAD_DIM:pool_w + (hm + 1) * HEAD_DIM, :] = mm.astype(BF16)

    uext = jnp.concatenate([halo_ref[...], u], axis=1)
    halo_ref[...] = u[:, ts - LANES:]
    pos1 = (s_idx * ts + lax.broadcasted_iota(jnp.int32, (1, ts), 1) + 1).astype(F32)
    acc = uext
    pooled = []
    shift = 1
    for g, w in enumerate(POOL_WINDOWS):
        while shift < w:
            acc = acc + pltpu.roll(acc, shift, 1)
            shift *= 2
        pooled.append(acc[0:gd, LANES:] / jnp.minimum(pos1, float(w)))
        acc = acc[gd:]
    d = (jnp.concatenate(pooled, axis=0) - u).astype(BF16)
    ya = _dot(wp_ref[...], d) * ps_ref[...]
    mam_ref[0, 0:pool_w, :] = (ya * _silu(ga)).astype(BF16)

    gbT_ref[0] = proj_t(offs["gb"], fox_w)

    fcum = (cs[0:BF16_ROWS] + cs[BF16_ROWS:2 * BF16_ROWS] + cs[2 * BF16_ROWS:3 * BF16_ROWS]
            + fcarry_ref[:, LANES - 1:LANES])
    fcarry_ref[...] = fcum[:, ts - LANES:]
    f2 = fcum * LOG2E

    arow = lax.broadcasted_iota(jnp.int32, (AUG_ROWS, ts), 0)

    def split3(v):
        v_hi = v.astype(BF16).astype(F32)
        r = v - v_hi
        v_mid = r.astype(BF16).astype(F32)
        return v_hi, v_mid, (r - v_mid).astype(BF16).astype(F32)

    def rows(*vals):
        out = jnp.zeros((AUG_ROWS, ts), F32)
        for r, v in enumerate(vals):
            out = jnp.where(arow == r, v, out)
        return out

    zero_rows = jnp.zeros((QK_ROWS - HEAD_DIM - AUG_ROWS, ts), F32)
    shifts = jnp.zeros((BF16_ROWS, ts), F32)
    for hd in range(n_fox):
        qh = qT[hd * HEAD_DIM:(hd + 1) * HEAD_DIM]
        qn = qh * _head_rms_scale(qh) * gq_ref[...] * (SCALE * LOG2E)
        m_t = jnp.sqrt(jnp.sum(qn * qn, axis=0, keepdims=True)) * (kb_ref[0] * BOUND_SLACK)
        shifts = jnp.where(arow == hd, m_t, shifts)
        f3, m3 = split3(f2[hd:hd + 1]), split3(m_t)
        qT_ref[0, hd, 0:HEAD_DIM, :] = qn.astype(BF16)
        qT_ref[0, hd, HEAD_DIM:HEAD_DIM + AUG_ROWS, :] = rows(
            f3[0], f3[1], f3[2], 1.0, 1.0, 1.0, -m3[0], -m3[1], -m3[2]).astype(BF16)
        qT_ref[0, hd, HEAD_DIM + AUG_ROWS:, :] = zero_rows.astype(BF16)

        kh = kT[hd * HEAD_DIM:(hd + 1) * HEAD_DIM]
        kn = kh * _head_rms_scale(kh) * gk_ref[...]
        k_aug = rows(1.0, 1.0, 1.0, -f3[0], -f3[1], -f3[2], 1.0, 1.0, 1.0)
        kfull = jnp.concatenate([kn, k_aug, zero_rows], axis=0)
        kp_ref[0, hd] = kfull.T.astype(BF16)
    shift_ref[0] = shifts

    ones_blk = _ones_row_block(ts)
    for hd in range(n_fox):
        va = jnp.concatenate([vT[hd * HEAD_DIM:(hd + 1) * HEAD_DIM], ones_blk], axis=0).astype(BF16)
        for c in range(ts // tk):
            vT_ref[0, hd, c] = va[:, c * tk:(c + 1) * tk]


def _fox_out_kernel(fast_ref, qT_ref, kp_ref, vT_ref, gbT_ref, mam_ref, x_ref, woT_ref, o_ref, acc_ref, *,
                    pool_w):
    hg = qT_ref.shape[1]
    tq = qT_ref.shape[3]
    tk = vT_ref.shape[4]
    n_diag = tq // tk
    i = pl.program_id(1)

    acc_ref[...] = jnp.zeros_like(acc_ref)
    causal = (lax.broadcasted_iota(jnp.int32, (tk, tq), 0)
              <= lax.broadcasted_iota(jnp.int32, (tk, tq), 1))

    def head_scores(hd, j, lo=0):
        start = pl.multiple_of(j * tk, tk)
        return _dot(kp_ref[0, hd, pl.ds(start, tk), :], qT_ref[0, hd, :, lo:])

    def mask(s, diag, lo):
        return s if diag is None else jnp.where(causal[:, :tq - lo], s, MASK_VALUE)

    def fast_tiles(tiles):
        units = [(j, diag, hd) for (j, diag) in tiles for hd in range(hg)]
        scores = {}

        def issue(u):
            j, diag, hd = units[u]
            scores[u] = head_scores(hd, j, 0 if diag is None else diag * tk)

        for u in range(min(FAST_LOOKAHEAD, len(units))):
            issue(u)
        for u, (j, diag, hd) in enumerate(units):
            lo = 0 if diag is None else diag * tk
            p = jnp.exp2(mask(scores.pop(u), diag, lo)).astype(BF16)
            acc_ref[hd, :, lo:] += _dot(vT_ref[0, hd, j], p)
            if u + FAST_LOOKAHEAD < len(units):
                issue(u + FAST_LOOKAHEAD)

    def safe_step(j, ms, diag=None):
        lo = 0 if diag is None else diag * tk
        scores = [head_scores(hd, j, lo) for hd in range(hg)]
        out = []
        for hd in range(hg):
            s = mask(scores[hd], diag, lo)
            m_old = ms[hd][:, lo:]
            m_new = jnp.maximum(m_old, jnp.max(s, axis=0, keepdims=True))
            p = jnp.exp2(s - m_new).astype(BF16)
            alpha = jnp.exp2(m_old - m_new)
            acc_ref[hd, :, lo:] = alpha * acc_ref[hd, :, lo:] + _dot(vT_ref[0, hd, j], p)
            out.append(m_new if lo == 0 else jnp.concatenate([ms[hd][:, :lo], m_new], axis=1))
        return tuple(out)

    def finish():
        mam = mam_ref[0]
        fox_w = hg * HEAD_DIM
        part = (_dot(woT_ref[:, 0:pool_w], mam[0:pool_w])
                + _dot(woT_ref[:, pool_w + fox_w:], mam[pool_w:]))
        ys = []
        for hd in range(hg):
            acc = acc_ref[hd]
            ys.append(acc[0:HEAD_DIM] / acc[HEAD_DIM:HEAD_DIM + 1])
        yb = (jnp.concatenate(ys, axis=0) * _silu(gbT_ref[0])).astype(BF16)
        outT = part + _dot(woT_ref[:, pool_w:pool_w + fox_w], yb)
        o_ref[0] = x_ref[0] + outT.T

    @pl.when(fast_ref[0] != 0)
    def _():
        def below_diagonal(first, count):
            fast_tiles([(first + d, None) for d in range(count)])

        def body(c, carry):
            below_diagonal(c * FAST_TILES_PER_TRIP, FAST_TILES_PER_TRIP)
            return carry

        n_below = i * n_diag
        n_trips = n_below // FAST_TILES_PER_TRIP
        lax.fori_loop(0, n_trips, body, 0)
        for rest in range(n_diag, FAST_TILES_PER_TRIP, n_diag):
            @pl.when(n_below - n_trips * FAST_TILES_PER_TRIP == rest)
            def _():
                below_diagonal(n_trips * FAST_TILES_PER_TRIP, rest)
        fast_tiles([(n_below + c, c) for c in range(n_diag)])
        finish()

    @pl.when(fast_ref[0] == 0)
    def _():
        ms = tuple(jnp.full((1, tq), MASK_VALUE, F32) for _ in range(hg))
        ms = lax.fori_loop(0, i * n_diag, lambda j, c: safe_step(j, c), ms)
        for c in range(n_diag):
            ms = safe_step(i * n_diag + c, ms, diag=c)
        finish()


def _col(v, rows=None):
    v = v.astype(F32)
    if rows is not None and rows > v.shape[0]:
        v = jnp.concatenate([v, jnp.zeros((rows - v.shape[0],), F32)])
    return v[:, None]


def _params(sem, flags=None):
    return pltpu.CompilerParams(dimension_semantics=sem, vmem_limit_bytes=VMEM_LIMIT_BYTES, flags=flags)


def _layer(x, mem, norm_g, w_in, b_f, w_pool, pool_scale, fox_q_g, fox_k_g,
           mem_norm_g, w_mem_kv, mem_q_g, mem_k_g, w_out):
    bsz, seq, dm = x.shape
    n_mem = mem.shape[1]
    n_fox = b_f.shape[0]
    pool_w = pool_scale.shape[0]
    fox_w = n_fox * HEAD_DIM
    mem_w = MEM_HEADS * HEAD_DIM
    assert n_fox <= BF16_ROWS and pool_w == len(POOL_WINDOWS) * POOL_GROUP_DIM
    assert w_in.shape[1] == 2 * pool_w + 4 * fox_w + n_fox + 2 * mem_w
    assert w_out.shape[0] == pool_w + fox_w + mem_w
    ts, tq, tk = SEQ_TILE, Q_TILE, KV_TILE
    assert seq % ts == 0 and seq % tq == 0 and tq % tk == 0 and ts % tk == 0

    sizes = (pool_w, pool_w, fox_w, fox_w, fox_w, n_fox, fox_w, mem_w, mem_w)
    parts, off = [], 0
    for w in sizes:
        parts.append(w_in[:, off:off + w])
        off += w
    w_ua, w_ga, w_q, w_k, w_v, w_f, w_gb, w_qm, w_gm = parts
    w_f = jnp.concatenate([w_f, jnp.zeros((dm, BF16_ROWS - n_fox), w_in.dtype)], axis=1)
    order = [("ua", w_ua), ("ga", w_ga), ("f", w_f), ("q", w_q), ("k", w_k), ("v", w_v),
             ("gb", w_gb), ("qm", w_qm), ("gm", w_gm)]
    offs, off = {}, 0
    for name, w in order:
        offs[name] = off
        off += w.shape[1]
    wT = jnp.concatenate([w for _, w in order], axis=1).T.astype(BF16)
    n_rows = wT.shape[0]
    tri = (lax.broadcasted_iota(jnp.int32, (ts, ts), 0)
           <= lax.broadcasted_iota(jnp.int32, (ts, ts), 1)).astype(BF16)
    wp_bd = jax.scipy.linalg.block_diag(*[w_pool[g].T for g in range(len(POOL_WINDOWS))]).astype(BF16)
    wkvT = w_mem_kv.T.astype(BF16)
    woT = w_out.T.astype(BF16)

    km, vm = pl.pallas_call(
        _mem_kv_kernel,
        grid=(bsz,),
        in_specs=[
            pl.BlockSpec((1, n_mem, dm), lambda b: (b, 0, 0)),
            pl.BlockSpec((1, dm), lambda b: (0, 0)),
            pl.BlockSpec((2 * mem_w, dm), lambda b: (0, 0)),
            pl.BlockSpec((HEAD_DIM, 1), lambda b: (0, 0)),
        ],
        out_specs=[
            pl.BlockSpec((1, MEM_HEADS, n_mem, QK_ROWS), lambda b: (b, 0, 0, 0)),
            pl.BlockSpec((1, MEM_HEADS, V_ROWS, n_mem), lambda b: (b, 0, 0, 0)),
        ],
        out_shape=[
            jax.ShapeDtypeStruct((bsz, MEM_HEADS, n_mem, QK_ROWS), BF16),
            jax.ShapeDtypeStruct((bsz, MEM_HEADS, V_ROWS, n_mem), BF16),
        ],
        compiler_params=_params(("arbitrary",)),
        name="mem_kv",
    )(mem, mem_norm_g[None, :], wkvT, _col(mem_k_g))

    const2 = lambda b, s: (0, 0)
    k_norm_bound = (math.sqrt(HEAD_DIM) * jnp.max(jnp.abs(fox_k_g))).astype(F32).reshape(1)
    qT, kp, vT, gbT, mam, shift = pl.pallas_call(
        functools.partial(_proj_kernel, offs=offs, n_fox=n_fox, pool_w=pool_w,
                          fox_w=fox_w, mem_w=mem_w),
        grid=(bsz, seq // ts),
        in_specs=[
            pl.BlockSpec(memory_space=pltpu.SMEM),
            pl.BlockSpec((1, ts, dm), lambda b, s: (b, s, 0)),
            pl.BlockSpec((1, dm), const2),
            pl.BlockSpec((n_rows, dm), const2),
            pl.BlockSpec((BF16_ROWS, 1), const2),
            pl.BlockSpec((ts, ts), const2),
            pl.BlockSpec((HEAD_DIM, 1), const2),
            pl.BlockSpec((HEAD_DIM, 1), const2),
            pl.BlockSpec((pool_w, pool_w), const2),
            pl.BlockSpec((pool_w, 1), const2),
            pl.BlockSpec((1, MEM_HEADS, n_mem, QK_ROWS), lambda b, s: (b, 0, 0, 0)),
            pl.BlockSpec((1, MEM_HEADS, V_ROWS, n_mem), lambda b, s: (b, 0, 0, 0)),
            pl.BlockSpec((HEAD_DIM, 1), const2),
        ],
        out_specs=[
            pl.BlockSpec((1, n_fox, QK_ROWS, ts), lambda b, s: (b, 0, 0, s)),
            pl.BlockSpec((1, n_fox, ts, QK_ROWS), lambda b, s: (b, 0, s, 0)),
            pl.BlockSpec((1, n_fox, ts // tk, V_ROWS, tk), lambda b, s: (b, 0, s, 0, 0)),
            pl.BlockSpec((1, fox_w, ts), lambda b, s: (b, 0, s)),
            pl.BlockSpec((1, pool_w + mem_w, ts), lambda b, s: (b, 0, s)),
            pl.BlockSpec((1, BF16_ROWS, ts), lambda b, s: (b, 0, s)),
        ],
        out_shape=[
            jax.ShapeDtypeStruct((bsz, n_fox, QK_ROWS, seq), BF16),
            jax.ShapeDtypeStruct((bsz, n_fox, seq, QK_ROWS), BF16),
            jax.ShapeDtypeStruct((bsz, n_fox, seq // tk, V_ROWS, tk), BF16),
            jax.ShapeDtypeStruct((bsz, fox_w, seq), F32),
            jax.ShapeDtypeStruct((bsz, pool_w + mem_w, seq), BF16),
            jax.ShapeDtypeStruct((bsz, BF16_ROWS, seq), F32),
        ],
        scratch_shapes=[
            pltpu.VMEM((BF16_ROWS, LANES), F32),
            pltpu.VMEM((pool_w, LANES), F32),
        ],
        compiler_params=_params(("arbitrary", "arbitrary")),
        name="proj",
    )(k_norm_bound, x, norm_g[None, :], wT, _col(b_f, BF16_ROWS), tri, _col(fox_q_g), _col(fox_k_g),
      wp_bd, _col(pool_scale), km, vm, _col(mem_q_g))

    fast = (jnp.max(shift) < FAST_MAX_SHIFT).astype(jnp.int32).reshape(1)
    return pl.pallas_call(
        functools.partial(_fox_out_kernel, pool_w=pool_w),
        grid=(bsz, seq // tq),
        in_specs=[
            pl.BlockSpec(memory_space=pltpu.SMEM),
            pl.BlockSpec((1, n_fox, QK_ROWS, tq), lambda b, i: (b, 0, 0, i)),
            pl.BlockSpec((1, n_fox, seq, QK_ROWS), lambda b, i: (b, 0, 0, 0)),
            pl.BlockSpec((1, n_fox, seq // tk, V_ROWS, tk), lambda b, i: (b, 0, 0, 0, 0)),
            pl.BlockSpec((1, fox_w, tq), lambda b, i: (b, 0, i)),
            pl.BlockSpec((1, pool_w + mem_w, tq), lambda b, i: (b, 0, i)),
            pl.BlockSpec((1, tq, dm), lambda b, i: (b, i, 0)),
            pl.BlockSpec((dm, pool_w + fox_w + mem_w), const2),
        ],
        out_specs=pl.BlockSpec((1, tq, dm), lambda b, i: (b, i, 0)),
        out_shape=jax.ShapeDtypeStruct((bsz, seq, dm), x.dtype),
        scratch_shapes=[pltpu.VMEM((n_fox, V_ROWS, tq), F32)],
        compiler_params=_params(("arbitrary", "arbitrary")),
        name="fox_out",
    )(fast, qT, kp, vT, gbT, mam, x, woT)


def kernel(x, mem, norm_g, w_in, b_f, w_pool, pool_scale, fox_q_g, fox_k_g, mem_norm_g,
           w_mem_kv, mem_q_g, mem_k_g, w_out):
    for l in range(norm_g.shape[0]):
        x = _layer(x, mem, norm_g[l], w_in[l], b_f[l], w_pool[l], pool_scale[l], fox_q_g[l],
                   fox_k_g[l], mem_norm_g[l], w_mem_kv[l], mem_q_g[l], mem_k_g[l], w_out[l])
    return x
```

```python
import functools
import math

import jax
import jax.numpy as jnp
from jax import lax
from jax.experimental import pallas as pl
from jax.experimental.pallas import tpu as pltpu

F32 = jnp.float32
BF16 = jnp.bfloat16

HEAD_DIM = 64
EPS = 1e-6
POOL_WINDOWS = (2, 4, 8, 16)
POOL_GROUP_DIM = 64
MEM_HEADS = 4
SCALE = 1.0 / math.sqrt(HEAD_DIM)
LOG2E = math.log2(math.e)
MASK_VALUE = -1e30

LANES = 128
BF16_ROWS = 16
QK_ROWS = 128
AUG_ROWS = BF16_ROWS
V_ROWS = HEAD_DIM + BF16_ROWS

SEQ_TILE = 512
Q_TILE = 512
KV_TILE = 256
FAST_TILES_PER_TRIP = 4
FAST_LOOKAHEAD = 3
FAST_MAX_SHIFT = 50.0
BOUND_SLACK = 1.01
VMEM_LIMIT_BYTES = 56 * 1024 * 1024

NT_DIMS = (((1,), (1,)), ((), ()))


def _dot(a, b):
    return jnp.dot(a, b, preferred_element_type=F32)


def _dot_nt(a, b):
    return lax.dot_general(a, b, NT_DIMS, preferred_element_type=F32)


def _silu(g):
    return g * jax.nn.sigmoid(g)


def _log_sigmoid(z):
    return jnp.minimum(z, 0.0) - jnp.log1p(jnp.exp(-jnp.abs(z)))


def _head_rms_scale(t):
    return lax.rsqrt(jnp.mean(t * t, axis=0, keepdims=True) + EPS)


def _ones_row_block(n):
    row = lax.broadcasted_iota(jnp.int32, (BF16_ROWS, n), 0)
    return (row == 0).astype(F32)


def _mem_kv_kernel(mem_ref, g_ref, wkvT_ref, gk_ref, km_ref, vm_ref):
    x = mem_ref[0]
    n_mem = x.shape[0]
    h = (x * lax.rsqrt(jnp.mean(x * x, axis=-1, keepdims=True) + EPS) * g_ref[...]).astype(BF16)
    kvT = _dot_nt(wkvT_ref[...], h)
    width = MEM_HEADS * HEAD_DIM
    ones_blk = _ones_row_block(n_mem)
    zero_pad = jnp.zeros((QK_ROWS - HEAD_DIM, n_mem), F32)
    for hm in range(MEM_HEADS):
        kh = kvT[hm * HEAD_DIM:(hm + 1) * HEAD_DIM]
        kn = kh * _head_rms_scale(kh) * gk_ref[...]
        km_ref[0, hm] = jnp.concatenate([kn, zero_pad], axis=0).T.astype(BF16)
        vh = kvT[width + hm * HEAD_DIM:width + (hm + 1) * HEAD_DIM]
        vm_ref[0, hm] = jnp.concatenate([vh, ones_blk], axis=0).astype(BF16)


def _proj_kernel(mb_ref, x_ref, ng_ref, wT_ref, bf_ref, tri_ref, gq_ref, gk_ref, wp_ref, ps_ref,
                 km_ref, vm_ref, gmq_ref,
                 qT_ref, kp_ref, vT_ref, gbT_ref, mam_ref,
                 fcarry_ref, halo_ref, *, offs, n_fox, pool_w, fox_w, mem_w):
    ts = x_ref.shape[1]
    s_idx = pl.program_id(1)

    @pl.when(s_idx == 0)
    def _():
        fcarry_ref[...] = jnp.zeros_like(fcarry_ref)
        halo_ref[...] = jnp.zeros_like(halo_ref)

    x = x_ref[0]
    h = (x * lax.rsqrt(jnp.mean(x * x, axis=-1, keepdims=True) + EPS) * ng_ref[...]).astype(BF16)

    def proj_t(lo, n):
        return _dot_nt(wT_ref[lo:lo + n, :], h)

    tk = vT_ref.shape[4]
    gd = POOL_GROUP_DIM

    sec_m = proj_t(offs["qm"], 2 * mem_w)
    qm = sec_m[0:mem_w]
    gm = sec_m[mem_w:2 * mem_w]
    sec_a = proj_t(offs["ua"], 2 * pool_w + BF16_ROWS)
    u = sec_a[0:pool_w]
    ga = sec_a[pool_w:2 * pool_w]
    z = sec_a[2 * pool_w:2 * pool_w + BF16_ROWS] + bf_ref[...]

    zero_pad = jnp.zeros((QK_ROWS - HEAD_DIM, ts), F32)
    mq_gain = gmq_ref[...] * (SCALE * LOG2E)
    lgs = []
    for hm in range(MEM_HEADS):
        qh = qm[hm * HEAD_DIM:(hm + 1) * HEAD_DIM]
        qn = qh * (_head_rms_scale(qh) * mq_gain)
        qpad = jnp.concatenate([qn, zero_pad], axis=0).astype(BF16)
        lgs.append(_dot(km_ref[0, hm], qpad))

    qT = proj_t(offs["q"], fox_w)

    logf = _log_sigmoid(z)
    hi = logf.astype(BF16)
    r1 = logf - hi.astype(F32)
    mid = r1.astype(BF16)
    lo = (r1 - mid.astype(F32)).astype(BF16)
    cs = _dot(jnp.concatenate([hi, mid, lo], axis=0), tri_ref[...])

    kT = proj_t(offs["k"], fox_w)
    vT = proj_t(offs["v"], fox_w)

    for hm in range(MEM_HEADS):
        lg = lgs[hm]
        p = jnp.exp2(lg - jnp.max(lg, axis=0, keepdims=True)).astype(BF16)
        yv = _dot(vm_ref[0, hm], p)
        y = yv[0:HEAD_DIM] / yv[HEAD_DIM:HEAD_DIM + 1]
        mm = y * _silu(gm[hm * HEAD_DIM:(hm + 1) * HEAD_DIM])
        mam_ref[0, pool_w + hm * HEAD_DIM:pool_w + (hm + 1) * HEAD_DIM, :] = mm.astype(BF16)

    uext = jnp.concatenate([halo_ref[...], u], axis=1)
    halo_ref[...] = u[:, ts - LANES:]
    pos1 = (s_idx * ts + lax.broadcasted_iota(jnp.int32, (1, ts), 1) + 1).astype(F32)
    acc = uext
    pooled = []
    shift = 1
    for g, w in enumerate(POOL_WINDOWS):
        while shift < w:
            acc = acc + pltpu.roll(acc, shift, 1)
            shift *= 2
        pooled.append(acc[0:gd, LANES:] / jnp.minimum(pos1, float(w)))
        acc = acc[gd:]
    d = (jnp.concatenate(pooled, axis=0) - u).astype(BF16)
    ya = _dot(wp_ref[...], d) * ps_ref[...]
    mam_ref[0, 0:pool_w, :] = (ya * _silu(ga)).astype(BF16)

    gbT_ref[0] = proj_t(offs["gb"], fox_w)

    fcum = (cs[0:BF16_ROWS] + cs[BF16_ROWS:2 * BF16_ROWS] + cs[2 * BF16_ROWS:3 * BF16_ROWS]
            + fcarry_ref[:, LANES - 1:LANES])
    fcarry_ref[...] = fcum[:, ts - LANES:]
    f2 = fcum * LOG2E

    arow = lax.broadcasted_iota(jnp.int32, (AUG_ROWS, ts), 0)

    def split3(v):
        v_hi = v.astype(BF16).astype(F32)
        r = v - v_hi
        v_mid = r.astype(BF16).astype(F32)
        return v_hi, v_mid, (r - v_mid).astype(BF16).astype(F32)

    def rows(*vals):
        out = jnp.zeros((AUG_ROWS, ts), F32)
        for r, v in enumerate(vals):
            out = jnp.where(arow == r, v, out)
        return out

    zero_rows = jnp.zeros((QK_ROWS - HEAD_DIM - AUG_ROWS, ts), F32)
    q_gain = gq_ref[...] * (SCALE * LOG2E)
    for hd in range(n_fox):
        qh = qT[hd * HEAD_DIM:(hd + 1) * HEAD_DIM]
        qn = qh * (_head_rms_scale(qh) * q_gain)
        f3 = split3(f2[hd:hd + 1])
        qT_ref[0, hd, 0:HEAD_DIM, :] = qn.astype(BF16)
        qT_ref[0, hd, HEAD_DIM:HEAD_DIM + AUG_ROWS, :] = rows(
            f3[0], f3[1], f3[2], 1.0, 1.0, 1.0, -mb_ref[0]).astype(BF16)
        qT_ref[0, hd, HEAD_DIM + AUG_ROWS:, :] = zero_rows.astype(BF16)

        kh = kT[hd * HEAD_DIM:(hd + 1) * HEAD_DIM]
        kn = kh * (_head_rms_scale(kh) * gk_ref[...])
        k_aug = rows(1.0, 1.0, 1.0, -f3[0], -f3[1], -f3[2], 1.0)
        kfull = jnp.concatenate([kn, k_aug, zero_rows], axis=0)
        kp_ref[0, hd] = kfull.T.astype(BF16)

    ones_blk = _ones_row_block(ts)
    for hd in range(n_fox):
        va = jnp.concatenate([vT[hd * HEAD_DIM:(hd + 1) * HEAD_DIM], ones_blk], axis=0).astype(BF16)
        for c in range(ts // tk):
            vT_ref[0, hd, c] = va[:, c * tk:(c + 1) * tk]


def _fox_out_kernel(fast_ref, qT_ref, kp_ref, vT_ref, gbT_ref, mam_ref, x_ref, woT_ref, o_ref, acc_ref, *,
                    pool_w):
    hg = qT_ref.shape[1]
    tq = qT_ref.shape[3]
    tk = vT_ref.shape[4]
    n_diag = tq // tk
    i = pl.program_id(1)

    acc_ref[...] = jnp.zeros_like(acc_ref)
    causal = (lax.broadcasted_iota(jnp.int32, (tk, tq), 0)
              <= lax.broadcasted_iota(jnp.int32, (tk, tq), 1))

    def head_scores(hd, j, lo=0):
        start = pl.multiple_of(j * tk, tk)
        return _dot(kp_ref[0, hd, pl.ds(start, tk), :], qT_ref[0, hd, :, lo:])

    def mask(s, diag, lo):
        return s if diag is None else jnp.where(causal[:, :tq - lo], s, MASK_VALUE)

    def fast_tiles(tiles):
        units = [(j, diag, hd) for (j, diag) in tiles for hd in range(hg)]
        scores = {}

        def issue(u):
            j, diag, hd = units[u]
            scores[u] = head_scores(hd, j, 0 if diag is None else diag * tk)

        for u in range(min(FAST_LOOKAHEAD, len(units))):
            issue(u)
        for u, (j, diag, hd) in enumerate(units):
            lo = 0 if diag is None else diag * tk
            p = jnp.exp2(mask(scores.pop(u), diag, lo)).astype(BF16)
            acc_ref[hd, :, lo:] += _dot(vT_ref[0, hd, j], p)
            if u + FAST_LOOKAHEAD < len(units):
                issue(u + FAST_LOOKAHEAD)

    def safe_step(j, ms, diag=None):
        lo = 0 if diag is None else diag * tk
        scores = [head_scores(hd, j, lo) for hd in range(hg)]
        out = []
        for hd in range(hg):
            s = mask(scores[hd], diag, lo)
            m_old = ms[hd][:, lo:]
            m_new = jnp.maximum(m_old, jnp.max(s, axis=0, keepdims=True))
            p = jnp.exp2(s - m_new).astype(BF16)
            alpha = jnp.exp2(m_old - m_new)
            acc_ref[hd, :, lo:] = alpha * acc_ref[hd, :, lo:] + _dot(vT_ref[0, hd, j], p)
            out.append(m_new if lo == 0 else jnp.concatenate([ms[hd][:, :lo], m_new], axis=1))
        return tuple(out)

    def finish():
        mam = mam_ref[0]
        fox_w = hg * HEAD_DIM
        part = (_dot(woT_ref[:, 0:pool_w], mam[0:pool_w])
                + _dot(woT_ref[:, pool_w + fox_w:], mam[pool_w:]))
        ys = []
        for hd in range(hg):
            acc = acc_ref[hd]
            ys.append(acc[0:HEAD_DIM] / acc[HEAD_DIM:HEAD_DIM + 1])
        yb = (jnp.concatenate(ys, axis=0) * _silu(gbT_ref[0])).astype(BF16)
        outT = part + _dot(woT_ref[:, pool_w:pool_w + fox_w], yb)
        o_ref[0] = x_ref[0] + outT.T

    @pl.when(fast_ref[0] != 0)
    def _():
        def below_diagonal(first, count):
            fast_tiles([(first + d, None) for d in range(count)])

        def body(c, carry):
            below_diagonal(c * FAST_TILES_PER_TRIP, FAST_TILES_PER_TRIP)
            return carry

        n_below = i * n_diag
        n_trips = n_below // FAST_TILES_PER_TRIP
        lax.fori_loop(0, n_trips, body, 0)
        for rest in range(n_diag, FAST_TILES_PER_TRIP, n_diag):
            @pl.when(n_below - n_trips * FAST_TILES_PER_TRIP == rest)
            def _():
                below_diagonal(n_trips * FAST_TILES_PER_TRIP, rest)
        fast_tiles([(n_below + c, c) for c in range(n_diag)])
        finish()

    @pl.when(fast_ref[0] == 0)
    def _():
        ms = tuple(jnp.full((1, tq), MASK_VALUE, F32) for _ in range(hg))
        ms = lax.fori_loop(0, i * n_diag, lambda j, c: safe_step(j, c), ms)
        for c in range(n_diag):
            ms = safe_step(i * n_diag + c, ms, diag=c)
        finish()


def _col(v, rows=None):
    v = v.astype(F32)
    if rows is not None and rows > v.shape[0]:
        v = jnp.concatenate([v, jnp.zeros((rows - v.shape[0],), F32)])
    return v[:, None]


def _params(sem):
    return pltpu.CompilerParams(dimension_semantics=sem, vmem_limit_bytes=VMEM_LIMIT_BYTES)


def _layer(x, mem, norm_g, w_in, b_f, w_pool, pool_scale, fox_q_g, fox_k_g,
           mem_norm_g, w_mem_kv, mem_q_g, mem_k_g, w_out):
    bsz, seq, dm = x.shape
    n_mem = mem.shape[1]
    n_fox = b_f.shape[0]
    pool_w = pool_scale.shape[0]
    fox_w = n_fox * HEAD_DIM
    mem_w = MEM_HEADS * HEAD_DIM
    assert n_fox <= BF16_ROWS and pool_w == len(POOL_WINDOWS) * POOL_GROUP_DIM
    assert w_in.shape[1] == 2 * pool_w + 4 * fox_w + n_fox + 2 * mem_w
    assert w_out.shape[0] == pool_w + fox_w + mem_w
    ts, tq, tk = SEQ_TILE, Q_TILE, KV_TILE
    assert seq % ts == 0 and seq % tq == 0 and tq % tk == 0 and ts % tk == 0
    assert FAST_TILES_PER_TRIP % (tq // tk) == 0

    sizes = (pool_w, pool_w, fox_w, fox_w, fox_w, n_fox, fox_w, mem_w, mem_w)
    parts, off = [], 0
    for w in sizes:
        parts.append(w_in[:, off:off + w])
        off += w
    w_ua, w_ga, w_q, w_k, w_v, w_f, w_gb, w_qm, w_gm = parts
    w_f = jnp.concatenate([w_f, jnp.zeros((dm, BF16_ROWS - n_fox), w_in.dtype)], axis=1)
    order = [("ua", w_ua), ("ga", w_ga), ("f", w_f), ("q", w_q), ("k", w_k), ("v", w_v),
             ("gb", w_gb), ("qm", w_qm), ("gm", w_gm)]
    offs, off = {}, 0
    for name, w in order:
        offs[name] = off
        off += w.shape[1]
    wT = jnp.concatenate([w for _, w in order], axis=1).T.astype(BF16)
    n_rows = wT.shape[0]
    tri = (lax.broadcasted_iota(jnp.int32, (ts, ts), 0)
           <= lax.broadcasted_iota(jnp.int32, (ts, ts), 1)).astype(BF16)
    wp_bd = jax.scipy.linalg.block_diag(*[w_pool[g].T for g in range(len(POOL_WINDOWS))]).astype(BF16)
    wkvT = w_mem_kv.T.astype(BF16)
    woT = w_out.T.astype(BF16)

    km, vm = pl.pallas_call(
        _mem_kv_kernel,
        grid=(bsz,),
        in_specs=[
            pl.BlockSpec((1, n_mem, dm), lambda b: (b, 0, 0)),
            pl.BlockSpec((1, dm), lambda b: (0, 0)),
            pl.BlockSpec((2 * mem_w, dm), lambda b: (0, 0)),
            pl.BlockSpec((HEAD_DIM, 1), lambda b: (0, 0)),
        ],
        out_specs=[
            pl.BlockSpec((1, MEM_HEADS, n_mem, QK_ROWS), lambda b: (b, 0, 0, 0)),
            pl.BlockSpec((1, MEM_HEADS, V_ROWS, n_mem), lambda b: (b, 0, 0, 0)),
        ],
        out_shape=[
            jax.ShapeDtypeStruct((bsz, MEM_HEADS, n_mem, QK_ROWS), BF16),
            jax.ShapeDtypeStruct((bsz, MEM_HEADS, V_ROWS, n_mem), BF16),
        ],
        compiler_params=_params(("arbitrary",)),
        name="mem_kv",
    )(mem, mem_norm_g[None, :], wkvT, _col(mem_k_g))

    const2 = lambda b, s: (0, 0)
    m_bound = (HEAD_DIM * SCALE * LOG2E * BOUND_SLACK
               * jnp.max(jnp.abs(fox_q_g)) * jnp.max(jnp.abs(fox_k_g))).astype(F32).reshape(1)
    fast = (m_bound < FAST_MAX_SHIFT).astype(jnp.int32)
    qT, kp, vT, gbT, mam = pl.pallas_call(
        functools.partial(_proj_kernel, offs=offs, n_fox=n_fox, pool_w=pool_w,
                          fox_w=fox_w, mem_w=mem_w),
        grid=(bsz, seq // ts),
        in_specs=[
            pl.BlockSpec(memory_space=pltpu.SMEM),
            pl.BlockSpec((1, ts, dm), lambda b, s: (b, s, 0)),
            pl.BlockSpec((1, dm), const2),
            pl.BlockSpec((n_rows, dm), const2),
            pl.BlockSpec((BF16_ROWS, 1), const2),
            pl.BlockSpec((ts, ts), const2),
            pl.BlockSpec((HEAD_DIM, 1), const2),
            pl.BlockSpec((HEAD_DIM, 1), const2),
            pl.BlockSpec((pool_w, pool_w), const2),
            pl.BlockSpec((pool_w, 1), const2),
            pl.BlockSpec((1, MEM_HEADS, n_mem, QK_ROWS), lambda b, s: (b, 0, 0, 0)),
            pl.BlockSpec((1, MEM_HEADS, V_ROWS, n_mem), lambda b, s: (b, 0, 0, 0)),
            pl.BlockSpec((HEAD_DIM, 1), const2),
        ],
        out_specs=[
            pl.BlockSpec((1, n_fox, QK_ROWS, ts), lambda b, s: (b, 0, 0, s)),
            pl.BlockSpec((1, n_fox, ts, QK_ROWS), lambda b, s: (b, 0, s, 0)),
            pl.BlockSpec((1, n_fox, ts // tk, V_ROWS, tk), lambda b, s: (b, 0, s, 0, 0)),
            pl.BlockSpec((1, fox_w, ts), lambda b, s: (b, 0, s)),
            pl.BlockSpec((1, pool_w + mem_w, ts), lambda b, s: (b, 0, s)),
        ],
        out_shape=[
            jax.ShapeDtypeStruct((bsz, n_fox, QK_ROWS, seq), BF16),
            jax.ShapeDtypeStruct((bsz, n_fox, seq, QK_ROWS), BF16),
            jax.ShapeDtypeStruct((bsz, n_fox, seq // tk, V_ROWS, tk), BF16),
            jax.ShapeDtypeStruct((bsz, fox_w, seq), F32),
            jax.ShapeDtypeStruct((bsz, pool_w + mem_w, seq), BF16),
        ],
        scratch_shapes=[
            pltpu.VMEM((BF16_ROWS, LANES), F32),
            pltpu.VMEM((pool_w, LANES), F32),
        ],
        compiler_params=_params(("arbitrary", "arbitrary")),
        name="proj",
    )(m_bound, x, norm_g[None, :], wT, _col(b_f, BF16_ROWS), tri, _col(fox_q_g), _col(fox_k_g),
      wp_bd, _col(pool_scale), km, vm, _col(mem_q_g))

    return pl.pallas_call(
        functools.partial(_fox_out_kernel, pool_w=pool_w),
        grid=(bsz, seq // tq),
        in_specs=[
            pl.BlockSpec(memory_space=pltpu.SMEM),
            pl.BlockSpec((1, n_fox, QK_ROWS, tq), lambda b, i: (b, 0, 0, i)),
            pl.BlockSpec((1, n_fox, seq, QK_ROWS), lambda b, i: (b, 0, 0, 0)),
            pl.BlockSpec((1, n_fox, seq // tk, V_ROWS, tk), lambda b, i: (b, 0, 0, 0, 0)),
            pl.BlockSpec((1, fox_w, tq), lambda b, i: (b, 0, i)),
            pl.BlockSpec((1, pool_w + mem_w, tq), lambda b, i: (b, 0, i)),
            pl.BlockSpec((1, tq, dm), lambda b, i: (b, i, 0)),
            pl.BlockSpec((dm, pool_w + fox_w + mem_w), const2),
        ],
        out_specs=pl.BlockSpec((1, tq, dm), lambda b, i: (b, i, 0)),
        out_shape=jax.ShapeDtypeStruct((bsz, seq, dm), x.dtype),
        scratch_shapes=[pltpu.VMEM((n_fox, V_ROWS, tq), F32)],
        compiler_params=_params(("arbitrary", "arbitrary")),
        name="fox_out",
    )(fast, qT, kp, vT, gbT, mam, x, woT)


def kernel(x, mem, norm_g, w_in, b_f, w_pool, pool_scale, fox_q_g, fox_k_g, mem_norm_g,
           w_mem_kv, mem_q_g, mem_k_g, w_out):
    for l in range(norm_g.shape[0]):
        x = _layer(x, mem, norm_g[l], w_in[l], b_f[l], w_pool[l], pool_scale[l], fox_q_g[l],
                   fox_k_g[l], mem_norm_g[l], w_mem_kv[l], mem_q_g[l], mem_k_g[l], w_out[l])
    return x
```

```python
import functools
import math

import jax
import jax.numpy as jnp
from jax import lax
from jax.experimental import pallas as pl
from jax.experimental.pallas import tpu as pltpu

F32 = jnp.float32
BF16 = jnp.bfloat16

HEAD_DIM = 64
EPS = 1e-6
POOL_WINDOWS = (2, 4, 8, 16)
POOL_GROUP_DIM = 64
MEM_HEADS = 4
SCALE = 1.0 / math.sqrt(HEAD_DIM)
LOG2E = math.log2(math.e)
MASK_VALUE = -1e30

LANES = 128
SUBLANES = 8
BF16_ROWS = 16
QK_ROWS = 128
AUG_ROWS = BF16_ROWS
MEM_V_ROWS = HEAD_DIM + BF16_ROWS

SEQ_TILE = 512
Q_TILE = 512
KV_TILE = 256
FAST_TILES_PER_TRIP = 4
FAST_LOOKAHEAD = 3
FAST_MAX_SHIFT = 50.0
BOUND_SLACK = 1.01
VMEM_LIMIT_BYTES = 56 * 1024 * 1024

NT_DIMS = (((1,), (1,)), ((), ()))


def _dot(a, b):
    return jnp.dot(a, b, preferred_element_type=F32)


def _dot_nt(a, b):
    return lax.dot_general(a, b, NT_DIMS, preferred_element_type=F32)


def _silu(g):
    return g * jax.nn.sigmoid(g)


def _log_sigmoid(z):
    return jnp.minimum(z, 0.0) - jnp.log1p(jnp.exp(-jnp.abs(z)))


def _head_rms_scale(t):
    return lax.rsqrt(jnp.mean(t * t, axis=0, keepdims=True) + EPS)


def _ones_row_block(n):
    row = lax.broadcasted_iota(jnp.int32, (BF16_ROWS, n), 0)
    return (row == 0).astype(F32)


def _mem_kv_kernel(mem_ref, g_ref, wkvT_ref, gk_ref, km_ref, vm_ref):
    x = mem_ref[0]
    n_mem = x.shape[0]
    h = (x * lax.rsqrt(jnp.mean(x * x, axis=-1, keepdims=True) + EPS) * g_ref[...]).astype(BF16)
    kvT = _dot_nt(wkvT_ref[...], h)
    width = MEM_HEADS * HEAD_DIM
    ones_blk = _ones_row_block(n_mem)
    zero_pad = jnp.zeros((QK_ROWS - HEAD_DIM, n_mem), F32)
    for hm in range(MEM_HEADS):
        kh = kvT[hm * HEAD_DIM:(hm + 1) * HEAD_DIM]
        kn = kh * _head_rms_scale(kh) * gk_ref[...]
        km_ref[0, hm] = jnp.concatenate([kn, zero_pad], axis=0).T.astype(BF16)
        vh = kvT[width + hm * HEAD_DIM:width + (hm + 1) * HEAD_DIM]
        vm_ref[0, hm] = jnp.concatenate([vh, ones_blk], axis=0).astype(BF16)


def _proj_kernel(mb_ref, x_ref, ng_ref, wT_ref, bf_ref, tri_ref, gq_ref, gk_ref, wp_ref, ps_ref,
                 km_ref, vm_ref, gmq_ref,
                 qT_ref, kp_ref, vT_ref, gbT_ref, mam_ref,
                 fcarry_ref, halo_ref, *, offs, n_fox, pool_w, fox_w, mem_w):
    ts = x_ref.shape[1]
    s_idx = pl.program_id(1)

    @pl.when(s_idx == 0)
    def _():
        fcarry_ref[...] = jnp.zeros_like(fcarry_ref)
        halo_ref[...] = jnp.zeros_like(halo_ref)

    x = x_ref[0]
    h = (x * lax.rsqrt(jnp.mean(x * x, axis=-1, keepdims=True) + EPS) * ng_ref[...]).astype(BF16)

    def proj_t(lo, n):
        return _dot_nt(wT_ref[lo:lo + n, :], h)

    tk = vT_ref.shape[4]
    gd = POOL_GROUP_DIM

    sec_m = proj_t(offs["qm"], 2 * mem_w)
    qm = sec_m[0:mem_w]
    gm = sec_m[mem_w:2 * mem_w]
    sec_a = proj_t(offs["ua"], 2 * pool_w + BF16_ROWS)
    u = sec_a[0:pool_w]
    ga = sec_a[pool_w:2 * pool_w]
    z = sec_a[2 * pool_w:2 * pool_w + BF16_ROWS] + bf_ref[...]

    zero_pad = jnp.zeros((QK_ROWS - HEAD_DIM, ts), F32)
    mq_gain = gmq_ref[...] * (SCALE * LOG2E)
    lgs = []
    for hm in range(MEM_HEADS):
        qh = qm[hm * HEAD_DIM:(hm + 1) * HEAD_DIM]
        qn = qh * (_head_rms_scale(qh) * mq_gain)
        qpad = jnp.concatenate([qn, zero_pad], axis=0).astype(BF16)
        lgs.append(_dot(km_ref[0, hm], qpad))

    qT = proj_t(offs["q"], fox_w)

    logf = _log_sigmoid(z)
    hi = logf.astype(BF16)
    r1 = logf - hi.astype(F32)
    mid = r1.astype(BF16)
    lo = (r1 - mid.astype(F32)).astype(BF16)
    cs = _dot(jnp.concatenate([hi, mid, lo], axis=0), tri_ref[...])

    kT = proj_t(offs["k"], fox_w)
    vT = proj_t(offs["v"], fox_w)

    for hm in range(MEM_HEADS):
        lg = lgs[hm]
        p = jnp.exp2(lg - jnp.max(lg, axis=0, keepdims=True)).astype(BF16)
        yv = _dot(vm_ref[0, hm], p)
        y = yv[0:HEAD_DIM] / yv[HEAD_DIM:HEAD_DIM + 1]
        mm = y * _silu(gm[hm * HEAD_DIM:(hm + 1) * HEAD_DIM])
        mam_ref[0, pool_w + hm * HEAD_DIM:pool_w + (hm + 1) * HEAD_DIM, :] = mm.astype(BF16)

    uext = jnp.concatenate([halo_ref[...], u], axis=1)
    halo_ref[...] = u[:, ts - LANES:]
    pos1 = (s_idx * ts + lax.broadcasted_iota(jnp.int32, (1, ts), 1) + 1).astype(F32)
    acc = uext
    pooled = []
    shift = 1
    for g, w in enumerate(POOL_WINDOWS):
        while shift < w:
            acc = acc + pltpu.roll(acc, shift, 1)
            shift *= 2
        pooled.append(acc[0:gd, LANES:] / jnp.minimum(pos1, float(w)))
        acc = acc[gd:]
    d = (jnp.concatenate(pooled, axis=0) - u).astype(BF16)
    ya = _dot(wp_ref[...], d) * ps_ref[...]
    mam_ref[0, 0:pool_w, :] = (ya * _silu(ga)).astype(BF16)

    gbT_ref[0] = proj_t(offs["gb"], fox_w)

    fcum = (cs[0:BF16_ROWS] + cs[BF16_ROWS:2 * BF16_ROWS] + cs[2 * BF16_ROWS:3 * BF16_ROWS]
            + fcarry_ref[:, LANES - 1:LANES])
    fcarry_ref[...] = fcum[:, ts - LANES:]
    f2 = fcum * LOG2E

    arow = lax.broadcasted_iota(jnp.int32, (AUG_ROWS, ts), 0)

    def split3(v):
        v_hi = v.astype(BF16).astype(F32)
        r = v - v_hi
        v_mid = r.astype(BF16).astype(F32)
        return v_hi, v_mid, (r - v_mid).astype(BF16).astype(F32)

    def rows(*vals):
        out = jnp.zeros((AUG_ROWS, ts), F32)
        for r, v in enumerate(vals):
            out = jnp.where(arow == r, v, out)
        return out

    zero_rows = jnp.zeros((QK_ROWS - HEAD_DIM - AUG_ROWS, ts), F32)
    q_gain = gq_ref[...] * (SCALE * LOG2E)
    for hd in range(n_fox):
        qh = qT[hd * HEAD_DIM:(hd + 1) * HEAD_DIM]
        qn = qh * (_head_rms_scale(qh) * q_gain)
        f3 = split3(f2[hd:hd + 1])
        qT_ref[0, hd, 0:HEAD_DIM, :] = qn.astype(BF16)
        qT_ref[0, hd, HEAD_DIM:HEAD_DIM + AUG_ROWS, :] = rows(
            f3[0], f3[1], f3[2], 1.0, 1.0, 1.0, -mb_ref[0]).astype(BF16)
        qT_ref[0, hd, HEAD_DIM + AUG_ROWS:, :] = zero_rows.astype(BF16)

        kh = kT[hd * HEAD_DIM:(hd + 1) * HEAD_DIM]
        kn = kh * (_head_rms_scale(kh) * gk_ref[...])
        k_aug = rows(1.0, 1.0, 1.0, -f3[0], -f3[1], -f3[2], 1.0)
        kfull = jnp.concatenate([kn, k_aug, zero_rows], axis=0)
        kp_ref[0, hd] = kfull.T.astype(BF16)

    for hd in range(n_fox):
        va = vT[hd * HEAD_DIM:(hd + 1) * HEAD_DIM].astype(BF16)
        for c in range(ts // tk):
            vT_ref[0, hd, c] = va[:, c * tk:(c + 1) * tk]


def _fox_out_kernel(fast_ref, qT_ref, kp_ref, vT_ref, gbT_ref, mam_ref, x_ref, woT_ref, o_ref,
                    acc_ref, den_ref, *, pool_w):
    hg = qT_ref.shape[1]
    tq = qT_ref.shape[3]
    tk = vT_ref.shape[4]
    n_diag = tq // tk
    i = pl.program_id(1)

    acc_ref[...] = jnp.zeros_like(acc_ref)
    den_ref[...] = jnp.zeros_like(den_ref)
    causal = (lax.broadcasted_iota(jnp.int32, (tk, tq), 0)
              <= lax.broadcasted_iota(jnp.int32, (tk, tq), 1))

    def head_scores(hd, j, lo=0):
        start = pl.multiple_of(j * tk, tk)
        return _dot(kp_ref[0, hd, pl.ds(start, tk), :], qT_ref[0, hd, :, lo:])

    def mask(s, diag, lo):
        return s if diag is None else jnp.where(causal[:, :tq - lo], s, MASK_VALUE)

    def fast_tiles(tiles):
        units = [(j, diag, hd) for (j, diag) in tiles for hd in range(hg)]
        scores = {}

        def issue(u):
            j, diag, hd = units[u]
            scores[u] = head_scores(hd, j, 0 if diag is None else diag * tk)

        for u in range(min(FAST_LOOKAHEAD, len(units))):
            issue(u)
        for u, (j, diag, hd) in enumerate(units):
            lo = 0 if diag is None else diag * tk
            p = jnp.exp2(mask(scores.pop(u), diag, lo))
            den_ref[hd, :, lo:] += jnp.sum(p, axis=0, keepdims=True)
            acc_ref[hd, :, lo:] += _dot(vT_ref[0, hd, j], p.astype(BF16))
            if u + FAST_LOOKAHEAD < len(units):
                issue(u + FAST_LOOKAHEAD)

    def safe_step(j, ms, diag=None):
        lo = 0 if diag is None else diag * tk
        scores = [head_scores(hd, j, lo) for hd in range(hg)]
        out = []
        for hd in range(hg):
            s = mask(scores[hd], diag, lo)
            m_old = ms[hd][:, lo:]
            m_new = jnp.maximum(m_old, jnp.max(s, axis=0, keepdims=True))
            p = jnp.exp2(s - m_new)
            alpha = jnp.exp2(m_old - m_new)
            den_ref[hd, :, lo:] = alpha * den_ref[hd, :, lo:] + jnp.sum(p, axis=0, keepdims=True)
            acc_ref[hd, :, lo:] = alpha * acc_ref[hd, :, lo:] + _dot(vT_ref[0, hd, j], p.astype(BF16))
            out.append(m_new if lo == 0 else jnp.concatenate([ms[hd][:, :lo], m_new], axis=1))
        return tuple(out)

    def finish():
        mam = mam_ref[0]
        fox_w = hg * HEAD_DIM
        part = (_dot(woT_ref[:, 0:pool_w], mam[0:pool_w])
                + _dot(woT_ref[:, pool_w + fox_w:], mam[pool_w:]))
        ys = []
        for hd in range(hg):
            ys.append(acc_ref[hd] / den_ref[hd, 0:1, :])
        yb = (jnp.concatenate(ys, axis=0) * _silu(gbT_ref[0])).astype(BF16)
        outT = part + _dot(woT_ref[:, pool_w:pool_w + fox_w], yb)
        o_ref[0] = x_ref[0] + outT.T

    @pl.when(fast_ref[0] != 0)
    def _():
        def below_diagonal(first, count):
            fast_tiles([(first + d, None) for d in range(count)])

        def body(c, carry):
            below_diagonal(c * FAST_TILES_PER_TRIP, FAST_TILES_PER_TRIP)
            return carry

        n_below = i * n_diag
        n_trips = n_below // FAST_TILES_PER_TRIP
        lax.fori_loop(0, n_trips, body, 0)
        for rest in range(n_diag, FAST_TILES_PER_TRIP, n_diag):
            @pl.when(n_below - n_trips * FAST_TILES_PER_TRIP == rest)
            def _():
                below_diagonal(n_trips * FAST_TILES_PER_TRIP, rest)
        fast_tiles([(n_below + c, c) for c in range(n_diag)])
        finish()

    @pl.when(fast_ref[0] == 0)
    def _():
        ms = tuple(jnp.full((1, tq), MASK_VALUE, F32) for _ in range(hg))
        ms = lax.fori_loop(0, i * n_diag, lambda j, c: safe_step(j, c), ms)
        for c in range(n_diag):
            ms = safe_step(i * n_diag + c, ms, diag=c)
        finish()


def _col(v, rows=None):
    v = v.astype(F32)
    if rows is not None and rows > v.shape[0]:
        v = jnp.concatenate([v, jnp.zeros((rows - v.shape[0],), F32)])
    return v[:, None]


def _params(sem):
    return pltpu.CompilerParams(dimension_semantics=sem, vmem_limit_bytes=VMEM_LIMIT_BYTES)


def _layer(x, mem, norm_g, w_in, b_f, w_pool, pool_scale, fox_q_g, fox_k_g,
           mem_norm_g, w_mem_kv, mem_q_g, mem_k_g, w_out):
    bsz, seq, dm = x.shape
    n_mem = mem.shape[1]
    n_fox = b_f.shape[0]
    pool_w = pool_scale.shape[0]
    fox_w = n_fox * HEAD_DIM
    mem_w = MEM_HEADS * HEAD_DIM
    assert n_fox <= BF16_ROWS and pool_w == len(POOL_WINDOWS) * POOL_GROUP_DIM
    assert w_in.shape[1] == 2 * pool_w + 4 * fox_w + n_fox + 2 * mem_w
    assert w_out.shape[0] == pool_w + fox_w + mem_w
    ts, tq, tk = SEQ_TILE, Q_TILE, KV_TILE
    assert seq % ts == 0 and seq % tq == 0 and tq % tk == 0 and ts % tk == 0
    assert FAST_TILES_PER_TRIP % (tq // tk) == 0

    sizes = (pool_w, pool_w, fox_w, fox_w, fox_w, n_fox, fox_w, mem_w, mem_w)
    parts, off = [], 0
    for w in sizes:
        parts.append(w_in[:, off:off + w])
        off += w
    w_ua, w_ga, w_q, w_k, w_v, w_f, w_gb, w_qm, w_gm = parts
    w_f = jnp.concatenate([w_f, jnp.zeros((dm, BF16_ROWS - n_fox), w_in.dtype)], axis=1)
    order = [("ua", w_ua), ("ga", w_ga), ("f", w_f), ("q", w_q), ("k", w_k), ("v", w_v),
             ("gb", w_gb), ("qm", w_qm), ("gm", w_gm)]
    offs, off = {}, 0
    for name, w in order:
        offs[name] = off
        off += w.shape[1]
    wT = jnp.concatenate([w for _, w in order], axis=1).T.astype(BF16)
    n_rows = wT.shape[0]
    tri = (lax.broadcasted_iota(jnp.int32, (ts, ts), 0)
           <= lax.broadcasted_iota(jnp.int32, (ts, ts), 1)).astype(BF16)
    wp_bd = jax.scipy.linalg.block_diag(*[w_pool[g].T for g in range(len(POOL_WINDOWS))]).astype(BF16)
    wkvT = w_mem_kv.T.astype(BF16)
    woT = w_out.T.astype(BF16)

    km, vm = pl.pallas_call(
        _mem_kv_kernel,
        grid=(bsz,),
        in_specs=[
            pl.BlockSpec((1, n_mem, dm), lambda b: (b, 0, 0)),
            pl.BlockSpec((1, dm), lambda b: (0, 0)),
            pl.BlockSpec((2 * mem_w, dm), lambda b: (0, 0)),
            pl.BlockSpec((HEAD_DIM, 1), lambda b: (0, 0)),
        ],
        out_specs=[
            pl.BlockSpec((1, MEM_HEADS, n_mem, QK_ROWS), lambda b: (b, 0, 0, 0)),
            pl.BlockSpec((1, MEM_HEADS, MEM_V_ROWS, n_mem), lambda b: (b, 0, 0, 0)),
        ],
        out_shape=[
            jax.ShapeDtypeStruct((bsz, MEM_HEADS, n_mem, QK_ROWS), BF16),
            jax.ShapeDtypeStruct((bsz, MEM_HEADS, MEM_V_ROWS, n_mem), BF16),
        ],
        compiler_params=_params(("arbitrary",)),
        name="mem_kv",
    )(mem, mem_norm_g[None, :], wkvT, _col(mem_k_g))

    const2 = lambda b, s: (0, 0)
    m_bound = (HEAD_DIM * SCALE * LOG2E * BOUND_SLACK
               * jnp.max(jnp.abs(fox_q_g)) * jnp.max(jnp.abs(fox_k_g))).astype(F32).reshape(1)
    fast = (m_bound < FAST_MAX_SHIFT).astype(jnp.int32)
    qT, kp, vT, gbT, mam = pl.pallas_call(
        functools.partial(_proj_kernel, offs=offs, n_fox=n_fox, pool_w=pool_w,
                          fox_w=fox_w, mem_w=mem_w),
        grid=(bsz, seq // ts),
        in_specs=[
            pl.BlockSpec(memory_space=pltpu.SMEM),
            pl.BlockSpec((1, ts, dm), lambda b, s: (b, s, 0)),
            pl.BlockSpec((1, dm), const2),
            pl.BlockSpec((n_rows, dm), const2),
            pl.BlockSpec((BF16_ROWS, 1), const2),
            pl.BlockSpec((ts, ts), const2),
            pl.BlockSpec((HEAD_DIM, 1), const2),
            pl.BlockSpec((HEAD_DIM, 1), const2),
            pl.BlockSpec((pool_w, pool_w), const2),
            pl.BlockSpec((pool_w, 1), const2),
            pl.BlockSpec((1, MEM_HEADS, n_mem, QK_ROWS), lambda b, s: (b, 0, 0, 0)),
            pl.BlockSpec((1, MEM_HEADS, MEM_V_ROWS, n_mem), lambda b, s: (b, 0, 0, 0)),
            pl.BlockSpec((HEAD_DIM, 1), const2),
        ],
        out_specs=[
            pl.BlockSpec((1, n_fox, QK_ROWS, ts), lambda b, s: (b, 0, 0, s)),
            pl.BlockSpec((1, n_fox, ts, QK_ROWS), lambda b, s: (b, 0, s, 0)),
            pl.BlockSpec((1, n_fox, ts // tk, HEAD_DIM, tk), lambda b, s: (b, 0, s, 0, 0)),
            pl.BlockSpec((1, fox_w, ts), lambda b, s: (b, 0, s)),
            pl.BlockSpec((1, pool_w + mem_w, ts), lambda b, s: (b, 0, s)),
        ],
        out_shape=[
            jax.ShapeDtypeStruct((bsz, n_fox, QK_ROWS, seq), BF16),
            jax.ShapeDtypeStruct((bsz, n_fox, seq, QK_ROWS), BF16),
            jax.ShapeDtypeStruct((bsz, n_fox, seq // tk, HEAD_DIM, tk), BF16),
            jax.ShapeDtypeStruct((bsz, fox_w, seq), F32),
            jax.ShapeDtypeStruct((bsz, pool_w + mem_w, seq), BF16),
        ],
        scratch_shapes=[
            pltpu.VMEM((BF16_ROWS, LANES), F32),
            pltpu.VMEM((pool_w, LANES), F32),
        ],
        compiler_params=_params(("arbitrary", "arbitrary")),
        name="proj",
    )(m_bound, x, norm_g[None, :], wT, _col(b_f, BF16_ROWS), tri, _col(fox_q_g), _col(fox_k_g),
      wp_bd, _col(pool_scale), km, vm, _col(mem_q_g))

    return pl.pallas_call(
        functools.partial(_fox_out_kernel, pool_w=pool_w),
        grid=(bsz, seq // tq),
        in_specs=[
            pl.BlockSpec(memory_space=pltpu.SMEM),
            pl.BlockSpec((1, n_fox, QK_ROWS, tq), lambda b, i: (b, 0, 0, i)),
            pl.BlockSpec((1, n_fox, seq, QK_ROWS), lambda b, i: (b, 0, 0, 0)),
            pl.BlockSpec((1, n_fox, seq // tk, HEAD_DIM, tk), lambda b, i: (b, 0, 0, 0, 0)),
            pl.BlockSpec((1, fox_w, tq), lambda b, i: (b, 0, i)),
            pl.BlockSpec((1, pool_w + mem_w, tq), lambda b, i: (b, 0, i)),
            pl.BlockSpec((1, tq, dm), lambda b, i: (b, i, 0)),
            pl.BlockSpec((dm, pool_w + fox_w + mem_w), const2),
        ],
        out_specs=pl.BlockSpec((1, tq, dm), lambda b, i: (b, i, 0)),
        out_shape=jax.ShapeDtypeStruct((bsz, seq, dm), x.dtype),
        scratch_shapes=[pltpu.VMEM((n_fox, HEAD_DIM, tq), F32), pltpu.VMEM((n_fox, SUBLANES, tq), F32)],
        compiler_params=_params(("arbitrary", "arbitrary")),
        name="fox_out",
    )(fast, qT, kp, vT, gbT, mam, x, woT)


def kernel(x, mem, norm_g, w_in, b_f, w_pool, pool_scale, fox_q_g, fox_k_g, mem_norm_g,
           w_mem_kv, mem_q_g, mem_k_g, w_out):
    for l in range(norm_g.shape[0]):
        x = _layer(x, mem, norm_g[l], w_in[l], b_f[l], w_pool[l], pool_scale[l], fox_q_g[l],
                   fox_k_g[l], mem_norm_g[l], w_mem_kv[l], mem_q_g[l], mem_k_g[l], w_out[l])
    return x
```

```python
import functools
import math

import jax
import jax.numpy as jnp
from jax import lax
from jax.experimental import pallas as pl
from jax.experimental.pallas import tpu as pltpu

F32 = jnp.float32
BF16 = jnp.bfloat16

HEAD_DIM = 64
EPS = 1e-6
POOL_WINDOWS = (2, 4, 8, 16)
POOL_GROUP_DIM = 64
MEM_HEADS = 4
SCALE = 1.0 / math.sqrt(HEAD_DIM)
LOG2E = math.log2(math.e)
MASK_VALUE = -1e30

LANES = 128
SUBLANES = 8
BF16_ROWS = 16
QK_ROWS = 128
AUG_ROWS = BF16_ROWS
MEM_V_ROWS = HEAD_DIM + BF16_ROWS

SEQ_TILE = 1024
Q_TILE = 512
KV_TILE = 256
FAST_TILES_PER_TRIP = 4
FAST_LOOKAHEAD = 3
FAST_MAX_SHIFT = 50.0
BOUND_SLACK = 1.01
VMEM_LIMIT_BYTES = 56 * 1024 * 1024

NT_DIMS = (((1,), (1,)), ((), ()))


def _dot(a, b):
    return jnp.dot(a, b, preferred_element_type=F32)


def _dot_nt(a, b):
    return lax.dot_general(a, b, NT_DIMS, preferred_element_type=F32)


def _silu(g):
    return g * jax.nn.sigmoid(g)


def _log_sigmoid(z):
    return jnp.minimum(z, 0.0) - jnp.log1p(jnp.exp(-jnp.abs(z)))


def _head_rms_scale(t):
    return lax.rsqrt(jnp.mean(t * t, axis=0, keepdims=True) + EPS)


def _ones_row_block(n):
    row = lax.broadcasted_iota(jnp.int32, (BF16_ROWS, n), 0)
    return (row == 0).astype(F32)


def _mem_kv_kernel(mem_ref, g_ref, wkvT_ref, gk_ref, km_ref, vm_ref):
    x = mem_ref[0]
    n_mem = x.shape[0]
    h = (x * lax.rsqrt(jnp.mean(x * x, axis=-1, keepdims=True) + EPS) * g_ref[...]).astype(BF16)
    kvT = _dot_nt(wkvT_ref[...], h)
    width = MEM_HEADS * HEAD_DIM
    ones_blk = _ones_row_block(n_mem)
    zero_pad = jnp.zeros((QK_ROWS - HEAD_DIM, n_mem), F32)
    for hm in range(MEM_HEADS):
        kh = kvT[hm * HEAD_DIM:(hm + 1) * HEAD_DIM]
        kn = kh * _head_rms_scale(kh) * gk_ref[...]
        km_ref[0, hm] = jnp.concatenate([kn, zero_pad], axis=0).T.astype(BF16)
        vh = kvT[width + hm * HEAD_DIM:width + (hm + 1) * HEAD_DIM]
        vm_ref[0, hm] = jnp.concatenate([vh, ones_blk], axis=0).astype(BF16)


def _proj_kernel(mb_ref, x_ref, ng_ref, wT_ref, bf_ref, tri_ref, gq_ref, gk_ref, wp_ref, ps_ref,
                 km_ref, vm_ref, gmq_ref,
                 qT_ref, kp_ref, vT_ref, gbT_ref, mam_ref,
                 fcarry_ref, halo_ref, *, offs, n_fox, pool_w, fox_w, mem_w):
    ts = x_ref.shape[1]
    s_idx = pl.program_id(1)

    @pl.when(s_idx == 0)
    def _():
        fcarry_ref[...] = jnp.zeros_like(fcarry_ref)
        halo_ref[...] = jnp.zeros_like(halo_ref)

    x = x_ref[0]
    h = (x * lax.rsqrt(jnp.mean(x * x, axis=-1, keepdims=True) + EPS) * ng_ref[...]).astype(BF16)

    def proj_t(lo, n):
        return _dot_nt(wT_ref[lo:lo + n, :], h)

    tk = vT_ref.shape[4]
    gd = POOL_GROUP_DIM

    sec_m = proj_t(offs["qm"], 2 * mem_w)
    qm = sec_m[0:mem_w]
    gm = sec_m[mem_w:2 * mem_w]
    sec_a = proj_t(offs["ua"], 2 * pool_w + BF16_ROWS)
    u = sec_a[0:pool_w]
    ga = sec_a[pool_w:2 * pool_w]
    z = sec_a[2 * pool_w:2 * pool_w + BF16_ROWS] + bf_ref[...]

    zero_pad = jnp.zeros((QK_ROWS - HEAD_DIM, ts), F32)
    mq_gain = gmq_ref[...] * (SCALE * LOG2E)
    lgs = []
    for hm in range(MEM_HEADS):
        qh = qm[hm * HEAD_DIM:(hm + 1) * HEAD_DIM]
        qn = qh * (_head_rms_scale(qh) * mq_gain)
        qpad = jnp.concatenate([qn, zero_pad], axis=0).astype(BF16)
        lgs.append(_dot(km_ref[0, hm], qpad))

    qT = proj_t(offs["q"], fox_w)

    logf = _log_sigmoid(z)
    hi = logf.astype(BF16)
    r1 = logf - hi.astype(F32)
    mid = r1.astype(BF16)
    lo = (r1 - mid.astype(F32)).astype(BF16)
    cs = _dot(jnp.concatenate([hi, mid, lo], axis=0), tri_ref[...])

    kT = proj_t(offs["k"], fox_w)
    vT = proj_t(offs["v"], fox_w)

    for hm in range(MEM_HEADS):
        lg = lgs[hm]
        p = jnp.exp2(lg - jnp.max(lg, axis=0, keepdims=True)).astype(BF16)
        yv = _dot(vm_ref[0, hm], p)
        y = yv[0:HEAD_DIM] / yv[HEAD_DIM:HEAD_DIM + 1]
        mm = y * _silu(gm[hm * HEAD_DIM:(hm + 1) * HEAD_DIM])
        mam_ref[0, pool_w + hm * HEAD_DIM:pool_w + (hm + 1) * HEAD_DIM, :] = mm.astype(BF16)

    uext = jnp.concatenate([halo_ref[...], u], axis=1)
    halo_ref[...] = u[:, ts - LANES:]
    pos1 = (s_idx * ts + lax.broadcasted_iota(jnp.int32, (1, ts), 1) + 1).astype(F32)
    acc = uext
    pooled = []
    shift = 1
    for g, w in enumerate(POOL_WINDOWS):
        while shift < w:
            acc = acc + pltpu.roll(acc, shift, 1)
            shift *= 2
        pooled.append(acc[0:gd, LANES:] / jnp.minimum(pos1, float(w)))
        acc = acc[gd:]
    d = (jnp.concatenate(pooled, axis=0) - u).astype(BF16)
    ya = _dot(wp_ref[...], d) * ps_ref[...]
    mam_ref[0, 0:pool_w, :] = (ya * _silu(ga)).astype(BF16)

    gbT_ref[0] = proj_t(offs["gb"], fox_w)

    fcum = (cs[0:BF16_ROWS] + cs[BF16_ROWS:2 * BF16_ROWS] + cs[2 * BF16_ROWS:3 * BF16_ROWS]
            + fcarry_ref[:, LANES - 1:LANES])
    fcarry_ref[...] = fcum[:, ts - LANES:]
    f2 = fcum * LOG2E

    arow = lax.broadcasted_iota(jnp.int32, (AUG_ROWS, ts), 0)

    def split3(v):
        v_hi = v.astype(BF16).astype(F32)
        r = v - v_hi
        v_mid = r.astype(BF16).astype(F32)
        return v_hi, v_mid, (r - v_mid).astype(BF16).astype(F32)

    def rows(*vals):
        out = jnp.zeros((AUG_ROWS, ts), F32)
        for r, v in enumerate(vals):
            out = jnp.where(arow == r, v, out)
        return out

    zero_rows = jnp.zeros((QK_ROWS - HEAD_DIM - AUG_ROWS, ts), F32)
    q_gain = gq_ref[...] * (SCALE * LOG2E)
    for hd in range(n_fox):
        qh = qT[hd * HEAD_DIM:(hd + 1) * HEAD_DIM]
        qn = qh * (_head_rms_scale(qh) * q_gain)
        f3 = split3(f2[hd:hd + 1])
        qT_ref[0, hd, 0:HEAD_DIM, :] = qn.astype(BF16)
        qT_ref[0, hd, HEAD_DIM:HEAD_DIM + AUG_ROWS, :] = rows(
            f3[0], f3[1], f3[2], 1.0, 1.0, 1.0, -mb_ref[0]).astype(BF16)
        qT_ref[0, hd, HEAD_DIM + AUG_ROWS:, :] = zero_rows.astype(BF16)

        kh = kT[hd * HEAD_DIM:(hd + 1) * HEAD_DIM]
        kn = kh * (_head_rms_scale(kh) * gk_ref[...])
        k_aug = rows(1.0, 1.0, 1.0, -f3[0], -f3[1], -f3[2], 1.0)
        kfull = jnp.concatenate([kn, k_aug, zero_rows], axis=0)
        kp_ref[0, hd] = kfull.T.astype(BF16)

    for hd in range(n_fox):
        va = vT[hd * HEAD_DIM:(hd + 1) * HEAD_DIM].astype(BF16)
        for c in range(ts // tk):
            vT_ref[0, hd, c] = va[:, c * tk:(c + 1) * tk]


def _fox_out_kernel(fast_ref, qT_ref, kp_ref, vT_ref, gbT_ref, mam_ref, x_ref, woT_ref, o_ref,
                    acc_ref, den_ref, *, pool_w):
    hg = qT_ref.shape[1]
    tq = qT_ref.shape[3]
    tk = vT_ref.shape[4]
    n_diag = tq // tk
    i = pl.program_id(1)

    acc_ref[...] = jnp.zeros_like(acc_ref)
    den_ref[...] = jnp.zeros_like(den_ref)
    causal = (lax.broadcasted_iota(jnp.int32, (tk, tq), 0)
              <= lax.broadcasted_iota(jnp.int32, (tk, tq), 1))

    def head_scores(hd, j, lo=0):
        start = pl.multiple_of(j * tk, tk)
        return _dot(kp_ref[0, hd, pl.ds(start, tk), :], qT_ref[0, hd, :, lo:])

    def mask(s, diag, lo):
        return s if diag is None else jnp.where(causal[:, :tq - lo], s, MASK_VALUE)

    def fast_tiles(tiles):
        units = [(j, diag, hd) for (j, diag) in tiles for hd in range(hg)]
        scores = {}

        def issue(u):
            j, diag, hd = units[u]
            scores[u] = head_scores(hd, j, 0 if diag is None else diag * tk)

        for u in range(min(FAST_LOOKAHEAD, len(units))):
            issue(u)
        for u, (j, diag, hd) in enumerate(units):
            lo = 0 if diag is None else diag * tk
            p = jnp.exp2(mask(scores.pop(u), diag, lo))
            den_ref[hd, :, lo:] += jnp.sum(p, axis=0, keepdims=True)
            acc_ref[hd, :, lo:] += _dot(vT_ref[0, hd, j], p.astype(BF16))
            if u + FAST_LOOKAHEAD < len(units):
                issue(u + FAST_LOOKAHEAD)

    def safe_step(j, ms, diag=None):
        lo = 0 if diag is None else diag * tk
        scores = [head_scores(hd, j, lo) for hd in range(hg)]
        out = []
        for hd in range(hg):
            s = mask(scores[hd], diag, lo)
            m_old = ms[hd][:, lo:]
            m_new = jnp.maximum(m_old, jnp.max(s, axis=0, keepdims=True))
            p = jnp.exp2(s - m_new)
            alpha = jnp.exp2(m_old - m_new)
            den_ref[hd, :, lo:] = alpha * den_ref[hd, :, lo:] + jnp.sum(p, axis=0, keepdims=True)
            acc_ref[hd, :, lo:] = alpha * acc_ref[hd, :, lo:] + _dot(vT_ref[0, hd, j], p.astype(BF16))
            out.append(m_new if lo == 0 else jnp.concatenate([ms[hd][:, :lo], m_new], axis=1))
        return tuple(out)

    def finish():
        mam = mam_ref[0]
        fox_w = hg * HEAD_DIM
        part = (_dot(woT_ref[:, 0:pool_w], mam[0:pool_w])
                + _dot(woT_ref[:, pool_w + fox_w:], mam[pool_w:]))
        ys = []
        for hd in range(hg):
            ys.append(acc_ref[hd] / den_ref[hd, 0:1, :])
        yb = (jnp.concatenate(ys, axis=0) * _silu(gbT_ref[0])).astype(BF16)
        outT = part + _dot(woT_ref[:, pool_w:pool_w + fox_w], yb)
        o_ref[0] = x_ref[0] + outT.T

    @pl.when(fast_ref[0] != 0)
    def _():
        def below_diagonal(first, count):
            fast_tiles([(first + d, None) for d in range(count)])

        def body(c, carry):
            below_diagonal(c * FAST_TILES_PER_TRIP, FAST_TILES_PER_TRIP)
            return carry

        n_below = i * n_diag
        n_trips = n_below // FAST_TILES_PER_TRIP
        lax.fori_loop(0, n_trips, body, 0)
        for rest in range(n_diag, FAST_TILES_PER_TRIP, n_diag):
            @pl.when(n_below - n_trips * FAST_TILES_PER_TRIP == rest)
            def _():
                below_diagonal(n_trips * FAST_TILES_PER_TRIP, rest)
        fast_tiles([(n_below + c, c) for c in range(n_diag)])
        finish()

    @pl.when(fast_ref[0] == 0)
    def _():
        ms = tuple(jnp.full((1, tq), MASK_VALUE, F32) for _ in range(hg))
        ms = lax.fori_loop(0, i * n_diag, lambda j, c: safe_step(j, c), ms)
        for c in range(n_diag):
            ms = safe_step(i * n_diag + c, ms, diag=c)
        finish()


def _col(v, rows=None):
    v = v.astype(F32)
    if rows is not None and rows > v.shape[0]:
        v = jnp.concatenate([v, jnp.zeros((rows - v.shape[0],), F32)])
    return v[:, None]


def _params(sem):
    return pltpu.CompilerParams(dimension_semantics=sem, vmem_limit_bytes=VMEM_LIMIT_BYTES)


def _layer(x, mem, norm_g, w_in, b_f, w_pool, pool_scale, fox_q_g, fox_k_g,
           mem_norm_g, w_mem_kv, mem_q_g, mem_k_g, w_out):
    bsz, seq, dm = x.shape
    n_mem = mem.shape[1]
    n_fox = b_f.shape[0]
    pool_w = pool_scale.shape[0]
    fox_w = n_fox * HEAD_DIM
    mem_w = MEM_HEADS * HEAD_DIM
    assert n_fox <= BF16_ROWS and pool_w == len(POOL_WINDOWS) * POOL_GROUP_DIM
    assert w_in.shape[1] == 2 * pool_w + 4 * fox_w + n_fox + 2 * mem_w
    assert w_out.shape[0] == pool_w + fox_w + mem_w
    ts, tq, tk = SEQ_TILE, Q_TILE, KV_TILE
    assert seq % ts == 0 and seq % tq == 0 and tq % tk == 0 and ts % tk == 0
    assert FAST_TILES_PER_TRIP % (tq // tk) == 0

    sizes = (pool_w, pool_w, fox_w, fox_w, fox_w, n_fox, fox_w, mem_w, mem_w)
    parts, off = [], 0
    for w in sizes:
        parts.append(w_in[:, off:off + w])
        off += w
    w_ua, w_ga, w_q, w_k, w_v, w_f, w_gb, w_qm, w_gm = parts
    w_f = jnp.concatenate([w_f, jnp.zeros((dm, BF16_ROWS - n_fox), w_in.dtype)], axis=1)
    order = [("ua", w_ua), ("ga", w_ga), ("f", w_f), ("q", w_q), ("k", w_k), ("v", w_v),
             ("gb", w_gb), ("qm", w_qm), ("gm", w_gm)]
    offs, off = {}, 0
    for name, w in order:
        offs[name] = off
        off += w.shape[1]
    wT = jnp.concatenate([w for _, w in order], axis=1).T.astype(BF16)
    n_rows = wT.shape[0]
    tri = (lax.broadcasted_iota(jnp.int32, (ts, ts), 0)
           <= lax.broadcasted_iota(jnp.int32, (ts, ts), 1)).astype(BF16)
    wp_bd = jax.scipy.linalg.block_diag(*[w_pool[g].T for g in range(len(POOL_WINDOWS))]).astype(BF16)
    wkvT = w_mem_kv.T.astype(BF16)
    woT = w_out.T.astype(BF16)

    km, vm = pl.pallas_call(
        _mem_kv_kernel,
        grid=(bsz,),
        in_specs=[
            pl.BlockSpec((1, n_mem, dm), lambda b: (b, 0, 0)),
            pl.BlockSpec((1, dm), lambda b: (0, 0)),
            pl.BlockSpec((2 * mem_w, dm), lambda b: (0, 0)),
            pl.BlockSpec((HEAD_DIM, 1), lambda b: (0, 0)),
        ],
        out_specs=[
            pl.BlockSpec((1, MEM_HEADS, n_mem, QK_ROWS), lambda b: (b, 0, 0, 0)),
            pl.BlockSpec((1, MEM_HEADS, MEM_V_ROWS, n_mem), lambda b: (b, 0, 0, 0)),
        ],
        out_shape=[
            jax.ShapeDtypeStruct((bsz, MEM_HEADS, n_mem, QK_ROWS), BF16),
            jax.ShapeDtypeStruct((bsz, MEM_HEADS, MEM_V_ROWS, n_mem), BF16),
        ],
        compiler_params=_params(("arbitrary",)),
        name="mem_kv",
    )(mem, mem_norm_g[None, :], wkvT, _col(mem_k_g))

    const2 = lambda b, s: (0, 0)
    m_bound = (HEAD_DIM * SCALE * LOG2E * BOUND_SLACK
               * jnp.max(jnp.abs(fox_q_g)) * jnp.max(jnp.abs(fox_k_g))).astype(F32).reshape(1)
    fast = (m_bound < FAST_MAX_SHIFT).astype(jnp.int32)
    qT, kp, vT, gbT, mam = pl.pallas_call(
        functools.partial(_proj_kernel, offs=offs, n_fox=n_fox, pool_w=pool_w,
                          fox_w=fox_w, mem_w=mem_w),
        grid=(bsz, seq // ts),
        in_specs=[
            pl.BlockSpec(memory_space=pltpu.SMEM),
            pl.BlockSpec((1, ts, dm), lambda b, s: (b, s, 0)),
            pl.BlockSpec((1, dm), const2),
            pl.BlockSpec((n_rows, dm), const2),
            pl.BlockSpec((BF16_ROWS, 1), const2),
            pl.BlockSpec((ts, ts), const2),
            pl.BlockSpec((HEAD_DIM, 1), const2),
            pl.BlockSpec((HEAD_DIM, 1), const2),
            pl.BlockSpec((pool_w, pool_w), const2),
            pl.BlockSpec((pool_w, 1), const2),
            pl.BlockSpec((1, MEM_HEADS, n_mem, QK_ROWS), lambda b, s: (b, 0, 0, 0)),
            pl.BlockSpec((1, MEM_HEADS, MEM_V_ROWS, n_mem), lambda b, s: (b, 0, 0, 0)),
            pl.BlockSpec((HEAD_DIM, 1), const2),
        ],
        out_specs=[
            pl.BlockSpec((1, n_fox, QK_ROWS, ts), lambda b, s: (b, 0, 0, s)),
            pl.BlockSpec((1, n_fox, ts, QK_ROWS), lambda b, s: (b, 0, s, 0)),
            pl.BlockSpec((1, n_fox, ts // tk, HEAD_DIM, tk), lambda b, s: (b, 0, s, 0, 0)),
            pl.BlockSpec((1, fox_w, ts), lambda b, s: (b, 0, s)),
            pl.BlockSpec((1, pool_w + mem_w, ts), lambda b, s: (b, 0, s)),
        ],
        out_shape=[
            jax.ShapeDtypeStruct((bsz, n_fox, QK_ROWS, seq), BF16),
            jax.ShapeDtypeStruct((bsz, n_fox, seq, QK_ROWS), BF16),
            jax.ShapeDtypeStruct((bsz, n_fox, seq // tk, HEAD_DIM, tk), BF16),
            jax.ShapeDtypeStruct((bsz, fox_w, seq), F32),
            jax.ShapeDtypeStruct((bsz, pool_w + mem_w, seq), BF16),
        ],
        scratch_shapes=[
            pltpu.VMEM((BF16_ROWS, LANES), F32),
            pltpu.VMEM((pool_w, LANES), F32),
        ],
        compiler_params=_params(("arbitrary", "arbitrary")),
        name="proj",
    )(m_bound, x, norm_g[None, :], wT, _col(b_f, BF16_ROWS), tri, _col(fox_q_g), _col(fox_k_g),
      wp_bd, _col(pool_scale), km, vm, _col(mem_q_g))

    return pl.pallas_call(
        functools.partial(_fox_out_kernel, pool_w=pool_w),
        grid=(bsz, seq // tq),
        in_specs=[
            pl.BlockSpec(memory_space=pltpu.SMEM),
            pl.BlockSpec((1, n_fox, QK_ROWS, tq), lambda b, i: (b, 0, 0, i)),
            pl.BlockSpec((1, n_fox, seq, QK_ROWS), lambda b, i: (b, 0, 0, 0)),
            pl.BlockSpec((1, n_fox, seq // tk, HEAD_DIM, tk), lambda b, i: (b, 0, 0, 0, 0)),
            pl.BlockSpec((1, fox_w, tq), lambda b, i: (b, 0, i)),
            pl.BlockSpec((1, pool_w + mem_w, tq), lambda b, i: (b, 0, i)),
            pl.BlockSpec((1, tq, dm), lambda b, i: (b, i, 0)),
            pl.BlockSpec((dm, pool_w + fox_w + mem_w), const2),
        ],
        out_specs=pl.BlockSpec((1, tq, dm), lambda b, i: (b, i, 0)),
        out_shape=jax.ShapeDtypeStruct((bsz, seq, dm), x.dtype),
        scratch_shapes=[pltpu.VMEM((n_fox, HEAD_DIM, tq), F32), pltpu.VMEM((n_fox, SUBLANES, tq), F32)],
        compiler_params=_params(("arbitrary", "arbitrary")),
        name="fox_out",
    )(fast, qT, kp, vT, gbT, mam, x, woT)


def kernel(x, mem, norm_g, w_in, b_f, w_pool, pool_scale, fox_q_g, fox_k_g, mem_norm_g,
           w_mem_kv, mem_q_g, mem_k_g, w_out):
    for l in range(norm_g.shape[0]):
        x = _layer(x, mem, norm_g[l], w_in[l], b_f[l], w_pool[l], pool_scale[l], fox_q_g[l],
                   fox_k_g[l], mem_norm_g[l], w_mem_kv[l], mem_q_g[l], mem_k_g[l], w_out[l])
    return x
```

```python
import functools
import math

import jax
import jax.numpy as jnp
from jax import lax
from jax.experimental import pallas as pl
from jax.experimental.pallas import tpu as pltpu

F32 = jnp.float32
BF16 = jnp.bfloat16

HEAD_DIM = 64
EPS = 1e-6
POOL_WINDOWS = (2, 4, 8, 16)
POOL_GROUP_DIM = 64
MEM_HEADS = 4
SCALE = 1.0 / math.sqrt(HEAD_DIM)
LOG2E = math.log2(math.e)
MASK_VALUE = -1e30

LANES = 128
SUBLANES = 8
BF16_ROWS = 16
QK_ROWS = 128
AUG_ROWS = BF16_ROWS
MEM_V_ROWS = HEAD_DIM + BF16_ROWS

SEQ_TILE = 1024
Q_TILE = 512
KV_TILE = 256
FAST_TILES_PER_TRIP = 4
FAST_LOOKAHEAD = 3
FAST_MAX_SHIFT = 50.0
BOUND_SLACK = 1.01
VMEM_LIMIT_BYTES = 56 * 1024 * 1024

NT_DIMS = (((1,), (1,)), ((), ()))
TN_DIMS = (((0,), (0,)), ((), ()))


def _dot(a, b):
    return jnp.dot(a, b, preferred_element_type=F32)


def _dot_nt(a, b):
    return lax.dot_general(a, b, NT_DIMS, preferred_element_type=F32)


def _dot_tn(a, b):
    return lax.dot_general(a, b, TN_DIMS, preferred_element_type=F32)


def _silu(g):
    return g * jax.nn.sigmoid(g)


def _log_sigmoid(z):
    return jnp.minimum(z, 0.0) - jnp.log1p(jnp.exp(-jnp.abs(z)))


def _head_rms_scale(t):
    return lax.rsqrt(jnp.mean(t * t, axis=0, keepdims=True) + EPS)


def _ones_row_block(n):
    row = lax.broadcasted_iota(jnp.int32, (BF16_ROWS, n), 0)
    return (row == 0).astype(F32)


def _mem_kv_kernel(mem_ref, g_ref, wkvT_ref, gk_ref, km_ref, vm_ref):
    x = mem_ref[0]
    n_mem = x.shape[0]
    h = (x * lax.rsqrt(jnp.mean(x * x, axis=-1, keepdims=True) + EPS) * g_ref[...]).astype(BF16)
    kvT = _dot_nt(wkvT_ref[...], h)
    width = MEM_HEADS * HEAD_DIM
    ones_blk = _ones_row_block(n_mem)
    zero_pad = jnp.zeros((QK_ROWS - HEAD_DIM, n_mem), F32)
    for hm in range(MEM_HEADS):
        kh = kvT[hm * HEAD_DIM:(hm + 1) * HEAD_DIM]
        kn = kh * _head_rms_scale(kh) * gk_ref[...]
        km_ref[0, hm] = jnp.concatenate([kn, zero_pad], axis=0).T.astype(BF16)
        vh = kvT[width + hm * HEAD_DIM:width + (hm + 1) * HEAD_DIM]
        vm_ref[0, hm] = jnp.concatenate([vh, ones_blk], axis=0).astype(BF16)


def _proj_kernel(mb_ref, x_ref, ng_ref, wT_ref, bf_ref, tri_ref, gq_ref, gk_ref, wp_ref, ps_ref,
                 km_ref, vm_ref, gmq_ref,
                 qT_ref, kp_ref, vT_ref, gbT_ref, mam_ref,
                 fcarry_ref, halo_ref, *, offs, n_fox, pool_w, fox_w, mem_w):
    ts = x_ref.shape[1]
    s_idx = pl.program_id(1)

    @pl.when(s_idx == 0)
    def _():
        fcarry_ref[...] = jnp.zeros_like(fcarry_ref)
        halo_ref[...] = jnp.zeros_like(halo_ref)

    x = x_ref[0]
    h = (x * lax.rsqrt(jnp.mean(x * x, axis=-1, keepdims=True) + EPS) * ng_ref[...]).astype(BF16)

    def proj_t(lo, n):
        return _dot_nt(wT_ref[lo:lo + n, :], h)

    tk = vT_ref.shape[4]
    gd = POOL_GROUP_DIM

    sec_m = proj_t(offs["qm"], 2 * mem_w)
    qm = sec_m[0:mem_w]
    gm = sec_m[mem_w:2 * mem_w]
    sec_a = proj_t(offs["ua"], 2 * pool_w + BF16_ROWS)
    u = sec_a[0:pool_w]
    ga = sec_a[pool_w:2 * pool_w]
    z = sec_a[2 * pool_w:2 * pool_w + BF16_ROWS] + bf_ref[...]

    zero_pad = jnp.zeros((QK_ROWS - HEAD_DIM, ts), F32)
    mq_gain = gmq_ref[...] * (SCALE * LOG2E)
    lgs = []
    for hm in range(MEM_HEADS):
        qh = qm[hm * HEAD_DIM:(hm + 1) * HEAD_DIM]
        qn = qh * (_head_rms_scale(qh) * mq_gain)
        qpad = jnp.concatenate([qn, zero_pad], axis=0).astype(BF16)
        lgs.append(_dot(km_ref[0, hm], qpad))

    qT = proj_t(offs["q"], fox_w)

    logf = _log_sigmoid(z)
    hi = logf.astype(BF16)
    r1 = logf - hi.astype(F32)
    mid = r1.astype(BF16)
    lo = (r1 - mid.astype(F32)).astype(BF16)
    cs = _dot(jnp.concatenate([hi, mid, lo], axis=0), tri_ref[...])

    kT = proj_t(offs["k"], fox_w)
    vT = proj_t(offs["v"], fox_w)

    for hm in range(MEM_HEADS):
        lg = lgs[hm]
        p = jnp.exp2(lg - jnp.max(lg, axis=0, keepdims=True)).astype(BF16)
        yv = _dot(vm_ref[0, hm], p)
        y = yv[0:HEAD_DIM] / yv[HEAD_DIM:HEAD_DIM + 1]
        mm = y * _silu(gm[hm * HEAD_DIM:(hm + 1) * HEAD_DIM])
        mam_ref[0, pool_w + hm * HEAD_DIM:pool_w + (hm + 1) * HEAD_DIM, :] = mm.astype(BF16)

    uext = jnp.concatenate([halo_ref[...], u], axis=1)
    halo_ref[...] = u[:, ts - LANES:]
    pos1 = (s_idx * ts + lax.broadcasted_iota(jnp.int32, (1, ts), 1) + 1).astype(F32)
    acc = uext
    pooled = []
    shift = 1
    for g, w in enumerate(POOL_WINDOWS):
        while shift < w:
            acc = acc + pltpu.roll(acc, shift, 1)
            shift *= 2
        pooled.append(acc[0:gd, LANES:] / jnp.minimum(pos1, float(w)))
        acc = acc[gd:]
    d = (jnp.concatenate(pooled, axis=0) - u).astype(BF16)
    ya = _dot(wp_ref[...], d) * ps_ref[...]
    mam_ref[0, 0:pool_w, :] = (ya * _silu(ga)).astype(BF16)

    gbT_ref[0] = proj_t(offs["gb"], fox_w)

    fcum = (cs[0:BF16_ROWS] + cs[BF16_ROWS:2 * BF16_ROWS] + cs[2 * BF16_ROWS:3 * BF16_ROWS]
            + fcarry_ref[:, LANES - 1:LANES])
    fcarry_ref[...] = fcum[:, ts - LANES:]
    f2 = fcum * LOG2E

    arow = lax.broadcasted_iota(jnp.int32, (AUG_ROWS, ts), 0)

    def split3(v):
        v_hi = v.astype(BF16).astype(F32)
        r = v - v_hi
        v_mid = r.astype(BF16).astype(F32)
        return v_hi, v_mid, (r - v_mid).astype(BF16).astype(F32)

    def rows(*vals):
        out = jnp.zeros((AUG_ROWS, ts), F32)
        for r, v in enumerate(vals):
            out = jnp.where(arow == r, v, out)
        return out

    zero_rows = jnp.zeros((QK_ROWS - HEAD_DIM - AUG_ROWS, ts), F32)
    q_gain = gq_ref[...] * (SCALE * LOG2E)
    for hd in range(n_fox):
        qh = qT[hd * HEAD_DIM:(hd + 1) * HEAD_DIM]
        qn = qh * (_head_rms_scale(qh) * q_gain)
        f3 = split3(f2[hd:hd + 1])
        qT_ref[0, hd, 0:HEAD_DIM, :] = qn.astype(BF16)
        qT_ref[0, hd, HEAD_DIM:HEAD_DIM + AUG_ROWS, :] = rows(
            f3[0], f3[1], f3[2], 1.0, 1.0, 1.0, -mb_ref[0]).astype(BF16)
        qT_ref[0, hd, HEAD_DIM + AUG_ROWS:, :] = zero_rows.astype(BF16)

        kh = kT[hd * HEAD_DIM:(hd + 1) * HEAD_DIM]
        kn = kh * (_head_rms_scale(kh) * gk_ref[...])
        k_aug = rows(1.0, 1.0, 1.0, -f3[0], -f3[1], -f3[2], 1.0)
        kfull = jnp.concatenate([kn, k_aug, zero_rows], axis=0)
        kp_ref[0, hd] = kfull.T.astype(BF16)

    for hd in range(n_fox):
        va = vT[hd * HEAD_DIM:(hd + 1) * HEAD_DIM].astype(BF16)
        for c in range(ts // tk):
            vT_ref[0, hd, c] = va[:, c * tk:(c + 1) * tk]


def _fox_out_kernel(fast_ref, qT_ref, kp_ref, vT_ref, gbT_ref, mam_ref, x_ref, wo_ref, o_ref,
                    acc_ref, den_ref, *, pool_w):
    hg = qT_ref.shape[1]
    tq = qT_ref.shape[3]
    tk = vT_ref.shape[4]
    n_diag = tq // tk
    i = pl.program_id(1)

    acc_ref[...] = jnp.zeros_like(acc_ref)
    den_ref[...] = jnp.zeros_like(den_ref)
    causal = (lax.broadcasted_iota(jnp.int32, (tk, tq), 0)
              <= lax.broadcasted_iota(jnp.int32, (tk, tq), 1))

    def head_scores(hd, j, lo=0):
        start = pl.multiple_of(j * tk, tk)
        return _dot(kp_ref[0, hd, pl.ds(start, tk), :], qT_ref[0, hd, :, lo:])

    def mask(s, diag, lo):
        return s if diag is None else jnp.where(causal[:, :tq - lo], s, MASK_VALUE)

    def fast_tiles(tiles):
        units = [(j, diag, hd) for (j, diag) in tiles for hd in range(hg)]
        scores = {}

        def issue(u):
            j, diag, hd = units[u]
            scores[u] = head_scores(hd, j, 0 if diag is None else diag * tk)

        for u in range(min(FAST_LOOKAHEAD, len(units))):
            issue(u)
        for u, (j, diag, hd) in enumerate(units):
            lo = 0 if diag is None else diag * tk
            p = jnp.exp2(mask(scores.pop(u), diag, lo))
            den_ref[hd, :, lo:] += jnp.sum(p, axis=0, keepdims=True)
            acc_ref[hd, :, lo:] += _dot(vT_ref[0, hd, j], p.astype(BF16))
            if u + FAST_LOOKAHEAD < len(units):
                issue(u + FAST_LOOKAHEAD)

    def safe_step(j, ms, diag=None):
        lo = 0 if diag is None else diag * tk
        scores = [head_scores(hd, j, lo) for hd in range(hg)]
        out = []
        for hd in range(hg):
            s = mask(scores[hd], diag, lo)
            m_old = ms[hd][:, lo:]
            m_new = jnp.maximum(m_old, jnp.max(s, axis=0, keepdims=True))
            p = jnp.exp2(s - m_new)
            alpha = jnp.exp2(m_old - m_new)
            den_ref[hd, :, lo:] = alpha * den_ref[hd, :, lo:] + jnp.sum(p, axis=0, keepdims=True)
            acc_ref[hd, :, lo:] = alpha * acc_ref[hd, :, lo:] + _dot(vT_ref[0, hd, j], p.astype(BF16))
            out.append(m_new if lo == 0 else jnp.concatenate([ms[hd][:, :lo], m_new], axis=1))
        return tuple(out)

    def finish():
        mam = mam_ref[0]
        fox_w = hg * HEAD_DIM
        part = (_dot_tn(mam[0:pool_w], wo_ref[0:pool_w, :])
                + _dot_tn(mam[pool_w:], wo_ref[pool_w + fox_w:, :]))
        ys = []
        for hd in range(hg):
            ys.append(acc_ref[hd] / den_ref[hd, 0:1, :])
        yb = (jnp.concatenate(ys, axis=0) * _silu(gbT_ref[0])).astype(BF16)
        o_ref[0] = x_ref[0] + part + _dot_tn(yb, wo_ref[pool_w:pool_w + fox_w, :])

    @pl.when(fast_ref[0] != 0)
    def _():
        def below_diagonal(first, count):
            fast_tiles([(first + d, None) for d in range(count)])

        def body(c, carry):
            below_diagonal(c * FAST_TILES_PER_TRIP, FAST_TILES_PER_TRIP)
            return carry

        n_below = i * n_diag
        n_trips = n_below // FAST_TILES_PER_TRIP
        lax.fori_loop(0, n_trips, body, 0)
        for rest in range(n_diag, FAST_TILES_PER_TRIP, n_diag):
            @pl.when(n_below - n_trips * FAST_TILES_PER_TRIP == rest)
            def _():
                below_diagonal(n_trips * FAST_TILES_PER_TRIP, rest)
        fast_tiles([(n_below + c, c) for c in range(n_diag)])
        finish()

    @pl.when(fast_ref[0] == 0)
    def _():
        ms = tuple(jnp.full((1, tq), MASK_VALUE, F32) for _ in range(hg))
        ms = lax.fori_loop(0, i * n_diag, lambda j, c: safe_step(j, c), ms)
        for c in range(n_diag):
            ms = safe_step(i * n_diag + c, ms, diag=c)
        finish()


def _col(v, rows=None):
    v = v.astype(F32)
    if rows is not None and rows > v.shape[0]:
        v = jnp.concatenate([v, jnp.zeros((rows - v.shape[0],), F32)])
    return v[:, None]


def _params(sem):
    return pltpu.CompilerParams(dimension_semantics=sem, vmem_limit_bytes=VMEM_LIMIT_BYTES)


def _layer(x, mem, norm_g, w_in, b_f, w_pool, pool_scale, fox_q_g, fox_k_g,
           mem_norm_g, w_mem_kv, mem_q_g, mem_k_g, w_out):
    bsz, seq, dm = x.shape
    n_mem = mem.shape[1]
    n_fox = b_f.shape[0]
    pool_w = pool_scale.shape[0]
    fox_w = n_fox * HEAD_DIM
    mem_w = MEM_HEADS * HEAD_DIM
    assert n_fox <= BF16_ROWS and pool_w == len(POOL_WINDOWS) * POOL_GROUP_DIM
    assert w_in.shape[1] == 2 * pool_w + 4 * fox_w + n_fox + 2 * mem_w
    assert w_out.shape[0] == pool_w + fox_w + mem_w
    ts, tq, tk = SEQ_TILE, Q_TILE, KV_TILE
    assert seq % ts == 0 and seq % tq == 0 and tq % tk == 0 and ts % tk == 0
    assert FAST_TILES_PER_TRIP % (tq // tk) == 0

    sizes = (pool_w, pool_w, fox_w, fox_w, fox_w, n_fox, fox_w, mem_w, mem_w)
    parts, off = [], 0
    for w in sizes:
        parts.append(w_in[:, off:off + w])
        off += w
    w_ua, w_ga, w_q, w_k, w_v, w_f, w_gb, w_qm, w_gm = parts
    w_f = jnp.concatenate([w_f, jnp.zeros((dm, BF16_ROWS - n_fox), w_in.dtype)], axis=1)
    order = [("ua", w_ua), ("ga", w_ga), ("f", w_f), ("q", w_q), ("k", w_k), ("v", w_v),
             ("gb", w_gb), ("qm", w_qm), ("gm", w_gm)]
    offs, off = {}, 0
    for name, w in order:
        offs[name] = off
        off += w.shape[1]
    wT = jnp.concatenate([w for _, w in order], axis=1).T.astype(BF16)
    n_rows = wT.shape[0]
    tri = (lax.broadcasted_iota(jnp.int32, (ts, ts), 0)
           <= lax.broadcasted_iota(jnp.int32, (ts, ts), 1)).astype(BF16)
    wp_bd = jax.scipy.linalg.block_diag(*[w_pool[g].T for g in range(len(POOL_WINDOWS))]).astype(BF16)
    wkvT = w_mem_kv.T.astype(BF16)
    wo = w_out.astype(BF16)

    km, vm = pl.pallas_call(
        _mem_kv_kernel,
        grid=(bsz,),
        in_specs=[
            pl.BlockSpec((1, n_mem, dm), lambda b: (b, 0, 0)),
            pl.BlockSpec((1, dm), lambda b: (0, 0)),
            pl.BlockSpec((2 * mem_w, dm), lambda b: (0, 0)),
            pl.BlockSpec((HEAD_DIM, 1), lambda b: (0, 0)),
        ],
        out_specs=[
            pl.BlockSpec((1, MEM_HEADS, n_mem, QK_ROWS), lambda b: (b, 0, 0, 0)),
            pl.BlockSpec((1, MEM_HEADS, MEM_V_ROWS, n_mem), lambda b: (b, 0, 0, 0)),
        ],
        out_shape=[
            jax.ShapeDtypeStruct((bsz, MEM_HEADS, n_mem, QK_ROWS), BF16),
            jax.ShapeDtypeStruct((bsz, MEM_HEADS, MEM_V_ROWS, n_mem), BF16),
        ],
        compiler_params=_params(("arbitrary",)),
        name="mem_kv",
    )(mem, mem_norm_g[None, :], wkvT, _col(mem_k_g))

    const2 = lambda b, s: (0, 0)
    m_bound = (HEAD_DIM * SCALE * LOG2E * BOUND_SLACK
               * jnp.max(jnp.abs(fox_q_g)) * jnp.max(jnp.abs(fox_k_g))).astype(F32).reshape(1)
    fast = (m_bound < FAST_MAX_SHIFT).astype(jnp.int32)
    qT, kp, vT, gbT, mam = pl.pallas_call(
        functools.partial(_proj_kernel, offs=offs, n_fox=n_fox, pool_w=pool_w,
                          fox_w=fox_w, mem_w=mem_w),
        grid=(bsz, seq // ts),
        in_specs=[
            pl.BlockSpec(memory_space=pltpu.SMEM),
            pl.BlockSpec((1, ts, dm), lambda b, s: (b, s, 0)),
            pl.BlockSpec((1, dm), const2),
            pl.BlockSpec((n_rows, dm), const2),
            pl.BlockSpec((BF16_ROWS, 1), const2),
            pl.BlockSpec((ts, ts), const2),
            pl.BlockSpec((HEAD_DIM, 1), const2),
            pl.BlockSpec((HEAD_DIM, 1), const2),
            pl.BlockSpec((pool_w, pool_w), const2),
            pl.BlockSpec((pool_w, 1), const2),
            pl.BlockSpec((1, MEM_HEADS, n_mem, QK_ROWS), lambda b, s: (b, 0, 0, 0)),
            pl.BlockSpec((1, MEM_HEADS, MEM_V_ROWS, n_mem), lambda b, s: (b, 0, 0, 0)),
            pl.BlockSpec((HEAD_DIM, 1), const2),
        ],
        out_specs=[
            pl.BlockSpec((1, n_fox, QK_ROWS, ts), lambda b, s: (b, 0, 0, s)),
            pl.BlockSpec((1, n_fox, ts, QK_ROWS), lambda b, s: (b, 0, s, 0)),
            pl.BlockSpec((1, n_fox, ts // tk, HEAD_DIM, tk), lambda b, s: (b, 0, s, 0, 0)),
            pl.BlockSpec((1, fox_w, ts), lambda b, s: (b, 0, s)),
            pl.BlockSpec((1, pool_w + mem_w, ts), lambda b, s: (b, 0, s)),
        ],
        out_shape=[
            jax.ShapeDtypeStruct((bsz, n_fox, QK_ROWS, seq), BF16),
            jax.ShapeDtypeStruct((bsz, n_fox, seq, QK_ROWS), BF16),
            jax.ShapeDtypeStruct((bsz, n_fox, seq // tk, HEAD_DIM, tk), BF16),
            jax.ShapeDtypeStruct((bsz, fox_w, seq), F32),
            jax.ShapeDtypeStruct((bsz, pool_w + mem_w, seq), BF16),
        ],
        scratch_shapes=[
            pltpu.VMEM((BF16_ROWS, LANES), F32),
            pltpu.VMEM((pool_w, LANES), F32),
        ],
        compiler_params=_params(("arbitrary", "arbitrary")),
        name="proj",
    )(m_bound, x, norm_g[None, :], wT, _col(b_f, BF16_ROWS), tri, _col(fox_q_g), _col(fox_k_g),
      wp_bd, _col(pool_scale), km, vm, _col(mem_q_g))

    return pl.pallas_call(
        functools.partial(_fox_out_kernel, pool_w=pool_w),
        grid=(bsz, seq // tq),
        in_specs=[
            pl.BlockSpec(memory_space=pltpu.SMEM),
            pl.BlockSpec((1, n_fox, QK_ROWS, tq), lambda b, i: (b, 0, 0, i)),
            pl.BlockSpec((1, n_fox, seq, QK_ROWS), lambda b, i: (b, 0, 0, 0)),
            pl.BlockSpec((1, n_fox, seq // tk, HEAD_DIM, tk), lambda b, i: (b, 0, 0, 0, 0)),
            pl.BlockSpec((1, fox_w, tq), lambda b, i: (b, 0, i)),
            pl.BlockSpec((1, pool_w + mem_w, tq), lambda b, i: (b, 0, i)),
            pl.BlockSpec((1, tq, dm), lambda b, i: (b, i, 0)),
            pl.BlockSpec((pool_w + fox_w + mem_w, dm), const2),
        ],
        out_specs=pl.BlockSpec((1, tq, dm), lambda b, i: (b, i, 0)),
        out_shape=jax.ShapeDtypeStruct((bsz, seq, dm), x.dtype),
        scratch_shapes=[pltpu.VMEM((n_fox, HEAD_DIM, tq), F32), pltpu.VMEM((n_fox, SUBLANES, tq), F32)],
        compiler_params=_params(("arbitrary", "arbitrary")),
        name="fox_out",
    )(fast, qT, kp, vT, gbT, mam, x, wo)


def kernel(x, mem, norm_g, w_in, b_f, w_pool, pool_scale, fox_q_g, fox_k_g, mem_norm_g,
           w_mem_kv, mem_q_g, mem_k_g, w_out):
    for l in range(norm_g.shape[0]):
        x = _layer(x, mem, norm_g[l], w_in[l], b_f[l], w_pool[l], pool_scale[l], fox_q_g[l],
                   fox_k_g[l], mem_norm_g[l], w_mem_kv[l], mem_q_g[l], mem_k_g[l], w_out[l])
    return x
```

```python
import functools
import math

import jax
import jax.numpy as jnp
from jax import lax
from jax.experimental import pallas as pl
from jax.experimental.pallas import tpu as pltpu

F32 = jnp.float32
BF16 = jnp.bfloat16

HEAD_DIM = 64
EPS = 1e-6
POOL_WINDOWS = (2, 4, 8, 16)
POOL_GROUP_DIM = 64
MEM_HEADS = 4
SCALE = 1.0 / math.sqrt(HEAD_DIM)
LOG2E = math.log2(math.e)
MASK_VALUE = -1e30

LANES = 128
SUBLANES = 8
BF16_ROWS = 16
QK_ROWS = 128
AUG_ROWS = BF16_ROWS
MEM_V_ROWS = HEAD_DIM + BF16_ROWS

SEQ_TILE = 1024
Q_TILE = 512
KV_TILE = 256
FAST_TILES_PER_TRIP = 8
FAST_LOOKAHEAD = 3
FAST_MAX_SHIFT = 50.0
BOUND_SLACK = 1.01
VMEM_LIMIT_BYTES = 56 * 1024 * 1024

NT_DIMS = (((1,), (1,)), ((), ()))
TN_DIMS = (((0,), (0,)), ((), ()))


def _dot(a, b):
    return jnp.dot(a, b, preferred_element_type=F32)


def _dot_nt(a, b):
    return lax.dot_general(a, b, NT_DIMS, preferred_element_type=F32)


def _dot_tn(a, b):
    return lax.dot_general(a, b, TN_DIMS, preferred_element_type=F32)


def _silu(g):
    return g * jax.nn.sigmoid(g)


def _log_sigmoid(z):
    return jnp.minimum(z, 0.0) - jnp.log1p(jnp.exp(-jnp.abs(z)))


def _head_rms_scale(t):
    return lax.rsqrt(jnp.mean(t * t, axis=0, keepdims=True) + EPS)


def _ones_row_block(n):
    row = lax.broadcasted_iota(jnp.int32, (BF16_ROWS, n), 0)
    return (row == 0).astype(F32)


def _mem_kv_kernel(mem_ref, g_ref, wkvT_ref, gk_ref, km_ref, vm_ref):
    x = mem_ref[0]
    n_mem = x.shape[0]
    h = (x * lax.rsqrt(jnp.mean(x * x, axis=-1, keepdims=True) + EPS) * g_ref[...]).astype(BF16)
    kvT = _dot_nt(wkvT_ref[...], h)
    width = MEM_HEADS * HEAD_DIM
    ones_blk = _ones_row_block(n_mem)
    zero_pad = jnp.zeros((QK_ROWS - HEAD_DIM, n_mem), F32)
    for hm in range(MEM_HEADS):
        kh = kvT[hm * HEAD_DIM:(hm + 1) * HEAD_DIM]
        kn = kh * _head_rms_scale(kh) * gk_ref[...]
        km_ref[0, hm] = jnp.concatenate([kn, zero_pad], axis=0).T.astype(BF16)
        vh = kvT[width + hm * HEAD_DIM:width + (hm + 1) * HEAD_DIM]
        vm_ref[0, hm] = jnp.concatenate([vh, ones_blk], axis=0).astype(BF16)


def _proj_kernel(mb_ref, x_ref, ng_ref, wT_ref, bf_ref, tri_ref, gq_ref, gk_ref, wp_ref, ps_ref,
                 km_ref, vm_ref, gmq_ref,
                 qT_ref, kp_ref, vT_ref, gbT_ref, mam_ref,
                 fcarry_ref, halo_ref, *, offs, n_fox, pool_w, fox_w, mem_w):
    ts = x_ref.shape[1]
    s_idx = pl.program_id(1)

    @pl.when(s_idx == 0)
    def _():
        fcarry_ref[...] = jnp.zeros_like(fcarry_ref)
        halo_ref[...] = jnp.zeros_like(halo_ref)

    x = x_ref[0]
    h = (x * lax.rsqrt(jnp.mean(x * x, axis=-1, keepdims=True) + EPS) * ng_ref[...]).astype(BF16)

    def proj_t(lo, n):
        return _dot_nt(wT_ref[lo:lo + n, :], h)

    tk = vT_ref.shape[4]
    gd = POOL_GROUP_DIM

    sec_m = proj_t(offs["qm"], 2 * mem_w)
    qm = sec_m[0:mem_w]
    gm = sec_m[mem_w:2 * mem_w]
    sec_a = proj_t(offs["ua"], 2 * pool_w + BF16_ROWS)
    u = sec_a[0:pool_w]
    ga = sec_a[pool_w:2 * pool_w]
    z = sec_a[2 * pool_w:2 * pool_w + BF16_ROWS] + bf_ref[...]

    zero_pad = jnp.zeros((QK_ROWS - HEAD_DIM, ts), F32)
    mq_gain = gmq_ref[...] * (SCALE * LOG2E)
    lgs = []
    for hm in range(MEM_HEADS):
        qh = qm[hm * HEAD_DIM:(hm + 1) * HEAD_DIM]
        qn = qh * (_head_rms_scale(qh) * mq_gain)
        qpad = jnp.concatenate([qn, zero_pad], axis=0).astype(BF16)
        lgs.append(_dot(km_ref[0, hm], qpad))

    qT = proj_t(offs["q"], fox_w)

    logf = _log_sigmoid(z)
    hi = logf.astype(BF16)
    r1 = logf - hi.astype(F32)
    mid = r1.astype(BF16)
    lo = (r1 - mid.astype(F32)).astype(BF16)
    cs = _dot(jnp.concatenate([hi, mid, lo], axis=0), tri_ref[...])

    kT = proj_t(offs["k"], fox_w)
    vT = proj_t(offs["v"], fox_w)

    for hm in range(MEM_HEADS):
        lg = lgs[hm]
        p = jnp.exp2(lg - jnp.max(lg, axis=0, keepdims=True)).astype(BF16)
        yv = _dot(vm_ref[0, hm], p)
        y = yv[0:HEAD_DIM] / yv[HEAD_DIM:HEAD_DIM + 1]
        mm = y * _silu(gm[hm * HEAD_DIM:(hm + 1) * HEAD_DIM])
        mam_ref[0, pool_w + hm * HEAD_DIM:pool_w + (hm + 1) * HEAD_DIM, :] = mm.astype(BF16)

    uext = jnp.concatenate([halo_ref[...], u], axis=1)
    halo_ref[...] = u[:, ts - LANES:]
    pos1 = (s_idx * ts + lax.broadcasted_iota(jnp.int32, (1, ts), 1) + 1).astype(F32)
    acc = uext
    pooled = []
    shift = 1
    for g, w in enumerate(POOL_WINDOWS):
        while shift < w:
            acc = acc + pltpu.roll(acc, shift, 1)
            shift *= 2
        pooled.append(acc[0:gd, LANES:] / jnp.minimum(pos1, float(w)))
        acc = acc[gd:]
    d = (jnp.concatenate(pooled, axis=0) - u).astype(BF16)
    ya = _dot(wp_ref[...], d) * ps_ref[...]
    mam_ref[0, 0:pool_w, :] = (ya * _silu(ga)).astype(BF16)

    gbT_ref[0] = proj_t(offs["gb"], fox_w)

    fcum = (cs[0:BF16_ROWS] + cs[BF16_ROWS:2 * BF16_ROWS] + cs[2 * BF16_ROWS:3 * BF16_ROWS]
            + fcarry_ref[:, LANES - 1:LANES])
    fcarry_ref[...] = fcum[:, ts - LANES:]
    f2 = fcum * LOG2E

    arow = lax.broadcasted_iota(jnp.int32, (AUG_ROWS, ts), 0)

    def split3(v):
        v_hi = v.astype(BF16).astype(F32)
        r = v - v_hi
        v_mid = r.astype(BF16).astype(F32)
        return v_hi, v_mid, (r - v_mid).astype(BF16).astype(F32)

    def rows(*vals):
        out = jnp.zeros((AUG_ROWS, ts), F32)
        for r, v in enumerate(vals):
            out = jnp.where(arow == r, v, out)
        return out

    zero_rows = jnp.zeros((QK_ROWS - HEAD_DIM - AUG_ROWS, ts), F32)
    q_gain = gq_ref[...] * (SCALE * LOG2E)
    for hd in range(n_fox):
        qh = qT[hd * HEAD_DIM:(hd + 1) * HEAD_DIM]
        qn = qh * (_head_rms_scale(qh) * q_gain)
        f3 = split3(f2[hd:hd + 1])
        qT_ref[0, hd, 0:HEAD_DIM, :] = qn.astype(BF16)
        qT_ref[0, hd, HEAD_DIM:HEAD_DIM + AUG_ROWS, :] = rows(
            f3[0], f3[1], f3[2], 1.0, 1.0, 1.0, -mb_ref[0]).astype(BF16)
        qT_ref[0, hd, HEAD_DIM + AUG_ROWS:, :] = zero_rows.astype(BF16)

        kh = kT[hd * HEAD_DIM:(hd + 1) * HEAD_DIM]
        kn = kh * (_head_rms_scale(kh) * gk_ref[...])
        k_aug = rows(1.0, 1.0, 1.0, -f3[0], -f3[1], -f3[2], 1.0)
        kfull = jnp.concatenate([kn, k_aug, zero_rows], axis=0)
        kp_ref[0, hd] = kfull.T.astype(BF16)

    for hd in range(n_fox):
        va = vT[hd * HEAD_DIM:(hd + 1) * HEAD_DIM].astype(BF16)
        for c in range(ts // tk):
            vT_ref[0, hd, c] = va[:, c * tk:(c + 1) * tk]


def _fox_out_kernel(fast_ref, qT_ref, kp_ref, vT_ref, gbT_ref, mam_ref, x_ref, wo_ref, o_ref,
                    acc_ref, den_ref, *, pool_w):
    hg = qT_ref.shape[1]
    tq = qT_ref.shape[3]
    tk = vT_ref.shape[4]
    n_diag = tq // tk
    i = pl.program_id(1)

    acc_ref[...] = jnp.zeros_like(acc_ref)
    den_ref[...] = jnp.zeros_like(den_ref)
    causal = (lax.broadcasted_iota(jnp.int32, (tk, tq), 0)
              <= lax.broadcasted_iota(jnp.int32, (tk, tq), 1))

    def head_scores(hd, j, lo=0):
        start = pl.multiple_of(j * tk, tk)
        return _dot(kp_ref[0, hd, pl.ds(start, tk), :], qT_ref[0, hd, :, lo:])

    def mask(s, diag, lo):
        return s if diag is None else jnp.where(causal[:, :tq - lo], s, MASK_VALUE)

    def fast_tiles(tiles):
        units = [(j, diag, hd) for (j, diag) in tiles for hd in range(hg)]
        scores = {}

        def issue(u):
            j, diag, hd = units[u]
            scores[u] = head_scores(hd, j, 0 if diag is None else diag * tk)

        for u in range(min(FAST_LOOKAHEAD, len(units))):
            issue(u)
        for u, (j, diag, hd) in enumerate(units):
            lo = 0 if diag is None else diag * tk
            p = jnp.exp2(mask(scores.pop(u), diag, lo))
            den_ref[hd, :, lo:] += jnp.sum(p, axis=0, keepdims=True)
            acc_ref[hd, :, lo:] += _dot(vT_ref[0, hd, j], p.astype(BF16))
            if u + FAST_LOOKAHEAD < len(units):
                issue(u + FAST_LOOKAHEAD)

    def safe_step(j, ms, diag=None):
        lo = 0 if diag is None else diag * tk
        scores = [head_scores(hd, j, lo) for hd in range(hg)]
        out = []
        for hd in range(hg):
            s = mask(scores[hd], diag, lo)
            m_old = ms[hd][:, lo:]
            m_new = jnp.maximum(m_old, jnp.max(s, axis=0, keepdims=True))
            p = jnp.exp2(s - m_new)
            alpha = jnp.exp2(m_old - m_new)
            den_ref[hd, :, lo:] = alpha * den_ref[hd, :, lo:] + jnp.sum(p, axis=0, keepdims=True)
            acc_ref[hd, :, lo:] = alpha * acc_ref[hd, :, lo:] + _dot(vT_ref[0, hd, j], p.astype(BF16))
            out.append(m_new if lo == 0 else jnp.concatenate([ms[hd][:, :lo], m_new], axis=1))
        return tuple(out)

    def finish():
        mam = mam_ref[0]
        fox_w = hg * HEAD_DIM
        part = (_dot_tn(mam[0:pool_w], wo_ref[0:pool_w, :])
                + _dot_tn(mam[pool_w:], wo_ref[pool_w + fox_w:, :]))
        ys = []
        for hd in range(hg):
            ys.append(acc_ref[hd] / den_ref[hd, 0:1, :])
        yb = (jnp.concatenate(ys, axis=0) * _silu(gbT_ref[0])).astype(BF16)
        o_ref[0] = x_ref[0] + part + _dot_tn(yb, wo_ref[pool_w:pool_w + fox_w, :])

    @pl.when(fast_ref[0] != 0)
    def _():
        def below_diagonal(first, count):
            fast_tiles([(first + d, None) for d in range(count)])

        def body(c, carry):
            below_diagonal(c * FAST_TILES_PER_TRIP, FAST_TILES_PER_TRIP)
            return carry

        n_below = i * n_diag
        n_trips = n_below // FAST_TILES_PER_TRIP
        lax.fori_loop(0, n_trips, body, 0)
        for rest in range(n_diag, FAST_TILES_PER_TRIP, n_diag):
            @pl.when(n_below - n_trips * FAST_TILES_PER_TRIP == rest)
            def _():
                below_diagonal(n_trips * FAST_TILES_PER_TRIP, rest)
        fast_tiles([(n_below + c, c) for c in range(n_diag)])
        finish()

    @pl.when(fast_ref[0] == 0)
    def _():
        ms = tuple(jnp.full((1, tq), MASK_VALUE, F32) for _ in range(hg))
        ms = lax.fori_loop(0, i * n_diag, lambda j, c: safe_step(j, c), ms)
        for c in range(n_diag):
            ms = safe_step(i * n_diag + c, ms, diag=c)
        finish()


def _col(v, rows=None):
    v = v.astype(F32)
    if rows is not None and rows > v.shape[0]:
        v = jnp.concatenate([v, jnp.zeros((rows - v.shape[0],), F32)])
    return v[:, None]


def _params(sem):
    return pltpu.CompilerParams(dimension_semantics=sem, vmem_limit_bytes=VMEM_LIMIT_BYTES)


def _layer(x, mem, norm_g, w_in, b_f, w_pool, pool_scale, fox_q_g, fox_k_g,
           mem_norm_g, w_mem_kv, mem_q_g, mem_k_g, w_out):
    bsz, seq, dm = x.shape
    n_mem = mem.shape[1]
    n_fox = b_f.shape[0]
    pool_w = pool_scale.shape[0]
    fox_w = n_fox * HEAD_DIM
    mem_w = MEM_HEADS * HEAD_DIM
    assert n_fox <= BF16_ROWS and pool_w == len(POOL_WINDOWS) * POOL_GROUP_DIM
    assert w_in.shape[1] == 2 * pool_w + 4 * fox_w + n_fox + 2 * mem_w
    assert w_out.shape[0] == pool_w + fox_w + mem_w
    ts, tq, tk = SEQ_TILE, Q_TILE, KV_TILE
    assert seq % ts == 0 and seq % tq == 0 and tq % tk == 0 and ts % tk == 0
    assert FAST_TILES_PER_TRIP % (tq // tk) == 0

    sizes = (pool_w, pool_w, fox_w, fox_w, fox_w, n_fox, fox_w, mem_w, mem_w)
    parts, off = [], 0
    for w in sizes:
        parts.append(w_in[:, off:off + w])
        off += w
    w_ua, w_ga, w_q, w_k, w_v, w_f, w_gb, w_qm, w_gm = parts
    w_f = jnp.concatenate([w_f, jnp.zeros((dm, BF16_ROWS - n_fox), w_in.dtype)], axis=1)
    order = [("ua", w_ua), ("ga", w_ga), ("f", w_f), ("q", w_q), ("k", w_k), ("v", w_v),
             ("gb", w_gb), ("qm", w_qm), ("gm", w_gm)]
    offs, off = {}, 0
    for name, w in order:
        offs[name] = off
        off += w.shape[1]
    wT = jnp.concatenate([w for _, w in order], axis=1).T.astype(BF16)
    n_rows = wT.shape[0]
    tri = (lax.broadcasted_iota(jnp.int32, (ts, ts), 0)
           <= lax.broadcasted_iota(jnp.int32, (ts, ts), 1)).astype(BF16)
    wp_bd = jax.scipy.linalg.block_diag(*[w_pool[g].T for g in range(len(POOL_WINDOWS))]).astype(BF16)
    wkvT = w_mem_kv.T.astype(BF16)
    wo = w_out.astype(BF16)

    km, vm = pl.pallas_call(
        _mem_kv_kernel,
        grid=(bsz,),
        in_specs=[
            pl.BlockSpec((1, n_mem, dm), lambda b: (b, 0, 0)),
            pl.BlockSpec((1, dm), lambda b: (0, 0)),
            pl.BlockSpec((2 * mem_w, dm), lambda b: (0, 0)),
            pl.BlockSpec((HEAD_DIM, 1), lambda b: (0, 0)),
        ],
        out_specs=[
            pl.BlockSpec((1, MEM_HEADS, n_mem, QK_ROWS), lambda b: (b, 0, 0, 0)),
            pl.BlockSpec((1, MEM_HEADS, MEM_V_ROWS, n_mem), lambda b: (b, 0, 0, 0)),
        ],
        out_shape=[
            jax.ShapeDtypeStruct((bsz, MEM_HEADS, n_mem, QK_ROWS), BF16),
            jax.ShapeDtypeStruct((bsz, MEM_HEADS, MEM_V_ROWS, n_mem), BF16),
        ],
        compiler_params=_params(("arbitrary",)),
        name="mem_kv",
    )(mem, mem_norm_g[None, :], wkvT, _col(mem_k_g))

    const2 = lambda b, s: (0, 0)
    m_bound = (HEAD_DIM * SCALE * LOG2E * BOUND_SLACK
               * jnp.max(jnp.abs(fox_q_g)) * jnp.max(jnp.abs(fox_k_g))).astype(F32).reshape(1)
    fast = (m_bound < FAST_MAX_SHIFT).astype(jnp.int32)
    qT, kp, vT, gbT, mam = pl.pallas_call(
        functools.partial(_proj_kernel, offs=offs, n_fox=n_fox, pool_w=pool_w,
                          fox_w=fox_w, mem_w=mem_w),
        grid=(bsz, seq // ts),
        in_specs=[
            pl.BlockSpec(memory_space=pltpu.SMEM),
            pl.BlockSpec((1, ts, dm), lambda b, s: (b, s, 0)),
            pl.BlockSpec((1, dm), const2),
            pl.BlockSpec((n_rows, dm), const2),
            pl.BlockSpec((BF16_ROWS, 1), const2),
            pl.BlockSpec((ts, ts), const2),
            pl.BlockSpec((HEAD_DIM, 1), const2),
            pl.BlockSpec((HEAD_DIM, 1), const2),
            pl.BlockSpec((pool_w, pool_w), const2),
            pl.BlockSpec((pool_w, 1), const2),
            pl.BlockSpec((1, MEM_HEADS, n_mem, QK_ROWS), lambda b, s: (b, 0, 0, 0)),
            pl.BlockSpec((1, MEM_HEADS, MEM_V_ROWS, n_mem), lambda b, s: (b, 0, 0, 0)),
            pl.BlockSpec((HEAD_DIM, 1), const2),
        ],
        out_specs=[
            pl.BlockSpec((1, n_fox, QK_ROWS, ts), lambda b, s: (b, 0, 0, s)),
            pl.BlockSpec((1, n_fox, ts, QK_ROWS), lambda b, s: (b, 0, s, 0)),
            pl.BlockSpec((1, n_fox, ts // tk, HEAD_DIM, tk), lambda b, s: (b, 0, s, 0, 0)),
            pl.BlockSpec((1, fox_w, ts), lambda b, s: (b, 0, s)),
            pl.BlockSpec((1, pool_w + mem_w, ts), lambda b, s: (b, 0, s)),
        ],
        out_shape=[
            jax.ShapeDtypeStruct((bsz, n_fox, QK_ROWS, seq), BF16),
            jax.ShapeDtypeStruct((bsz, n_fox, seq, QK_ROWS), BF16),
            jax.ShapeDtypeStruct((bsz, n_fox, seq // tk, HEAD_DIM, tk), BF16),
            jax.ShapeDtypeStruct((bsz, fox_w, seq), F32),
            jax.ShapeDtypeStruct((bsz, pool_w + mem_w, seq), BF16),
        ],
        scratch_shapes=[
            pltpu.VMEM((BF16_ROWS, LANES), F32),
            pltpu.VMEM((pool_w, LANES), F32),
        ],
        compiler_params=_params(("arbitrary", "arbitrary")),
        name="proj",
    )(m_bound, x, norm_g[None, :], wT, _col(b_f, BF16_ROWS), tri, _col(fox_q_g), _col(fox_k_g),
      wp_bd, _col(pool_scale), km, vm, _col(mem_q_g))

    return pl.pallas_call(
        functools.partial(_fox_out_kernel, pool_w=pool_w),
        grid=(bsz, seq // tq),
        in_specs=[
            pl.BlockSpec(memory_space=pltpu.SMEM),
            pl.BlockSpec((1, n_fox, QK_ROWS, tq), lambda b, i: (b, 0, 0, i)),
            pl.BlockSpec((1, n_fox, seq, QK_ROWS), lambda b, i: (b, 0, 0, 0)),
            pl.BlockSpec((1, n_fox, seq // tk, HEAD_DIM, tk), lambda b, i: (b, 0, 0, 0, 0)),
            pl.BlockSpec((1, fox_w, tq), lambda b, i: (b, 0, i)),
            pl.BlockSpec((1, pool_w + mem_w, tq), lambda b, i: (b, 0, i)),
            pl.BlockSpec((1, tq, dm), lambda b, i: (b, i, 0)),
            pl.BlockSpec((pool_w + fox_w + mem_w, dm), const2),
        ],
        out_specs=pl.BlockSpec((1, tq, dm), lambda b, i: (b, i, 0)),
        out_shape=jax.ShapeDtypeStruct((bsz, seq, dm), x.dtype),
        scratch_shapes=[pltpu.VMEM((n_fox, HEAD_DIM, tq), F32), pltpu.VMEM((n_fox, SUBLANES, tq), F32)],
        compiler_params=_params(("arbitrary", "arbitrary")),
        name="fox_out",
    )(fast, qT, kp, vT, gbT, mam, x, wo)


def kernel(x, mem, norm_g, w_in, b_f, w_pool, pool_scale, fox_q_g, fox_k_g, mem_norm_g,
           w_mem_kv, mem_q_g, mem_k_g, w_out):
    for l in range(norm_g.shape[0]):
        x = _layer(x, mem, norm_g[l], w_in[l], b_f[l], w_pool[l], pool_scale[l], fox_q_g[l],
                   fox_k_g[l], mem_norm_g[l], w_mem_kv[l], mem_q_g[l], mem_k_g[l], w_out[l])
    return x
```

```python
import functools
import math

import jax
import jax.numpy as jnp
from jax import lax
from jax.experimental import pallas as pl
from jax.experimental.pallas import tpu as pltpu

F32 = jnp.float32
BF16 = jnp.bfloat16

HEAD_DIM = 64
EPS = 1e-6
POOL_WINDOWS = (2, 4, 8, 16)
POOL_GROUP_DIM = 64
MEM_HEADS = 4
SCALE = 1.0 / math.sqrt(HEAD_DIM)
LOG2E = math.log2(math.e)
MASK_VALUE = -1e30

LANES = 128
SUBLANES = 8
BF16_ROWS = 16
QK_ROWS = 128
AUG_ROWS = BF16_ROWS
MEM_V_ROWS = HEAD_DIM + BF16_ROWS

SEQ_TILE = 1024
Q_TILE = 512
KV_TILE = 256
FAST_TILES_PER_TRIP = 8
FAST_LOOKAHEAD = 3
FAST_MAX_SHIFT = 50.0
BOUND_SLACK = 1.01
VMEM_LIMIT_BYTES = 56 * 1024 * 1024

NT_DIMS = (((1,), (1,)), ((), ()))
TN_DIMS = (((0,), (0,)), ((), ()))


def _dot(a, b):
    return jnp.dot(a, b, preferred_element_type=F32)


def _dot_nt(a, b):
    return lax.dot_general(a, b, NT_DIMS, preferred_element_type=F32)


def _dot_tn(a, b):
    return lax.dot_general(a, b, TN_DIMS, preferred_element_type=F32)


def _silu(g):
    return g * jax.nn.sigmoid(g)


def _log_sigmoid(z):
    return jnp.minimum(z, 0.0) - jnp.log1p(jnp.exp(-jnp.abs(z)))


def _head_rms_scale(t):
    return lax.rsqrt(jnp.mean(t * t, axis=0, keepdims=True) + EPS)


def _ones_row_block(n):
    row = lax.broadcasted_iota(jnp.int32, (BF16_ROWS, n), 0)
    return (row == 0).astype(F32)


def _mem_kv_kernel(mem_ref, g_ref, wkvT_ref, gk_ref, km_ref, vm_ref):
    x = mem_ref[0]
    n_mem = x.shape[0]
    h = (x * lax.rsqrt(jnp.mean(x * x, axis=-1, keepdims=True) + EPS) * g_ref[...]).astype(BF16)
    kvT = _dot_nt(wkvT_ref[...], h)
    width = MEM_HEADS * HEAD_DIM
    ones_blk = _ones_row_block(n_mem)
    zero_pad = jnp.zeros((QK_ROWS - HEAD_DIM, n_mem), F32)
    for hm in range(MEM_HEADS):
        kh = kvT[hm * HEAD_DIM:(hm + 1) * HEAD_DIM]
        kn = kh * _head_rms_scale(kh) * gk_ref[...]
        km_ref[0, hm] = jnp.concatenate([kn, zero_pad], axis=0).T.astype(BF16)
        vh = kvT[width + hm * HEAD_DIM:width + (hm + 1) * HEAD_DIM]
        vm_ref[0, hm] = jnp.concatenate([vh, ones_blk], axis=0).astype(BF16)


def _proj_kernel(mb_ref, x_ref, ng_ref, wT_ref, bf_ref, tri_ref, gq_ref, gk_ref, wp_ref, ps_ref,
                 km_ref, vm_ref, gmq_ref,
                 qT_ref, kp_ref, vT_ref, gbT_ref, mam_ref,
                 fcarry_ref, halo_ref, *, offs, n_fox, pool_w, fox_w, mem_w):
    ts = x_ref.shape[1]
    s_idx = pl.program_id(1)

    @pl.when(s_idx == 0)
    def _():
        fcarry_ref[...] = jnp.zeros_like(fcarry_ref)
        halo_ref[...] = jnp.zeros_like(halo_ref)

    x = x_ref[0]
    h = (x * lax.rsqrt(jnp.mean(x * x, axis=-1, keepdims=True) + EPS) * ng_ref[...]).astype(BF16)

    def proj_t(lo, n):
        return _dot_nt(wT_ref[lo:lo + n, :], h)

    tk = vT_ref.shape[4]
    gd = POOL_GROUP_DIM

    sec_m = proj_t(offs["qm"], 2 * mem_w)
    qm = sec_m[0:mem_w]
    gm = sec_m[mem_w:2 * mem_w]
    sec_a = proj_t(offs["ua"], 2 * pool_w + BF16_ROWS)
    u = sec_a[0:pool_w]
    ga = sec_a[pool_w:2 * pool_w]
    z = sec_a[2 * pool_w:2 * pool_w + BF16_ROWS] + bf_ref[...]

    zero_pad = jnp.zeros((QK_ROWS - HEAD_DIM, ts), F32)
    mq_gain = gmq_ref[...] * (SCALE * LOG2E)
    lgs = []
    for hm in range(MEM_HEADS):
        qh = qm[hm * HEAD_DIM:(hm + 1) * HEAD_DIM]
        qn = qh * (_head_rms_scale(qh) * mq_gain)
        qpad = jnp.concatenate([qn, zero_pad], axis=0).astype(BF16)
        lgs.append(_dot(km_ref[0, hm], qpad))

    qT = proj_t(offs["q"], fox_w)

    logf = _log_sigmoid(z)
    hi = logf.astype(BF16)
    r1 = logf - hi.astype(F32)
    mid = r1.astype(BF16)
    lo = (r1 - mid.astype(F32)).astype(BF16)
    cs = _dot(jnp.concatenate([hi, mid, lo], axis=0), tri_ref[...])

    kT = proj_t(offs["k"], fox_w)
    vT = proj_t(offs["v"], fox_w)

    for hm in range(MEM_HEADS):
        lg = lgs[hm]
        p = jnp.exp2(lg - jnp.max(lg, axis=0, keepdims=True)).astype(BF16)
        yv = _dot(vm_ref[0, hm], p)
        y = yv[0:HEAD_DIM] / yv[HEAD_DIM:HEAD_DIM + 1]
        mm = y * _silu(gm[hm * HEAD_DIM:(hm + 1) * HEAD_DIM])
        mam_ref[0, pool_w + hm * HEAD_DIM:pool_w + (hm + 1) * HEAD_DIM, :] = mm.astype(BF16)

    uext = jnp.concatenate([halo_ref[...], u], axis=1)
    halo_ref[...] = u[:, ts - LANES:]
    pos1 = (s_idx * ts + lax.broadcasted_iota(jnp.int32, (1, ts), 1) + 1).astype(F32)
    acc = uext
    pooled = []
    shift = 1
    for g, w in enumerate(POOL_WINDOWS):
        while shift < w:
            acc = acc + pltpu.roll(acc, shift, 1)
            shift *= 2
        pooled.append(acc[0:gd, LANES:] / jnp.minimum(pos1, float(w)))
        acc = acc[gd:]
    d = (jnp.concatenate(pooled, axis=0) - u).astype(BF16)
    ya = _dot(wp_ref[...], d) * ps_ref[...]
    mam_ref[0, 0:pool_w, :] = (ya * _silu(ga)).astype(BF16)

    gbT_ref[0] = proj_t(offs["gb"], fox_w)

    fcum = (cs[0:BF16_ROWS] + cs[BF16_ROWS:2 * BF16_ROWS] + cs[2 * BF16_ROWS:3 * BF16_ROWS]
            + fcarry_ref[:, LANES - 1:LANES])
    fcarry_ref[...] = fcum[:, ts - LANES:]
    f2 = fcum * LOG2E

    arow = lax.broadcasted_iota(jnp.int32, (AUG_ROWS, ts), 0)

    def split3(v):
        v_hi = v.astype(BF16).astype(F32)
        r = v - v_hi
        v_mid = r.astype(BF16).astype(F32)
        return v_hi, v_mid, (r - v_mid).astype(BF16).astype(F32)

    def rows(*vals):
        out = jnp.zeros((AUG_ROWS, ts), F32)
        for r, v in enumerate(vals):
            out = jnp.where(arow == r, v, out)
        return out

    zero_rows = jnp.zeros((QK_ROWS - HEAD_DIM - AUG_ROWS, ts), F32)
    q_gain = gq_ref[...] * (SCALE * LOG2E)
    for hd in range(n_fox):
        qh = qT[hd * HEAD_DIM:(hd + 1) * HEAD_DIM]
        qn = qh * (_head_rms_scale(qh) * q_gain)
        f3 = split3(f2[hd:hd + 1])
        qT_ref[0, hd, 0:HEAD_DIM, :] = qn.astype(BF16)
        qT_ref[0, hd, HEAD_DIM:HEAD_DIM + AUG_ROWS, :] = rows(
            f3[0], f3[1], f3[2], 1.0, 1.0, 1.0, -mb_ref[0]).astype(BF16)
        qT_ref[0, hd, HEAD_DIM + AUG_ROWS:, :] = zero_rows.astype(BF16)

        kh = kT[hd * HEAD_DIM:(hd + 1) * HEAD_DIM]
        kn = kh * (_head_rms_scale(kh) * gk_ref[...])
        k_aug = rows(1.0, 1.0, 1.0, -f3[0], -f3[1], -f3[2], 1.0)
        kfull = jnp.concatenate([kn, k_aug, zero_rows], axis=0)
        kp_ref[0, hd] = kfull.T.astype(BF16)

    for hd in range(n_fox):
        va = vT[hd * HEAD_DIM:(hd + 1) * HEAD_DIM].astype(BF16)
        for c in range(ts // tk):
            vT_ref[0, hd, c] = va[:, c * tk:(c + 1) * tk]


def _fox_out_kernel(fast_ref, qT_ref, kp_ref, vT_ref, gbT_ref, mam_ref, x_ref, wo_ref, o_ref,
                    acc_ref, den_ref, *, pool_w):
    hg = qT_ref.shape[1]
    tq = qT_ref.shape[3]
    tk = vT_ref.shape[4]
    n_diag = tq // tk
    i = pl.program_id(1)

    acc_ref[...] = jnp.zeros_like(acc_ref)
    den_ref[...] = jnp.zeros_like(den_ref)
    causal = (lax.broadcasted_iota(jnp.int32, (tk, tq), 0)
              <= lax.broadcasted_iota(jnp.int32, (tk, tq), 1))

    def head_scores(hd, j, lo=0):
        start = pl.multiple_of(j * tk, tk)
        return _dot(kp_ref[0, hd, pl.ds(start, tk), :], qT_ref[0, hd, :, lo:])

    def mask(s, diag, lo):
        return s if diag is None else jnp.where(causal[:, :tq - lo], s, MASK_VALUE)

    def fast_tiles(tiles):
        units = [(j, diag, hd) for (j, diag) in tiles for hd in range(hg)]
        scores = {}

        def issue(u):
            j, diag, hd = units[u]
            scores[u] = head_scores(hd, j, 0 if diag is None else diag * tk)

        for u in range(min(FAST_LOOKAHEAD, len(units))):
            issue(u)
        for u, (j, diag, hd) in enumerate(units):
            lo = 0 if diag is None else diag * tk
            p = jnp.exp2(mask(scores.pop(u), diag, lo))
            den_ref[hd, :, lo:] += jnp.sum(p, axis=0, keepdims=True)
            acc_ref[hd, :, lo:] += _dot(vT_ref[0, hd, j], p.astype(BF16))
            if u + FAST_LOOKAHEAD < len(units):
                issue(u + FAST_LOOKAHEAD)

    def safe_step(j, ms, diag=None):
        lo = 0 if diag is None else diag * tk
        scores = [head_scores(hd, j, lo) for hd in range(hg)]
        out = []
        for hd in range(hg):
            s = mask(scores[hd], diag, lo)
            m_old = ms[hd][:, lo:]
            m_new = jnp.maximum(m_old, jnp.max(s, axis=0, keepdims=True))
            p = jnp.exp2(s - m_new)
            alpha = jnp.exp2(m_old - m_new)
            den_ref[hd, :, lo:] = alpha * den_ref[hd, :, lo:] + jnp.sum(p, axis=0, keepdims=True)
            acc_ref[hd, :, lo:] = alpha * acc_ref[hd, :, lo:] + _dot(vT_ref[0, hd, j], p.astype(BF16))
            out.append(m_new if lo == 0 else jnp.concatenate([ms[hd][:, :lo], m_new], axis=1))
        return tuple(out)

    def finish():
        mam = mam_ref[0]
        fox_w = hg * HEAD_DIM
        part = (_dot_tn(mam[0:pool_w], wo_ref[0:pool_w, :])
                + _dot_tn(mam[pool_w:], wo_ref[pool_w + fox_w:, :]))
        ys = []
        for hd in range(hg):
            ys.append(acc_ref[hd] / den_ref[hd, 0:1, :])
        yb = (jnp.concatenate(ys, axis=0) * _silu(gbT_ref[0])).astype(BF16)
        o_ref[0] = x_ref[0] + part + _dot_tn(yb, wo_ref[pool_w:pool_w + fox_w, :])

    @pl.when(fast_ref[0] != 0)
    def _():
        def below_diagonal(first, count):
            return [(first + d, None) for d in range(count)]

        def body(c, carry):
            fast_tiles(below_diagonal(c * FAST_TILES_PER_TRIP, FAST_TILES_PER_TRIP))
            return carry

        n_below = i * n_diag
        n_trips = n_below // FAST_TILES_PER_TRIP
        lax.fori_loop(0, n_trips, body, 0)
        for rest in range(0, FAST_TILES_PER_TRIP, n_diag):
            @pl.when(n_below - n_trips * FAST_TILES_PER_TRIP == rest)
            def _():
                fast_tiles(below_diagonal(n_trips * FAST_TILES_PER_TRIP, rest)
                           + [(n_below + c, c) for c in range(n_diag)])
                finish()

    @pl.when(fast_ref[0] == 0)
    def _():
        ms = tuple(jnp.full((1, tq), MASK_VALUE, F32) for _ in range(hg))
        ms = lax.fori_loop(0, i * n_diag, lambda j, c: safe_step(j, c), ms)
        for c in range(n_diag):
            ms = safe_step(i * n_diag + c, ms, diag=c)
        finish()


def _col(v, rows=None):
    v = v.astype(F32)
    if rows is not None and rows > v.shape[0]:
        v = jnp.concatenate([v, jnp.zeros((rows - v.shape[0],), F32)])
    return v[:, None]


def _params(sem):
    return pltpu.CompilerParams(dimension_semantics=sem, vmem_limit_bytes=VMEM_LIMIT_BYTES)


def _layer(x, mem, norm_g, w_in, b_f, w_pool, pool_scale, fox_q_g, fox_k_g,
           mem_norm_g, w_mem_kv, mem_q_g, mem_k_g, w_out):
    bsz, seq, dm = x.shape
    n_mem = mem.shape[1]
    n_fox = b_f.shape[0]
    pool_w = pool_scale.shape[0]
    fox_w = n_fox * HEAD_DIM
    mem_w = MEM_HEADS * HEAD_DIM
    assert n_fox <= BF16_ROWS and pool_w == len(POOL_WINDOWS) * POOL_GROUP_DIM
    assert w_in.shape[1] == 2 * pool_w + 4 * fox_w + n_fox + 2 * mem_w
    assert w_out.shape[0] == pool_w + fox_w + mem_w
    ts, tq, tk = SEQ_TILE, Q_TILE, KV_TILE
    assert seq % ts == 0 and seq % tq == 0 and tq % tk == 0 and ts % tk == 0
    assert FAST_TILES_PER_TRIP % (tq // tk) == 0

    sizes = (pool_w, pool_w, fox_w, fox_w, fox_w, n_fox, fox_w, mem_w, mem_w)
    parts, off = [], 0
    for w in sizes:
        parts.append(w_in[:, off:off + w])
        off += w
    w_ua, w_ga, w_q, w_k, w_v, w_f, w_gb, w_qm, w_gm = parts
    w_f = jnp.concatenate([w_f, jnp.zeros((dm, BF16_ROWS - n_fox), w_in.dtype)], axis=1)
    order = [("ua", w_ua), ("ga", w_ga), ("f", w_f), ("q", w_q), ("k", w_k), ("v", w_v),
             ("gb", w_gb), ("qm", w_qm), ("gm", w_gm)]
    offs, off = {}, 0
    for name, w in order:
        offs[name] = off
        off += w.shape[1]
    wT = jnp.concatenate([w for _, w in order], axis=1).T.astype(BF16)
    n_rows = wT.shape[0]
    tri = (lax.broadcasted_iota(jnp.int32, (ts, ts), 0)
           <= lax.broadcasted_iota(jnp.int32, (ts, ts), 1)).astype(BF16)
    wp_bd = jax.scipy.linalg.block_diag(*[w_pool[g].T for g in range(len(POOL_WINDOWS))]).astype(BF16)
    wkvT = w_mem_kv.T.astype(BF16)
    wo = w_out.astype(BF16)

    km, vm = pl.pallas_call(
        _mem_kv_kernel,
        grid=(bsz,),
        in_specs=[
            pl.BlockSpec((1, n_mem, dm), lambda b: (b, 0, 0)),
            pl.BlockSpec((1, dm), lambda b: (0, 0)),
            pl.BlockSpec((2 * mem_w, dm), lambda b: (0, 0)),
            pl.BlockSpec((HEAD_DIM, 1), lambda b: (0, 0)),
        ],
        out_specs=[
            pl.BlockSpec((1, MEM_HEADS, n_mem, QK_ROWS), lambda b: (b, 0, 0, 0)),
            pl.BlockSpec((1, MEM_HEADS, MEM_V_ROWS, n_mem), lambda b: (b, 0, 0, 0)),
        ],
        out_shape=[
            jax.ShapeDtypeStruct((bsz, MEM_HEADS, n_mem, QK_ROWS), BF16),
            jax.ShapeDtypeStruct((bsz, MEM_HEADS, MEM_V_ROWS, n_mem), BF16),
        ],
        compiler_params=_params(("arbitrary",)),
        name="mem_kv",
    )(mem, mem_norm_g[None, :], wkvT, _col(mem_k_g))

    const2 = lambda b, s: (0, 0)
    m_bound = (HEAD_DIM * SCALE * LOG2E * BOUND_SLACK
               * jnp.max(jnp.abs(fox_q_g)) * jnp.max(jnp.abs(fox_k_g))).astype(F32).reshape(1)
    fast = (m_bound < FAST_MAX_SHIFT).astype(jnp.int32)
    qT, kp, vT, gbT, mam = pl.pallas_call(
        functools.partial(_proj_kernel, offs=offs, n_fox=n_fox, pool_w=pool_w,
                          fox_w=fox_w, mem_w=mem_w),
        grid=(bsz, seq // ts),
        in_specs=[
            pl.BlockSpec(memory_space=pltpu.SMEM),
            pl.BlockSpec((1, ts, dm), lambda b, s: (b, s, 0)),
            pl.BlockSpec((1, dm), const2),
            pl.BlockSpec((n_rows, dm), const2),
            pl.BlockSpec((BF16_ROWS, 1), const2),
            pl.BlockSpec((ts, ts), const2),
            pl.BlockSpec((HEAD_DIM, 1), const2),
            pl.BlockSpec((HEAD_DIM, 1), const2),
            pl.BlockSpec((pool_w, pool_w), const2),
            pl.BlockSpec((pool_w, 1), const2),
            pl.BlockSpec((1, MEM_HEADS, n_mem, QK_ROWS), lambda b, s: (b, 0, 0, 0)),
            pl.BlockSpec((1, MEM_HEADS, MEM_V_ROWS, n_mem), lambda b, s: (b, 0, 0, 0)),
            pl.BlockSpec((HEAD_DIM, 1), const2),
        ],
        out_specs=[
            pl.BlockSpec((1, n_fox, QK_ROWS, ts), lambda b, s: (b, 0, 0, s)),
            pl.BlockSpec((1, n_fox, ts, QK_ROWS), lambda b, s: (b, 0, s, 0)),
            pl.BlockSpec((1, n_fox, ts // tk, HEAD_DIM, tk), lambda b, s: (b, 0, s, 0, 0)),
            pl.BlockSpec((1, fox_w, ts), lambda b, s: (b, 0, s)),
            pl.BlockSpec((1, pool_w + mem_w, ts), lambda b, s: (b, 0, s)),
        ],
        out_shape=[
            jax.ShapeDtypeStruct((bsz, n_fox, QK_ROWS, seq), BF16),
            jax.ShapeDtypeStruct((bsz, n_fox, seq, QK_ROWS), BF16),
            jax.ShapeDtypeStruct((bsz, n_fox, seq // tk, HEAD_DIM, tk), BF16),
            jax.ShapeDtypeStruct((bsz, fox_w, seq), F32),
            jax.ShapeDtypeStruct((bsz, pool_w + mem_w, seq), BF16),
        ],
        scratch_shapes=[
            pltpu.VMEM((BF16_ROWS, LANES), F32),
            pltpu.VMEM((pool_w, LANES), F32),
        ],
        compiler_params=_params(("arbitrary", "arbitrary")),
        name="proj",
    )(m_bound, x, norm_g[None, :], wT, _col(b_f, BF16_ROWS), tri, _col(fox_q_g), _col(fox_k_g),
      wp_bd, _col(pool_scale), km, vm, _col(mem_q_g))

    return pl.pallas_call(
        functools.partial(_fox_out_kernel, pool_w=pool_w),
        grid=(bsz, seq // tq),
        in_specs=[
            pl.BlockSpec(memory_space=pltpu.SMEM),
            pl.BlockSpec((1, n_fox, QK_ROWS, tq), lambda b, i: (b, 0, 0, i)),
            pl.BlockSpec((1, n_fox, seq, QK_ROWS), lambda b, i: (b, 0, 0, 0)),
            pl.BlockSpec((1, n_fox, seq // tk, HEAD_DIM, tk), lambda b, i: (b, 0, 0, 0, 0)),
            pl.BlockSpec((1, fox_w, tq), lambda b, i: (b, 0, i)),
            pl.BlockSpec((1, pool_w + mem_w, tq), lambda b, i: (b, 0, i)),
            pl.BlockSpec((1, tq, dm), lambda b, i: (b, i, 0)),
            pl.BlockSpec((pool_w + fox_w + mem_w, dm), const2),
        ],
        out_specs=pl.BlockSpec((1, tq, dm), lambda b, i: (b, i, 0)),
        out_shape=jax.ShapeDtypeStruct((bsz, seq, dm), x.dtype),
        scratch_shapes=[pltpu.VMEM((n_fox, HEAD_DIM, tq), F32), pltpu.VMEM((n_fox, SUBLANES, tq), F32)],
        compiler_params=_params(("arbitrary", "arbitrary")),
        name="fox_out",
    )(fast, qT, kp, vT, gbT, mam, x, wo)


def kernel(x, mem, norm_g, w_in, b_f, w_pool, pool_scale, fox_q_g, fox_k_g, mem_norm_g,
           w_mem_kv, mem_q_g, mem_k_g, w_out):
    for l in range(norm_g.shape[0]):
        x = _layer(x, mem, norm_g[l], w_in[l], b_f[l], w_pool[l], pool_scale[l], fox_q_g[l],
                   fox_k_g[l], mem_norm_g[l], w_mem_kv[l], mem_q_g[l], mem_k_g[l], w_out[l])
    return x
```

```python
import functools
import math

import jax
import jax.numpy as jnp
from jax import lax
from jax.experimental import pallas as pl
from jax.experimental.pallas import tpu as pltpu

F32 = jnp.float32
BF16 = jnp.bfloat16

HEAD_DIM = 64
EPS = 1e-6
POOL_WINDOWS = (2, 4, 8, 16)
POOL_GROUP_DIM = 64
MEM_HEADS = 4
SCALE = 1.0 / math.sqrt(HEAD_DIM)
LOG2E = math.log2(math.e)
MASK_VALUE = -1e30

LANES = 128
SUBLANES = 8
BF16_ROWS = 16
QK_ROWS = 128
AUG_ROWS = BF16_ROWS
MEM_V_ROWS = HEAD_DIM + BF16_ROWS

MEM_BATCH_ROWS_PER_STEP = 4
SEQ_TILE = 1024
Q_TILE = 512
KV_TILE = 256
FAST_TILES_PER_TRIP = 8
FAST_LOOKAHEAD = 3
FAST_MAX_SHIFT = 50.0
BOUND_SLACK = 1.01
VMEM_LIMIT_BYTES = 56 * 1024 * 1024

NT_DIMS = (((1,), (1,)), ((), ()))
TN_DIMS = (((0,), (0,)), ((), ()))


def _dot(a, b):
    return jnp.dot(a, b, preferred_element_type=F32)


def _dot_nt(a, b):
    return lax.dot_general(a, b, NT_DIMS, preferred_element_type=F32)


def _dot_tn(a, b):
    return lax.dot_general(a, b, TN_DIMS, preferred_element_type=F32)


def _silu(g):
    return g * jax.nn.sigmoid(g)


def _log_sigmoid(z):
    return jnp.minimum(z, 0.0) - jnp.log1p(jnp.exp(-jnp.abs(z)))


def _head_rms_scale(t):
    return lax.rsqrt(jnp.mean(t * t, axis=0, keepdims=True) + EPS)


def _ones_row_block(n):
    row = lax.broadcasted_iota(jnp.int32, (BF16_ROWS, n), 0)
    return (row == 0).astype(F32)


def _mem_kv_kernel(mem_ref, g_ref, wkvT_ref, gk_ref, km_ref, vm_ref):
    nb, n_mem, dm = mem_ref.shape
    x = mem_ref[...].reshape(nb * n_mem, dm)
    h = (x * lax.rsqrt(jnp.mean(x * x, axis=-1, keepdims=True) + EPS) * g_ref[...]).astype(BF16)
    kvT = _dot_nt(wkvT_ref[...], h)
    width = MEM_HEADS * HEAD_DIM
    ones_blk = _ones_row_block(n_mem)
    zero_pad = jnp.zeros((QK_ROWS - HEAD_DIM, n_mem), F32)
    for hm in range(MEM_HEADS):
        kh = kvT[hm * HEAD_DIM:(hm + 1) * HEAD_DIM]
        kn = kh * _head_rms_scale(kh) * gk_ref[...]
        vh = kvT[width + hm * HEAD_DIM:width + (hm + 1) * HEAD_DIM]
        for bi in range(nb):
            cols = slice(bi * n_mem, (bi + 1) * n_mem)
            km_ref[bi, hm] = jnp.concatenate([kn[:, cols], zero_pad], axis=0).T.astype(BF16)
            vm_ref[bi, hm] = jnp.concatenate([vh[:, cols], ones_blk], axis=0).astype(BF16)


def _proj_kernel(mb_ref, x_ref, ng_ref, wT_ref, bf_ref, tri_ref, gq_ref, gk_ref, wp_ref, ps_ref,
                 km_ref, vm_ref, gmq_ref,
                 qT_ref, kp_ref, vT_ref, gbT_ref, mam_ref,
                 fcarry_ref, halo_ref, *, offs, n_fox, pool_w, fox_w, mem_w):
    ts = x_ref.shape[1]
    s_idx = pl.program_id(1)

    @pl.when(s_idx == 0)
    def _():
        fcarry_ref[...] = jnp.zeros_like(fcarry_ref)
        halo_ref[...] = jnp.zeros_like(halo_ref)

    x = x_ref[0]
    h = (x * lax.rsqrt(jnp.mean(x * x, axis=-1, keepdims=True) + EPS) * ng_ref[...]).astype(BF16)

    def proj_t(lo, n):
        return _dot_nt(wT_ref[lo:lo + n, :], h)

    tk = vT_ref.shape[4]
    gd = POOL_GROUP_DIM

    sec_m = proj_t(offs["qm"], 2 * mem_w)
    qm = sec_m[0:mem_w]
    gm = sec_m[mem_w:2 * mem_w]
    sec_a = proj_t(offs["ua"], 2 * pool_w + BF16_ROWS)
    u = sec_a[0:pool_w]
    ga = sec_a[pool_w:2 * pool_w]
    z = sec_a[2 * pool_w:2 * pool_w + BF16_ROWS] + bf_ref[...]

    zero_pad = jnp.zeros((QK_ROWS - HEAD_DIM, ts), F32)
    mq_gain = gmq_ref[...] * (SCALE * LOG2E)
    lgs = []
    for hm in range(MEM_HEADS):
        qh = qm[hm * HEAD_DIM:(hm + 1) * HEAD_DIM]
        qn = qh * (_head_rms_scale(qh) * mq_gain)
        qpad = jnp.concatenate([qn, zero_pad], axis=0).astype(BF16)
        lgs.append(_dot(km_ref[0, hm], qpad))

    qT = proj_t(offs["q"], fox_w)

    logf = _log_sigmoid(z)
    hi = logf.astype(BF16)
    r1 = logf - hi.astype(F32)
    mid = r1.astype(BF16)
    lo = (r1 - mid.astype(F32)).astype(BF16)
    cs = _dot(jnp.concatenate([hi, mid, lo], axis=0), tri_ref[...])

    kT = proj_t(offs["k"], fox_w)
    vT = proj_t(offs["v"], fox_w)

    for hm in range(MEM_HEADS):
        lg = lgs[hm]
        p = jnp.exp2(lg - jnp.max(lg, axis=0, keepdims=True)).astype(BF16)
        yv = _dot(vm_ref[0, hm], p)
        y = yv[0:HEAD_DIM] / yv[HEAD_DIM:HEAD_DIM + 1]
        mm = y * _silu(gm[hm * HEAD_DIM:(hm + 1) * HEAD_DIM])
        mam_ref[0, pool_w + hm * HEAD_DIM:pool_w + (hm + 1) * HEAD_DIM, :] = mm.astype(BF16)

    uext = jnp.concatenate([halo_ref[...], u], axis=1)
    halo_ref[...] = u[:, ts - LANES:]
    pos1 = (s_idx * ts + lax.broadcasted_iota(jnp.int32, (1, ts), 1) + 1).astype(F32)
    acc = uext
    pooled = []
    shift = 1
    for g, w in enumerate(POOL_WINDOWS):
        while shift < w:
            acc = acc + pltpu.roll(acc, shift, 1)
            shift *= 2
        pooled.append(acc[0:gd, LANES:] / jnp.minimum(pos1, float(w)))
        acc = acc[gd:]
    d = (jnp.concatenate(pooled, axis=0) - u).astype(BF16)
    ya = _dot(wp_ref[...], d) * ps_ref[...]
    mam_ref[0, 0:pool_w, :] = (ya * _silu(ga)).astype(BF16)

    gbT_ref[0] = proj_t(offs["gb"], fox_w)

    fcum = (cs[0:BF16_ROWS] + cs[BF16_ROWS:2 * BF16_ROWS] + cs[2 * BF16_ROWS:3 * BF16_ROWS]
            + fcarry_ref[:, LANES - 1:LANES])
    fcarry_ref[...] = fcum[:, ts - LANES:]
    f2 = fcum * LOG2E

    arow = lax.broadcasted_iota(jnp.int32, (AUG_ROWS, ts), 0)

    def split3(v):
        v_hi = v.astype(BF16).astype(F32)
        r = v - v_hi
        v_mid = r.astype(BF16).astype(F32)
        return v_hi, v_mid, (r - v_mid).astype(BF16).astype(F32)

    def rows(*vals):
        out = jnp.zeros((AUG_ROWS, ts), F32)
        for r, v in enumerate(vals):
            out = jnp.where(arow == r, v, out)
        return out

    zero_rows = jnp.zeros((QK_ROWS - HEAD_DIM - AUG_ROWS, ts), F32)
    q_gain = gq_ref[...] * (SCALE * LOG2E)
    for hd in range(n_fox):
        qh = qT[hd * HEAD_DIM:(hd + 1) * HEAD_DIM]
        qn = qh * (_head_rms_scale(qh) * q_gain)
        f3 = split3(f2[hd:hd + 1])
        qT_ref[0, hd, 0:HEAD_DIM, :] = qn.astype(BF16)
        qT_ref[0, hd, HEAD_DIM:HEAD_DIM + AUG_ROWS, :] = rows(
            f3[0], f3[1], f3[2], 1.0, 1.0, 1.0, -mb_ref[0]).astype(BF16)
        qT_ref[0, hd, HEAD_DIM + AUG_ROWS:, :] = zero_rows.astype(BF16)

        kh = kT[hd * HEAD_DIM:(hd + 1) * HEAD_DIM]
        kn = kh * (_head_rms_scale(kh) * gk_ref[...])
        k_aug = rows(1.0, 1.0, 1.0, -f3[0], -f3[1], -f3[2], 1.0)
        kfull = jnp.concatenate([kn, k_aug, zero_rows], axis=0)
        kp_ref[0, hd] = kfull.T.astype(BF16)

    for hd in range(n_fox):
        va = vT[hd * HEAD_DIM:(hd + 1) * HEAD_DIM].astype(BF16)
        for c in range(ts // tk):
            vT_ref[0, hd, c] = va[:, c * tk:(c + 1) * tk]


def _fox_out_kernel(fast_ref, qT_ref, kp_ref, vT_ref, gbT_ref, mam_ref, x_ref, wo_ref, o_ref,
                    acc_ref, den_ref, *, pool_w):
    hg = qT_ref.shape[1]
    tq = qT_ref.shape[3]
    tk = vT_ref.shape[4]
    n_diag = tq // tk
    i = pl.program_id(1)

    acc_ref[...] = jnp.zeros_like(acc_ref)
    den_ref[...] = jnp.zeros_like(den_ref)
    causal = (lax.broadcasted_iota(jnp.int32, (tk, tq), 0)
              <= lax.broadcasted_iota(jnp.int32, (tk, tq), 1))

    def head_scores(hd, j, lo=0):
        start = pl.multiple_of(j * tk, tk)
        return _dot(kp_ref[0, hd, pl.ds(start, tk), :], qT_ref[0, hd, :, lo:])

    def mask(s, diag, lo):
        return s if diag is None else jnp.where(causal[:, :tq - lo], s, MASK_VALUE)

    def fast_tiles(tiles):
        units = [(j, diag, hd) for (j, diag) in tiles for hd in range(hg)]
        scores = {}

        def issue(u):
            j, diag, hd = units[u]
            scores[u] = head_scores(hd, j, 0 if diag is None else diag * tk)

        for u in range(min(FAST_LOOKAHEAD, len(units))):
            issue(u)
        for u, (j, diag, hd) in enumerate(units):
            lo = 0 if diag is None else diag * tk
            p = jnp.exp2(mask(scores.pop(u), diag, lo))
            den_ref[hd, :, lo:] += jnp.sum(p, axis=0, keepdims=True)
            acc_ref[hd, :, lo:] += _dot(vT_ref[0, hd, j], p.astype(BF16))
            if u + FAST_LOOKAHEAD < len(units):
                issue(u + FAST_LOOKAHEAD)

    def safe_step(j, ms, diag=None):
        lo = 0 if diag is None else diag * tk
        scores = [head_scores(hd, j, lo) for hd in range(hg)]
        out = []
        for hd in range(hg):
            s = mask(scores[hd], diag, lo)
            m_old = ms[hd][:, lo:]
            m_new = jnp.maximum(m_old, jnp.max(s, axis=0, keepdims=True))
            p = jnp.exp2(s - m_new)
            alpha = jnp.exp2(m_old - m_new)
            den_ref[hd, :, lo:] = alpha * den_ref[hd, :, lo:] + jnp.sum(p, axis=0, keepdims=True)
            acc_ref[hd, :, lo:] = alpha * acc_ref[hd, :, lo:] + _dot(vT_ref[0, hd, j], p.astype(BF16))
            out.append(m_new if lo == 0 else jnp.concatenate([ms[hd][:, :lo], m_new], axis=1))
        return tuple(out)

    def finish():
        mam = mam_ref[0]
        fox_w = hg * HEAD_DIM
        part = (_dot_tn(mam[0:pool_w], wo_ref[0:pool_w, :])
                + _dot_tn(mam[pool_w:], wo_ref[pool_w + fox_w:, :]))
        ys = []
        for hd in range(hg):
            ys.append(acc_ref[hd] / den_ref[hd, 0:1, :])
        yb = (jnp.concatenate(ys, axis=0) * _silu(gbT_ref[0])).astype(BF16)
        o_ref[0] = x_ref[0] + part + _dot_tn(yb, wo_ref[pool_w:pool_w + fox_w, :])

    @pl.when(fast_ref[0] != 0)
    def _():
        def below_diagonal(first, count):
            fast_tiles([(first + d, None) for d in range(count)])

        def body(c, carry):
            below_diagonal(c * FAST_TILES_PER_TRIP, FAST_TILES_PER_TRIP)
            return carry

        n_below = i * n_diag
        n_trips = n_below // FAST_TILES_PER_TRIP
        lax.fori_loop(0, n_trips, body, 0)
        for rest in range(n_diag, FAST_TILES_PER_TRIP, n_diag):
            @pl.when(n_below - n_trips * FAST_TILES_PER_TRIP == rest)
            def _():
                below_diagonal(n_trips * FAST_TILES_PER_TRIP, rest)
        fast_tiles([(n_below + c, c) for c in range(n_diag)])
        finish()

    @pl.when(fast_ref[0] == 0)
    def _():
        ms = tuple(jnp.full((1, tq), MASK_VALUE, F32) for _ in range(hg))
        ms = lax.fori_loop(0, i * n_diag, lambda j, c: safe_step(j, c), ms)
        for c in range(n_diag):
            ms = safe_step(i * n_diag + c, ms, diag=c)
        finish()


def _col(v, rows=None):
    v = v.astype(F32)
    if rows is not None and rows > v.shape[0]:
        v = jnp.concatenate([v, jnp.zeros((rows - v.shape[0],), F32)])
    return v[:, None]


def _params(sem):
    return pltpu.CompilerParams(dimension_semantics=sem, vmem_limit_bytes=VMEM_LIMIT_BYTES)


def _layer(x, mem, norm_g, w_in, b_f, w_pool, pool_scale, fox_q_g, fox_k_g,
           mem_norm_g, w_mem_kv, mem_q_g, mem_k_g, w_out):
    bsz, seq, dm = x.shape
    n_mem = mem.shape[1]
    n_fox = b_f.shape[0]
    pool_w = pool_scale.shape[0]
    fox_w = n_fox * HEAD_DIM
    mem_w = MEM_HEADS * HEAD_DIM
    assert n_fox <= BF16_ROWS and pool_w == len(POOL_WINDOWS) * POOL_GROUP_DIM
    assert w_in.shape[1] == 2 * pool_w + 4 * fox_w + n_fox + 2 * mem_w
    assert w_out.shape[0] == pool_w + fox_w + mem_w
    ts, tq, tk = SEQ_TILE, Q_TILE, KV_TILE
    assert seq % ts == 0 and seq % tq == 0 and tq % tk == 0 and ts % tk == 0
    assert FAST_TILES_PER_TRIP % (tq // tk) == 0

    sizes = (pool_w, pool_w, fox_w, fox_w, fox_w, n_fox, fox_w, mem_w, mem_w)
    w_in = w_in.astype(BF16)
    parts, off = [], 0
    for w in sizes:
        parts.append(w_in[:, off:off + w])
        off += w
    w_ua, w_ga, w_q, w_k, w_v, w_f, w_gb, w_qm, w_gm = parts
    w_f = jnp.concatenate([w_f, jnp.zeros((dm, BF16_ROWS - n_fox), w_in.dtype)], axis=1)
    order = [("ua", w_ua), ("ga", w_ga), ("f", w_f), ("q", w_q), ("k", w_k), ("v", w_v),
             ("gb", w_gb), ("qm", w_qm), ("gm", w_gm)]
    offs, off = {}, 0
    for name, w in order:
        offs[name] = off
        off += w.shape[1]
    wT = jnp.concatenate([w for _, w in order], axis=1).T
    n_rows = wT.shape[0]
    tri = (lax.broadcasted_iota(jnp.int32, (ts, ts), 0)
           <= lax.broadcasted_iota(jnp.int32, (ts, ts), 1)).astype(BF16)
    wp_bd = jax.scipy.linalg.block_diag(*[w_pool[g].T for g in range(len(POOL_WINDOWS))]).astype(BF16)
    wkvT = w_mem_kv.T.astype(BF16)
    wo = w_out.astype(BF16)

    nb = math.gcd(MEM_BATCH_ROWS_PER_STEP, bsz)
    km, vm = pl.pallas_call(
        _mem_kv_kernel,
        grid=(bsz // nb,),
        in_specs=[
            pl.BlockSpec((nb, n_mem, dm), lambda b: (b, 0, 0)),
            pl.BlockSpec((1, dm), lambda b: (0, 0)),
            pl.BlockSpec((2 * mem_w, dm), lambda b: (0, 0)),
            pl.BlockSpec((HEAD_DIM, 1), lambda b: (0, 0)),
        ],
        out_specs=[
            pl.BlockSpec((nb, MEM_HEADS, n_mem, QK_ROWS), lambda b: (b, 0, 0, 0)),
            pl.BlockSpec((nb, MEM_HEADS, MEM_V_ROWS, n_mem), lambda b: (b, 0, 0, 0)),
        ],
        out_shape=[
            jax.ShapeDtypeStruct((bsz, MEM_HEADS, n_mem, QK_ROWS), BF16),
            jax.ShapeDtypeStruct((bsz, MEM_HEADS, MEM_V_ROWS, n_mem), BF16),
        ],
        compiler_params=_params(("arbitrary",)),
        name="mem_kv",
    )(mem, mem_norm_g[None, :], wkvT, _col(mem_k_g))

    const2 = lambda b, s: (0, 0)
    m_bound = (HEAD_DIM * SCALE * LOG2E * BOUND_SLACK
               * jnp.max(jnp.abs(fox_q_g)) * jnp.max(jnp.abs(fox_k_g))).astype(F32).reshape(1)
    fast = (m_bound < FAST_MAX_SHIFT).astype(jnp.int32)
    qT, kp, vT, gbT, mam = pl.pallas_call(
        functools.partial(_proj_kernel, offs=offs, n_fox=n_fox, pool_w=pool_w,
                          fox_w=fox_w, mem_w=mem_w),
        grid=(bsz, seq // ts),
        in_specs=[
            pl.BlockSpec(memory_space=pltpu.SMEM),
            pl.BlockSpec((1, ts, dm), lambda b, s: (b, s, 0)),
            pl.BlockSpec((1, dm), const2),
            pl.BlockSpec((n_rows, dm), const2),
            pl.BlockSpec((BF16_ROWS, 1), const2),
            pl.BlockSpec((ts, ts), const2),
            pl.BlockSpec((HEAD_DIM, 1), const2),
            pl.BlockSpec((HEAD_DIM, 1), const2),
            pl.BlockSpec((pool_w, pool_w), const2),
            pl.BlockSpec((pool_w, 1), const2),
            pl.BlockSpec((1, MEM_HEADS, n_mem, QK_ROWS), lambda b, s: (b, 0, 0, 0)),
            pl.BlockSpec((1, MEM_HEADS, MEM_V_ROWS, n_mem), lambda b, s: (b, 0, 0, 0)),
            pl.BlockSpec((HEAD_DIM, 1), const2),
        ],
        out_specs=[
            pl.BlockSpec((1, n_fox, QK_ROWS, ts), lambda b, s: (b, 0, 0, s)),
            pl.BlockSpec((1, n_fox, ts, QK_ROWS), lambda b, s: (b, 0, s, 0)),
            pl.BlockSpec((1, n_fox, ts // tk, HEAD_DIM, tk), lambda b, s: (b, 0, s, 0, 0)),
            pl.BlockSpec((1, fox_w, ts), lambda b, s: (b, 0, s)),
            pl.BlockSpec((1, pool_w + mem_w, ts), lambda b, s: (b, 0, s)),
        ],
        out_shape=[
            jax.ShapeDtypeStruct((bsz, n_fox, QK_ROWS, seq), BF16),
            jax.ShapeDtypeStruct((bsz, n_fox, seq, QK_ROWS), BF16),
            jax.ShapeDtypeStruct((bsz, n_fox, seq // tk, HEAD_DIM, tk), BF16),
            jax.ShapeDtypeStruct((bsz, fox_w, seq), F32),
            jax.ShapeDtypeStruct((bsz, pool_w + mem_w, seq), BF16),
        ],
        scratch_shapes=[
            pltpu.VMEM((BF16_ROWS, LANES), F32),
            pltpu.VMEM((pool_w, LANES), F32),
        ],
        compiler_params=_params(("arbitrary", "arbitrary")),
        name="proj",
    )(m_bound, x, norm_g[None, :], wT, _col(b_f, BF16_ROWS), tri, _col(fox_q_g), _col(fox_k_g),
      wp_bd, _col(pool_scale), km, vm, _col(mem_q_g))

    return pl.pallas_call(
        functools.partial(_fox_out_kernel, pool_w=pool_w),
        grid=(bsz, seq // tq),
        in_specs=[
            pl.BlockSpec(memory_space=pltpu.SMEM),
            pl.BlockSpec((1, n_fox, QK_ROWS, tq), lambda b, i: (b, 0, 0, i)),
            pl.BlockSpec((1, n_fox, seq, QK_ROWS), lambda b, i: (b, 0, 0, 0)),
            pl.BlockSpec((1, n_fox, seq // tk, HEAD_DIM, tk), lambda b, i: (b, 0, 0, 0, 0)),
            pl.BlockSpec((1, fox_w, tq), lambda b, i: (b, 0, i)),
            pl.BlockSpec((1, pool_w + mem_w, tq), lambda b, i: (b, 0, i)),
            pl.BlockSpec((1, tq, dm), lambda b, i: (b, i, 0)),
            pl.BlockSpec((pool_w + fox_w + mem_w, dm), const2),
        ],
        out_specs=pl.BlockSpec((1, tq, dm), lambda b, i: (b, i, 0)),
        out_shape=jax.ShapeDtypeStruct((bsz, seq, dm), x.dtype),
        scratch_shapes=[pltpu.VMEM((n_fox, HEAD_DIM, tq), F32), pltpu.VMEM((n_fox, SUBLANES, tq), F32)],
        compiler_params=_params(("arbitrary", "arbitrary")),
        name="fox_out",
    )(fast, qT, kp, vT, gbT, mam, x, wo)


def kernel(x, mem, norm_g, w_in, b_f, w_pool, pool_scale, fox_q_g, fox_k_g, mem_norm_g,
           w_mem_kv, mem_q_g, mem_k_g, w_out):
    for l in range(norm_g.shape[0]):
        x = _layer(x, mem, norm_g[l], w_in[l], b_f[l], w_pool[l], pool_scale[l], fox_q_g[l],
                   fox_k_g[l], mem_norm_g[l], w_mem_kv[l], mem_q_g[l], mem_k_g[l], w_out[l])
    return x
```

```python
import functools
import math

import jax
import jax.numpy as jnp
from jax import lax
from jax.experimental import pallas as pl
from jax.experimental.pallas import tpu as pltpu

F32 = jnp.float32
BF16 = jnp.bfloat16

HEAD_DIM = 64
EPS = 1e-6
POOL_WINDOWS = (2, 4, 8, 16)
POOL_GROUP_DIM = 64
MEM_HEADS = 4
SCALE = 1.0 / math.sqrt(HEAD_DIM)
LOG2E = math.log2(math.e)
MASK_VALUE = -1e30

LANES = 128
SUBLANES = 8
BF16_ROWS = 16
QK_ROWS = 128
AUG_ROWS = BF16_ROWS
MEM_V_ROWS = HEAD_DIM + BF16_ROWS

MEM_BATCH_ROWS_PER_STEP = 4
SEQ_TILE = 1024
Q_TILE = 512
KV_TILE = 256
FAST_TILES_PER_TRIP = 8
FAST_LOOKAHEAD = 3
FAST_MAX_SHIFT = 50.0
BOUND_SLACK = 1.01
VMEM_LIMIT_BYTES = 56 * 1024 * 1024

NT_DIMS = (((1,), (1,)), ((), ()))
TN_DIMS = (((0,), (0,)), ((), ()))


def _dot(a, b):
    return jnp.dot(a, b, preferred_element_type=F32)


def _dot_nt(a, b):
    return lax.dot_general(a, b, NT_DIMS, preferred_element_type=F32)


def _dot_tn(a, b):
    return lax.dot_general(a, b, TN_DIMS, preferred_element_type=F32)


def _silu(g):
    return g * jax.nn.sigmoid(g)


def _log_sigmoid(z):
    return jnp.minimum(z, 0.0) - jnp.log1p(jnp.exp(-jnp.abs(z)))


def _head_rms_scale(t):
    return lax.rsqrt(jnp.mean(t * t, axis=0, keepdims=True) + EPS)


def _ones_row_block(n):
    row = lax.broadcasted_iota(jnp.int32, (BF16_ROWS, n), 0)
    return (row == 0).astype(F32)


def _mem_kv_kernel(mem_ref, g_ref, wkvT_ref, gk_ref, km_ref, vm_ref):
    nb, n_mem, dm = mem_ref.shape
    x = mem_ref[...].reshape(nb * n_mem, dm)
    h = (x * lax.rsqrt(jnp.mean(x * x, axis=-1, keepdims=True) + EPS) * g_ref[...]).astype(BF16)
    kvT = _dot_nt(wkvT_ref[...], h)
    width = MEM_HEADS * HEAD_DIM
    ones_blk = _ones_row_block(n_mem)
    zero_pad = jnp.zeros((QK_ROWS - HEAD_DIM, n_mem), F32)
    for hm in range(MEM_HEADS):
        kh = kvT[hm * HEAD_DIM:(hm + 1) * HEAD_DIM]
        kn = kh * _head_rms_scale(kh) * gk_ref[...]
        vh = kvT[width + hm * HEAD_DIM:width + (hm + 1) * HEAD_DIM]
        for bi in range(nb):
            cols = slice(bi * n_mem, (bi + 1) * n_mem)
            km_ref[bi, hm] = jnp.concatenate([kn[:, cols], zero_pad], axis=0).T.astype(BF16)
            vm_ref[bi, hm] = jnp.concatenate([vh[:, cols], ones_blk], axis=0).astype(BF16)


def _proj_kernel(mb_ref, x_ref, ng_ref, wT_ref, bf_ref, tri_ref, gq_ref, gk_ref, wp_ref, ps_ref,
                 km_ref, vm_ref, gmq_ref,
                 qT_ref, kp_ref, vT_ref, gbT_ref, mam_ref,
                 fcarry_ref, halo_ref, *, offs, n_fox, pool_w, fox_w, mem_w):
    ts = x_ref.shape[1]
    s_idx = pl.program_id(1)

    @pl.when(s_idx == 0)
    def _():
        fcarry_ref[...] = jnp.zeros_like(fcarry_ref)
        halo_ref[...] = jnp.zeros_like(halo_ref)

    x = x_ref[0]
    h = (x * lax.rsqrt(jnp.mean(x * x, axis=-1, keepdims=True) + EPS) * ng_ref[...]).astype(BF16)

    def proj_t(lo, n):
        return _dot_nt(wT_ref[lo:lo + n, :], h)

    tk = vT_ref.shape[4]
    gd = POOL_GROUP_DIM

    sec_m = proj_t(offs["qm"], 2 * mem_w)
    qm = sec_m[0:mem_w]
    gm = sec_m[mem_w:2 * mem_w]
    sec_a = proj_t(offs["ua"], 2 * pool_w + BF16_ROWS)
    u = sec_a[0:pool_w]
    ga = sec_a[pool_w:2 * pool_w]
    z = sec_a[2 * pool_w:2 * pool_w + BF16_ROWS] + bf_ref[...]

    zero_pad = jnp.zeros((QK_ROWS - HEAD_DIM, ts), F32)
    mq_gain = gmq_ref[...] * (SCALE * LOG2E)
    lgs = []
    for hm in range(MEM_HEADS):
        qh = qm[hm * HEAD_DIM:(hm + 1) * HEAD_DIM]
        qn = qh * (_head_rms_scale(qh) * mq_gain)
        qpad = jnp.concatenate([qn, zero_pad], axis=0).astype(BF16)
        lgs.append(_dot(km_ref[0, hm], qpad))

    qT = proj_t(offs["q"], fox_w)

    logf = _log_sigmoid(z)
    hi = logf.astype(BF16)
    r1 = logf - hi.astype(F32)
    mid = r1.astype(BF16)
    lo = (r1 - mid.astype(F32)).astype(BF16)
    cs = _dot(jnp.concatenate([hi, mid, lo], axis=0), tri_ref[...])

    kT = proj_t(offs["k"], fox_w)
    vT = proj_t(offs["v"], fox_w)

    for hm in range(MEM_HEADS):
        lg = lgs[hm]
        p = jnp.exp2(lg - jnp.max(lg, axis=0, keepdims=True)).astype(BF16)
        yv = _dot(vm_ref[0, hm], p)
        y = yv[0:HEAD_DIM] / yv[HEAD_DIM:HEAD_DIM + 1]
        mm = y * _silu(gm[hm * HEAD_DIM:(hm + 1) * HEAD_DIM])
        mam_ref[0, pool_w + hm * HEAD_DIM:pool_w + (hm + 1) * HEAD_DIM, :] = mm.astype(BF16)

    uext = jnp.concatenate([halo_ref[...], u], axis=1)
    halo_ref[...] = u[:, ts - LANES:]
    pos1 = (s_idx * ts + lax.broadcasted_iota(jnp.int32, (1, ts), 1) + 1).astype(F32)
    acc = uext
    pooled = []
    shift = 1
    for g, w in enumerate(POOL_WINDOWS):
        while shift < w:
            acc = acc + pltpu.roll(acc, shift, 1)
            shift *= 2
        pooled.append(acc[0:gd, LANES:] / jnp.minimum(pos1, float(w)))
        acc = acc[gd:]
    d = (jnp.concatenate(pooled, axis=0) - u).astype(BF16)
    ya = _dot(wp_ref[...], d) * ps_ref[...]
    mam_ref[0, 0:pool_w, :] = (ya * _silu(ga)).astype(BF16)

    gbT_ref[0] = proj_t(offs["gb"], fox_w)

    fcum = (cs[0:BF16_ROWS] + cs[BF16_ROWS:2 * BF16_ROWS] + cs[2 * BF16_ROWS:3 * BF16_ROWS]
            + fcarry_ref[:, LANES - 1:LANES])
    fcarry_ref[...] = fcum[:, ts - LANES:]
    f2 = fcum * LOG2E

    arow = lax.broadcasted_iota(jnp.int32, (AUG_ROWS, ts), 0)

    def split3(v):
        v_hi = v.astype(BF16).astype(F32)
        r = v - v_hi
        v_mid = r.astype(BF16).astype(F32)
        return v_hi, v_mid, (r - v_mid).astype(BF16).astype(F32)

    def rows(*vals):
        out = jnp.zeros((AUG_ROWS, ts), F32)
        for r, v in enumerate(vals):
            out = jnp.where(arow == r, v, out)
        return out

    zero_rows = jnp.zeros((QK_ROWS - HEAD_DIM - AUG_ROWS, ts), F32)
    q_gain = gq_ref[...] * (SCALE * LOG2E)
    for hd in range(n_fox):
        qh = qT[hd * HEAD_DIM:(hd + 1) * HEAD_DIM]
        qn = qh * (_head_rms_scale(qh) * q_gain)
        f3 = split3(f2[hd:hd + 1])
        qT_ref[0, hd, 0:HEAD_DIM, :] = qn.astype(BF16)
        qT_ref[0, hd, HEAD_DIM:HEAD_DIM + AUG_ROWS, :] = rows(
            f3[0], f3[1], f3[2], 1.0, 1.0, 1.0, -mb_ref[0]).astype(BF16)
        qT_ref[0, hd, HEAD_DIM + AUG_ROWS:, :] = zero_rows.astype(BF16)

        kh = kT[hd * HEAD_DIM:(hd + 1) * HEAD_DIM]
        kn = kh * (_head_rms_scale(kh) * gk_ref[...])
        k_aug = rows(1.0, 1.0, 1.0, -f3[0], -f3[1], -f3[2], 1.0)
        kfull = jnp.concatenate([kn, k_aug, zero_rows], axis=0)
        kp_ref[0, hd] = kfull.T.astype(BF16)

    for hd in range(n_fox):
        va = vT[hd * HEAD_DIM:(hd + 1) * HEAD_DIM].astype(BF16)
        for c in range(ts // tk):
            vT_ref[0, hd, c] = va[:, c * tk:(c + 1) * tk]


def _fox_out_kernel(fast_ref, qT_ref, kp_ref, vT_ref, gbT_ref, mam_ref, x_ref, wo_ref, o_ref,
                    acc_ref, den_ref, *, pool_w):
    hg = qT_ref.shape[1]
    tq = qT_ref.shape[3]
    tk = vT_ref.shape[4]
    n_diag = tq // tk
    i = pl.program_id(1)

    acc_ref[...] = jnp.zeros_like(acc_ref)
    den_ref[...] = jnp.zeros_like(den_ref)
    causal = (lax.broadcasted_iota(jnp.int32, (tk, tq), 0)
              <= lax.broadcasted_iota(jnp.int32, (tk, tq), 1))

    def head_scores(hd, j, lo=0):
        start = pl.multiple_of(j * tk, tk)
        return _dot(kp_ref[0, hd, pl.ds(start, tk), :], qT_ref[0, hd, :, lo:])

    def mask(s, diag, lo):
        return s if diag is None else jnp.where(causal[:, :tq - lo], s, MASK_VALUE)

    def fast_tiles(tiles):
        units = [(j, diag, hd) for (j, diag) in tiles for hd in range(hg)]
        scores = {}

        def issue(u):
            j, diag, hd = units[u]
            scores[u] = head_scores(hd, j, 0 if diag is None else diag * tk)

        for u in range(min(FAST_LOOKAHEAD, len(units))):
            issue(u)
        for u, (j, diag, hd) in enumerate(units):
            lo = 0 if diag is None else diag * tk
            p = jnp.exp2(mask(scores.pop(u), diag, lo))
            den_ref[hd, :, lo:] += jnp.sum(p, axis=0, keepdims=True)
            acc_ref[hd, :, lo:] += _dot(vT_ref[0, hd, j], p.astype(BF16))
            if u + FAST_LOOKAHEAD < len(units):
                issue(u + FAST_LOOKAHEAD)

    def safe_step(j, ms, diag=None):
        lo = 0 if diag is None else diag * tk
        scores = [head_scores(hd, j, lo) for hd in range(hg)]
        out = []
        for hd in range(hg):
            s = mask(scores[hd], diag, lo)
            m_old = ms[hd][:, lo:]
            m_new = jnp.maximum(m_old, jnp.max(s, axis=0, keepdims=True))
            p = jnp.exp2(s - m_new)
            alpha = jnp.exp2(m_old - m_new)
            den_ref[hd, :, lo:] = alpha * den_ref[hd, :, lo:] + jnp.sum(p, axis=0, keepdims=True)
            acc_ref[hd, :, lo:] = alpha * acc_ref[hd, :, lo:] + _dot(vT_ref[0, hd, j], p.astype(BF16))
            out.append(m_new if lo == 0 else jnp.concatenate([ms[hd][:, :lo], m_new], axis=1))
        return tuple(out)

    def finish():
        mam = mam_ref[0]
        fox_w = hg * HEAD_DIM
        part = (_dot_tn(mam[0:pool_w], wo_ref[0:pool_w, :])
                + _dot_tn(mam[pool_w:], wo_ref[pool_w + fox_w:, :]))
        ys = []
        for hd in range(hg):
            ys.append(acc_ref[hd] / den_ref[hd, 0:1, :])
        yb = (jnp.concatenate(ys, axis=0) * _silu(gbT_ref[0])).astype(BF16)
        o_ref[0] = x_ref[0] + part + _dot_tn(yb, wo_ref[pool_w:pool_w + fox_w, :])

    @pl.when(fast_ref[0] != 0)
    def _():
        def below_diagonal(first, count):
            fast_tiles([(first + d, None) for d in range(count)])

        def body(c, carry):
            below_diagonal(c * FAST_TILES_PER_TRIP, FAST_TILES_PER_TRIP)
            return carry

        n_below = i * n_diag
        n_trips = n_below // FAST_TILES_PER_TRIP
        lax.fori_loop(0, n_trips, body, 0)
        for rest in range(n_diag, FAST_TILES_PER_TRIP, n_diag):
            @pl.when(n_below - n_trips * FAST_TILES_PER_TRIP == rest)
            def _():
                below_diagonal(n_trips * FAST_TILES_PER_TRIP, rest)
        fast_tiles([(n_below + c, c) for c in range(n_diag)])
        finish()

    @pl.when(fast_ref[0] == 0)
    def _():
        ms = tuple(jnp.full((1, tq), MASK_VALUE, F32) for _ in range(hg))
        ms = lax.fori_loop(0, i * n_diag, lambda j, c: safe_step(j, c), ms)
        for c in range(n_diag):
            ms = safe_step(i * n_diag + c, ms, diag=c)
        finish()


def _col(v, rows=None):
    v = v.astype(F32)
    if rows is not None and rows > v.shape[0]:
        v = jnp.concatenate([v, jnp.zeros((rows - v.shape[0],), F32)])
    return v[:, None]


def _params(sem):
    return pltpu.CompilerParams(dimension_semantics=sem, vmem_limit_bytes=VMEM_LIMIT_BYTES)


def _layer(x, mem, norm_g, w_in, b_f, w_pool, pool_scale, fox_q_g, fox_k_g,
           mem_norm_g, w_mem_kv, mem_q_g, mem_k_g, w_out):
    bsz, seq, dm = x.shape
    n_mem = mem.shape[1]
    n_fox = b_f.shape[0]
    pool_w = pool_scale.shape[0]
    fox_w = n_fox * HEAD_DIM
    mem_w = MEM_HEADS * HEAD_DIM
    assert n_fox <= BF16_ROWS and pool_w == len(POOL_WINDOWS) * POOL_GROUP_DIM
    assert w_in.shape[1] == 2 * pool_w + 4 * fox_w + n_fox + 2 * mem_w
    assert w_out.shape[0] == pool_w + fox_w + mem_w
    ts, tq, tk = SEQ_TILE, Q_TILE, KV_TILE
    assert seq % ts == 0 and seq % tq == 0 and tq % tk == 0 and ts % tk == 0
    assert FAST_TILES_PER_TRIP % (tq // tk) == 0

    names = ("ua", "ga", "q", "k", "v", "f", "gb", "qm", "gm")
    sizes = (pool_w, pool_w, fox_w, fox_w, fox_w, n_fox, fox_w, mem_w, mem_w)
    w_inT = w_in.T.astype(BF16)
    src, off = {}, 0
    for name, w in zip(names, sizes):
        src[name] = w_inT[off:off + w]
        off += w
    src["f"] = jnp.concatenate([src["f"], jnp.zeros((BF16_ROWS - n_fox, dm), BF16)], axis=0)
    order = ("ua", "ga", "f", "q", "k", "v", "gb", "qm", "gm")
    offs, off = {}, 0
    for name in order:
        offs[name] = off
        off += src[name].shape[0]
    wT = jnp.concatenate([src[name] for name in order], axis=0)
    n_rows = wT.shape[0]
    tri = (lax.broadcasted_iota(jnp.int32, (ts, ts), 0)
           <= lax.broadcasted_iota(jnp.int32, (ts, ts), 1)).astype(BF16)
    wp_bd = jax.scipy.linalg.block_diag(*[w_pool[g].T for g in range(len(POOL_WINDOWS))]).astype(BF16)
    wkvT = w_mem_kv.T.astype(BF16)
    wo = w_out.astype(BF16)

    nb = math.gcd(MEM_BATCH_ROWS_PER_STEP, bsz)
    km, vm = pl.pallas_call(
        _mem_kv_kernel,
        grid=(bsz // nb,),
        in_specs=[
            pl.BlockSpec((nb, n_mem, dm), lambda b: (b, 0, 0)),
            pl.BlockSpec((1, dm), lambda b: (0, 0)),
            pl.BlockSpec((2 * mem_w, dm), lambda b: (0, 0)),
            pl.BlockSpec((HEAD_DIM, 1), lambda b: (0, 0)),
        ],
        out_specs=[
            pl.BlockSpec((nb, MEM_HEADS, n_mem, QK_ROWS), lambda b: (b, 0, 0, 0)),
            pl.BlockSpec((nb, MEM_HEADS, MEM_V_ROWS, n_mem), lambda b: (b, 0, 0, 0)),
        ],
        out_shape=[
            jax.ShapeDtypeStruct((bsz, MEM_HEADS, n_mem, QK_ROWS), BF16),
            jax.ShapeDtypeStruct((bsz, MEM_HEADS, MEM_V_ROWS, n_mem), BF16),
        ],
        compiler_params=_params(("arbitrary",)),
        name="mem_kv",
    )(mem, mem_norm_g[None, :], wkvT, _col(mem_k_g))

    const2 = lambda b, s: (0, 0)
    m_bound = (HEAD_DIM * SCALE * LOG2E * BOUND_SLACK
               * jnp.max(jnp.abs(fox_q_g)) * jnp.max(jnp.abs(fox_k_g))).astype(F32).reshape(1)
    fast = (m_bound < FAST_MAX_SHIFT).astype(jnp.int32)
    qT, kp, vT, gbT, mam = pl.pallas_call(
        functools.partial(_proj_kernel, offs=offs, n_fox=n_fox, pool_w=pool_w,
                          fox_w=fox_w, mem_w=mem_w),
        grid=(bsz, seq // ts),
        in_specs=[
            pl.BlockSpec(memory_space=pltpu.SMEM),
            pl.BlockSpec((1, ts, dm), lambda b, s: (b, s, 0)),
            pl.BlockSpec((1, dm), const2),
            pl.BlockSpec((n_rows, dm), const2),
            pl.BlockSpec((BF16_ROWS, 1), const2),
            pl.BlockSpec((ts, ts), const2),
            pl.BlockSpec((HEAD_DIM, 1), const2),
            pl.BlockSpec((HEAD_DIM, 1), const2),
            pl.BlockSpec((pool_w, pool_w), const2),
            pl.BlockSpec((pool_w, 1), const2),
            pl.BlockSpec((1, MEM_HEADS, n_mem, QK_ROWS), lambda b, s: (b, 0, 0, 0)),
            pl.BlockSpec((1, MEM_HEADS, MEM_V_ROWS, n_mem), lambda b, s: (b, 0, 0, 0)),
            pl.BlockSpec((HEAD_DIM, 1), const2),
        ],
        out_specs=[
            pl.BlockSpec((1, n_fox, QK_ROWS, ts), lambda b, s: (b, 0, 0, s)),
            pl.BlockSpec((1, n_fox, ts, QK_ROWS), lambda b, s: (b, 0, s, 0)),
            pl.BlockSpec((1, n_fox, ts // tk, HEAD_DIM, tk), lambda b, s: (b, 0, s, 0, 0)),
            pl.BlockSpec((1, fox_w, ts), lambda b, s: (b, 0, s)),
            pl.BlockSpec((1, pool_w + mem_w, ts), lambda b, s: (b, 0, s)),
        ],
        out_shape=[
            jax.ShapeDtypeStruct((bsz, n_fox, QK_ROWS, seq), BF16),
            jax.ShapeDtypeStruct((bsz, n_fox, seq, QK_ROWS), BF16),
            jax.ShapeDtypeStruct((bsz, n_fox, seq // tk, HEAD_DIM, tk), BF16),
            jax.ShapeDtypeStruct((bsz, fox_w, seq), F32),
            jax.ShapeDtypeStruct((bsz, pool_w + mem_w, seq), BF16),
        ],
        scratch_shapes=[
            pltpu.VMEM((BF16_ROWS, LANES), F32),
            pltpu.VMEM((pool_w, LANES), F32),
        ],
        compiler_params=_params(("arbitrary", "arbitrary")),
        name="proj",
    )(m_bound, x, norm_g[None, :], wT, _col(b_f, BF16_ROWS), tri, _col(fox_q_g), _col(fox_k_g),
      wp_bd, _col(pool_scale), km, vm, _col(mem_q_g))

    return pl.pallas_call(
        functools.partial(_fox_out_kernel, pool_w=pool_w),
        grid=(bsz, seq // tq),
        in_specs=[
            pl.BlockSpec(memory_space=pltpu.SMEM),
            pl.BlockSpec((1, n_fox, QK_ROWS, tq), lambda b, i: (b, 0, 0, i)),
            pl.BlockSpec((1, n_fox, seq, QK_ROWS), lambda b, i: (b, 0, 0, 0)),
            pl.BlockSpec((1, n_fox, seq // tk, HEAD_DIM, tk), lambda b, i: (b, 0, 0, 0, 0)),
            pl.BlockSpec((1, fox_w, tq), lambda b, i: (b, 0, i)),
            pl.BlockSpec((1, pool_w + mem_w, tq), lambda b, i: (b, 0, i)),
            pl.BlockSpec((1, tq, dm), lambda b, i: (b, i, 0)),
            pl.BlockSpec((pool_w + fox_w + mem_w, dm), const2),
        ],
        out_specs=pl.BlockSpec((1, tq, dm), lambda b, i: (b, i, 0)),
        out_shape=jax.ShapeDtypeStruct((bsz, seq, dm), x.dtype),
        scratch_shapes=[pltpu.VMEM((n_fox, HEAD_DIM, tq), F32), pltpu.VMEM((n_fox, SUBLANES, tq), F32)],
        compiler_params=_params(("arbitrary", "arbitrary")),
        name="fox_out",
    )(fast, qT, kp, vT, gbT, mam, x, wo)


def kernel(x, mem, norm_g, w_in, b_f, w_pool, pool_scale, fox_q_g, fox_k_g, mem_norm_g,
           w_mem_kv, mem_q_g, mem_k_g, w_out):
    for l in range(norm_g.shape[0]):
        x = _layer(x, mem, norm_g[l], w_in[l], b_f[l], w_pool[l], pool_scale[l], fox_q_g[l],
                   fox_k_g[l], mem_norm_g[l], w_mem_kv[l], mem_q_g[l], mem_k_g[l], w_out[l])
    return x
```

```python
import functools
import math

import jax
import jax.numpy as jnp
from jax import lax
from jax.experimental import pallas as pl
from jax.experimental.pallas import tpu as pltpu

F32 = jnp.float32
BF16 = jnp.bfloat16

HEAD_DIM = 64
EPS = 1e-6
POOL_WINDOWS = (2, 4, 8, 16)
POOL_GROUP_DIM = 64
MEM_HEADS = 4
SCALE = 1.0 / math.sqrt(HEAD_DIM)
LOG2E = math.log2(math.e)
MASK_VALUE = -1e30

LANES = 128
SUBLANES = 8
BF16_ROWS = 16
QK_ROWS = 128
AUG_ROWS = BF16_ROWS
MEM_V_ROWS = HEAD_DIM + BF16_ROWS

MEM_BATCH_ROWS_PER_STEP = 4
SEQ_TILE = 1024
Q_TILE = 512
KV_TILE = 256
FAST_TILES_PER_TRIP = 6
FAST_LOOKAHEAD = 3
FAST_MAX_SHIFT = 50.0
BOUND_SLACK = 1.01
VMEM_LIMIT_BYTES = 56 * 1024 * 1024

NT_DIMS = (((1,), (1,)), ((), ()))
TN_DIMS = (((0,), (0,)), ((), ()))


def _dot(a, b):
    return jnp.dot(a, b, preferred_element_type=F32)


def _dot_nt(a, b):
    return lax.dot_general(a, b, NT_DIMS, preferred_element_type=F32)


def _dot_tn(a, b):
    return lax.dot_general(a, b, TN_DIMS, preferred_element_type=F32)


def _silu(g):
    return g * jax.nn.sigmoid(g)


def _log_sigmoid(z):
    return jnp.minimum(z, 0.0) - jnp.log1p(jnp.exp(-jnp.abs(z)))


def _head_rms_scale(t):
    return lax.rsqrt(jnp.mean(t * t, axis=0, keepdims=True) + EPS)


def _ones_row_block(n):
    row = lax.broadcasted_iota(jnp.int32, (BF16_ROWS, n), 0)
    return (row == 0).astype(F32)


def _mem_kv_kernel(mem_ref, g_ref, wkvT_ref, gk_ref, km_ref, vm_ref):
    nb, n_mem, dm = mem_ref.shape
    x = mem_ref[...].reshape(nb * n_mem, dm)
    h = (x * lax.rsqrt(jnp.mean(x * x, axis=-1, keepdims=True) + EPS) * g_ref[...]).astype(BF16)
    kvT = _dot_nt(wkvT_ref[...], h)
    width = MEM_HEADS * HEAD_DIM
    ones_blk = _ones_row_block(n_mem)
    zero_pad = jnp.zeros((QK_ROWS - HEAD_DIM, n_mem), F32)
    for hm in range(MEM_HEADS):
        kh = kvT[hm * HEAD_DIM:(hm + 1) * HEAD_DIM]
        kn = kh * _head_rms_scale(kh) * gk_ref[...]
        vh = kvT[width + hm * HEAD_DIM:width + (hm + 1) * HEAD_DIM]
        for bi in range(nb):
            cols = slice(bi * n_mem, (bi + 1) * n_mem)
            km_ref[bi, hm] = jnp.concatenate([kn[:, cols], zero_pad], axis=0).T.astype(BF16)
            vm_ref[bi, hm] = jnp.concatenate([vh[:, cols], ones_blk], axis=0).astype(BF16)


def _proj_kernel(mb_ref, x_ref, ng_ref, wT_ref, bf_ref, tri_ref, gq_ref, gk_ref, wp_ref, ps_ref,
                 km_ref, vm_ref, gmq_ref,
                 qT_ref, kp_ref, vT_ref, gbT_ref, mam_ref,
                 fcarry_ref, halo_ref, *, offs, n_fox, pool_w, fox_w, mem_w):
    ts = x_ref.shape[1]
    s_idx = pl.program_id(1)

    @pl.when(s_idx == 0)
    def _():
        fcarry_ref[...] = jnp.zeros_like(fcarry_ref)
        halo_ref[...] = jnp.zeros_like(halo_ref)

    x = x_ref[0]
    h = (x * lax.rsqrt(jnp.mean(x * x, axis=-1, keepdims=True) + EPS) * ng_ref[...]).astype(BF16)

    def proj_t(lo, n):
        return _dot_nt(wT_ref[lo:lo + n, :], h)

    tk = vT_ref.shape[4]
    gd = POOL_GROUP_DIM

    sec_m = proj_t(offs["qm"], 2 * mem_w)
    qm = sec_m[0:mem_w]
    gm = sec_m[mem_w:2 * mem_w]
    sec_a = proj_t(offs["ua"], 2 * pool_w + BF16_ROWS)
    u = sec_a[0:pool_w]
    ga = sec_a[pool_w:2 * pool_w]
    z = sec_a[2 * pool_w:2 * pool_w + BF16_ROWS] + bf_ref[...]

    zero_pad = jnp.zeros((QK_ROWS - HEAD_DIM, ts), F32)
    mq_gain = gmq_ref[...] * (SCALE * LOG2E)
    lgs = []
    for hm in range(MEM_HEADS):
        qh = qm[hm * HEAD_DIM:(hm + 1) * HEAD_DIM]
        qn = qh * (_head_rms_scale(qh) * mq_gain)
        qpad = jnp.concatenate([qn, zero_pad], axis=0).astype(BF16)
        lgs.append(_dot(km_ref[0, hm], qpad))

    qT = proj_t(offs["q"], fox_w)

    logf = _log_sigmoid(z)
    hi = logf.astype(BF16)
    r1 = logf - hi.astype(F32)
    mid = r1.astype(BF16)
    lo = (r1 - mid.astype(F32)).astype(BF16)
    cs = _dot(jnp.concatenate([hi, mid, lo], axis=0), tri_ref[...])

    kT = proj_t(offs["k"], fox_w)
    vT = proj_t(offs["v"], fox_w)

    for hm in range(MEM_HEADS):
        lg = lgs[hm]
        p = jnp.exp2(lg - jnp.max(lg, axis=0, keepdims=True)).astype(BF16)
        yv = _dot(vm_ref[0, hm], p)
        y = yv[0:HEAD_DIM] / yv[HEAD_DIM:HEAD_DIM + 1]
        mm = y * _silu(gm[hm * HEAD_DIM:(hm + 1) * HEAD_DIM])
        mam_ref[0, pool_w + hm * HEAD_DIM:pool_w + (hm + 1) * HEAD_DIM, :] = mm.astype(BF16)

    uext = jnp.concatenate([halo_ref[...], u], axis=1)
    halo_ref[...] = u[:, ts - LANES:]
    pos1 = (s_idx * ts + lax.broadcasted_iota(jnp.int32, (1, ts), 1) + 1).astype(F32)
    acc = uext
    pooled = []
    shift = 1
    for g, w in enumerate(POOL_WINDOWS):
        while shift < w:
            acc = acc + pltpu.roll(acc, shift, 1)
            shift *= 2
        pooled.append(acc[0:gd, LANES:] / jnp.minimum(pos1, float(w)))
        acc = acc[gd:]
    d = (jnp.concatenate(pooled, axis=0) - u).astype(BF16)
    ya = _dot(wp_ref[...], d) * ps_ref[...]
    mam_ref[0, 0:pool_w, :] = (ya * _silu(ga)).astype(BF16)

    gbT_ref[0] = proj_t(offs["gb"], fox_w)

    fcum = (cs[0:BF16_ROWS] + cs[BF16_ROWS:2 * BF16_ROWS] + cs[2 * BF16_ROWS:3 * BF16_ROWS]
            + fcarry_ref[:, LANES - 1:LANES])
    fcarry_ref[...] = fcum[:, ts - LANES:]
    f2 = fcum * LOG2E

    arow = lax.broadcasted_iota(jnp.int32, (AUG_ROWS, ts), 0)

    def split3(v):
        v_hi = v.astype(BF16).astype(F32)
        r = v - v_hi
        v_mid = r.astype(BF16).astype(F32)
        return v_hi, v_mid, (r - v_mid).astype(BF16).astype(F32)

    def rows(*vals):
        out = jnp.zeros((AUG_ROWS, ts), F32)
        for r, v in enumerate(vals):
            out = jnp.where(arow == r, v, out)
        return out

    zero_rows = jnp.zeros((QK_ROWS - HEAD_DIM - AUG_ROWS, ts), F32)
    q_gain = gq_ref[...] * (SCALE * LOG2E)
    for hd in range(n_fox):
        qh = qT[hd * HEAD_DIM:(hd + 1) * HEAD_DIM]
        qn = qh * (_head_rms_scale(qh) * q_gain)
        f3 = split3(f2[hd:hd + 1])
        qT_ref[0, hd, 0:HEAD_DIM, :] = qn.astype(BF16)
        qT_ref[0, hd, HEAD_DIM:HEAD_DIM + AUG_ROWS, :] = rows(
            f3[0], f3[1], f3[2], 1.0, 1.0, 1.0, -mb_ref[0]).astype(BF16)
        qT_ref[0, hd, HEAD_DIM + AUG_ROWS:, :] = zero_rows.astype(BF16)

        kh = kT[hd * HEAD_DIM:(hd + 1) * HEAD_DIM]
        kn = kh * (_head_rms_scale(kh) * gk_ref[...])
        k_aug = rows(1.0, 1.0, 1.0, -f3[0], -f3[1], -f3[2], 1.0)
        kfull = jnp.concatenate([kn, k_aug, zero_rows], axis=0)
        kp_ref[0, hd] = kfull.T.astype(BF16)

    for hd in range(n_fox):
        va = vT[hd * HEAD_DIM:(hd + 1) * HEAD_DIM].astype(BF16)
        for c in range(ts // tk):
            vT_ref[0, hd, c] = va[:, c * tk:(c + 1) * tk]


def _fox_out_kernel(fast_ref, qT_ref, kp_ref, vT_ref, gbT_ref, mam_ref, x_ref, wo_ref, o_ref,
                    acc_ref, den_ref, *, pool_w):
    hg = qT_ref.shape[1]
    tq = qT_ref.shape[3]
    tk = vT_ref.shape[4]
    n_diag = tq // tk
    i = pl.program_id(1)

    acc_ref[...] = jnp.zeros_like(acc_ref)
    den_ref[...] = jnp.zeros_like(den_ref)
    causal = (lax.broadcasted_iota(jnp.int32, (tk, tq), 0)
              <= lax.broadcasted_iota(jnp.int32, (tk, tq), 1))

    def head_scores(hd, j, lo=0):
        start = pl.multiple_of(j * tk, tk)
        return _dot(kp_ref[0, hd, pl.ds(start, tk), :], qT_ref[0, hd, :, lo:])

    def mask(s, diag, lo):
        return s if diag is None else jnp.where(causal[:, :tq - lo], s, MASK_VALUE)

    def fast_tiles(tiles):
        units = [(j, diag, hd) for (j, diag) in tiles for hd in range(hg)]
        scores = {}

        def issue(u):
            j, diag, hd = units[u]
            scores[u] = head_scores(hd, j, 0 if diag is None else diag * tk)

        for u in range(min(FAST_LOOKAHEAD, len(units))):
            issue(u)
        for u, (j, diag, hd) in enumerate(units):
            lo = 0 if diag is None else diag * tk
            p = jnp.exp2(mask(scores.pop(u), diag, lo))
            den_ref[hd, :, lo:] += jnp.sum(p, axis=0, keepdims=True)
            acc_ref[hd, :, lo:] += _dot(vT_ref[0, hd, j], p.astype(BF16))
            if u + FAST_LOOKAHEAD < len(units):
                issue(u + FAST_LOOKAHEAD)

    def safe_step(j, ms, diag=None):
        lo = 0 if diag is None else diag * tk
        scores = [head_scores(hd, j, lo) for hd in range(hg)]
        out = []
        for hd in range(hg):
            s = mask(scores[hd], diag, lo)
            m_old = ms[hd][:, lo:]
            m_new = jnp.maximum(m_old, jnp.max(s, axis=0, keepdims=True))
            p = jnp.exp2(s - m_new)
            alpha = jnp.exp2(m_old - m_new)
            den_ref[hd, :, lo:] = alpha * den_ref[hd, :, lo:] + jnp.sum(p, axis=0, keepdims=True)
            acc_ref[hd, :, lo:] = alpha * acc_ref[hd, :, lo:] + _dot(vT_ref[0, hd, j], p.astype(BF16))
            out.append(m_new if lo == 0 else jnp.concatenate([ms[hd][:, :lo], m_new], axis=1))
        return tuple(out)

    def finish():
        mam = mam_ref[0]
        fox_w = hg * HEAD_DIM
        part = (_dot_tn(mam[0:pool_w], wo_ref[0:pool_w, :])
                + _dot_tn(mam[pool_w:], wo_ref[pool_w + fox_w:, :]))
        ys = []
        for hd in range(hg):
            ys.append(acc_ref[hd] / den_ref[hd, 0:1, :])
        yb = (jnp.concatenate(ys, axis=0) * _silu(gbT_ref[0])).astype(BF16)
        o_ref[0] = x_ref[0] + part + _dot_tn(yb, wo_ref[pool_w:pool_w + fox_w, :])

    @pl.when(fast_ref[0] != 0)
    def _():
        def below_diagonal(first, count):
            return [(first + d, None) for d in range(count)]

        def body(c, carry):
            fast_tiles(below_diagonal(c * FAST_TILES_PER_TRIP, FAST_TILES_PER_TRIP))
            return carry

        n_below = i * n_diag
        n_trips = n_below // FAST_TILES_PER_TRIP
        lax.fori_loop(0, n_trips, body, 0)
        for rest in range(0, FAST_TILES_PER_TRIP, n_diag):
            @pl.when(n_below - n_trips * FAST_TILES_PER_TRIP == rest)
            def _():
                fast_tiles(below_diagonal(n_trips * FAST_TILES_PER_TRIP, rest)
                           + [(n_below + c, c) for c in range(n_diag)])
                finish()

    @pl.when(fast_ref[0] == 0)
    def _():
        ms = tuple(jnp.full((1, tq), MASK_VALUE, F32) for _ in range(hg))
        ms = lax.fori_loop(0, i * n_diag, lambda j, c: safe_step(j, c), ms)
        for c in range(n_diag):
            ms = safe_step(i * n_diag + c, ms, diag=c)
        finish()


def _col(v, rows=None):
    v = v.astype(F32)
    if rows is not None and rows > v.shape[0]:
        v = jnp.concatenate([v, jnp.zeros((rows - v.shape[0],), F32)])
    return v[:, None]


def _params(sem):
    return pltpu.CompilerParams(dimension_semantics=sem, vmem_limit_bytes=VMEM_LIMIT_BYTES)


def _layer(x, mem, norm_g, w_in, b_f, w_pool, pool_scale, fox_q_g, fox_k_g,
           mem_norm_g, w_mem_kv, mem_q_g, mem_k_g, w_out):
    bsz, seq, dm = x.shape
    n_mem = mem.shape[1]
    n_fox = b_f.shape[0]
    pool_w = pool_scale.shape[0]
    fox_w = n_fox * HEAD_DIM
    mem_w = MEM_HEADS * HEAD_DIM
    assert n_fox <= BF16_ROWS and pool_w == len(POOL_WINDOWS) * POOL_GROUP_DIM
    assert w_in.shape[1] == 2 * pool_w + 4 * fox_w + n_fox + 2 * mem_w
    assert w_out.shape[0] == pool_w + fox_w + mem_w
    ts, tq, tk = SEQ_TILE, Q_TILE, KV_TILE
    assert seq % ts == 0 and seq % tq == 0 and tq % tk == 0 and ts % tk == 0
    assert FAST_TILES_PER_TRIP % (tq // tk) == 0

    sizes = (pool_w, pool_w, fox_w, fox_w, fox_w, n_fox, fox_w, mem_w, mem_w)
    w_in = w_in.astype(BF16)
    parts, off = [], 0
    for w in sizes:
        parts.append(w_in[:, off:off + w])
        off += w
    w_ua, w_ga, w_q, w_k, w_v, w_f, w_gb, w_qm, w_gm = parts
    w_f = jnp.concatenate([w_f, jnp.zeros((dm, BF16_ROWS - n_fox), w_in.dtype)], axis=1)
    order = [("ua", w_ua), ("ga", w_ga), ("f", w_f), ("q", w_q), ("k", w_k), ("v", w_v),
             ("gb", w_gb), ("qm", w_qm), ("gm", w_gm)]
    offs, off = {}, 0
    for name, w in order:
        offs[name] = off
        off += w.shape[1]
    wT = jnp.concatenate([w for _, w in order], axis=1).T
    n_rows = wT.shape[0]
    tri = (lax.broadcasted_iota(jnp.int32, (ts, ts), 0)
           <= lax.broadcasted_iota(jnp.int32, (ts, ts), 1)).astype(BF16)
    wp_bd = jax.scipy.linalg.block_diag(*[w_pool[g].T for g in range(len(POOL_WINDOWS))]).astype(BF16)
    wkvT = w_mem_kv.T.astype(BF16)
    wo = w_out.astype(BF16)

    nb = math.gcd(MEM_BATCH_ROWS_PER_STEP, bsz)
    km, vm = pl.pallas_call(
        _mem_kv_kernel,
        grid=(bsz // nb,),
        in_specs=[
            pl.BlockSpec((nb, n_mem, dm), lambda b: (b, 0, 0)),
            pl.BlockSpec((1, dm), lambda b: (0, 0)),
            pl.BlockSpec((2 * mem_w, dm), lambda b: (0, 0)),
            pl.BlockSpec((HEAD_DIM, 1), lambda b: (0, 0)),
        ],
        out_specs=[
            pl.BlockSpec((nb, MEM_HEADS, n_mem, QK_ROWS), lambda b: (b, 0, 0, 0)),
            pl.BlockSpec((nb, MEM_HEADS, MEM_V_ROWS, n_mem), lambda b: (b, 0, 0, 0)),
        ],
        out_shape=[
            jax.ShapeDtypeStruct((bsz, MEM_HEADS, n_mem, QK_ROWS), BF16),
            jax.ShapeDtypeStruct((bsz, MEM_HEADS, MEM_V_ROWS, n_mem), BF16),
        ],
        compiler_params=_params(("arbitrary",)),
        name="mem_kv",
    )(mem, mem_norm_g[None, :], wkvT, _col(mem_k_g))

    const2 = lambda b, s: (0, 0)
    m_bound = (HEAD_DIM * SCALE * LOG2E * BOUND_SLACK
               * jnp.max(jnp.abs(fox_q_g)) * jnp.max(jnp.abs(fox_k_g))).astype(F32).reshape(1)
    fast = (m_bound < FAST_MAX_SHIFT).astype(jnp.int32)
    qT, kp, vT, gbT, mam = pl.pallas_call(
        functools.partial(_proj_kernel, offs=offs, n_fox=n_fox, pool_w=pool_w,
                          fox_w=fox_w, mem_w=mem_w),
        grid=(bsz, seq // ts),
        in_specs=[
            pl.BlockSpec(memory_space=pltpu.SMEM),
            pl.BlockSpec((1, ts, dm), lambda b, s: (b, s, 0)),
            pl.BlockSpec((1, dm), const2),
            pl.BlockSpec((n_rows, dm), const2),
            pl.BlockSpec((BF16_ROWS, 1), const2),
            pl.BlockSpec((ts, ts), const2),
            pl.BlockSpec((HEAD_DIM, 1), const2),
            pl.BlockSpec((HEAD_DIM, 1), const2),
            pl.BlockSpec((pool_w, pool_w), const2),
            pl.BlockSpec((pool_w, 1), const2),
            pl.BlockSpec((1, MEM_HEADS, n_mem, QK_ROWS), lambda b, s: (b, 0, 0, 0)),
            pl.BlockSpec((1, MEM_HEADS, MEM_V_ROWS, n_mem), lambda b, s: (b, 0, 0, 0)),
            pl.BlockSpec((HEAD_DIM, 1), const2),
        ],
        out_specs=[
            pl.BlockSpec((1, n_fox, QK_ROWS, ts), lambda b, s: (b, 0, 0, s)),
            pl.BlockSpec((1, n_fox, ts, QK_ROWS), lambda b, s: (b, 0, s, 0)),
            pl.BlockSpec((1, n_fox, ts // tk, HEAD_DIM, tk), lambda b, s: (b, 0, s, 0, 0)),
            pl.BlockSpec((1, fox_w, ts), lambda b, s: (b, 0, s)),
            pl.BlockSpec((1, pool_w + mem_w, ts), lambda b, s: (b, 0, s)),
        ],
        out_shape=[
            jax.ShapeDtypeStruct((bsz, n_fox, QK_ROWS, seq), BF16),
            jax.ShapeDtypeStruct((bsz, n_fox, seq, QK_ROWS), BF16),
            jax.ShapeDtypeStruct((bsz, n_fox, seq // tk, HEAD_DIM, tk), BF16),
            jax.ShapeDtypeStruct((bsz, fox_w, seq), F32),
            jax.ShapeDtypeStruct((bsz, pool_w + mem_w, seq), BF16),
        ],
        scratch_shapes=[
            pltpu.VMEM((BF16_ROWS, LANES), F32),
            pltpu.VMEM((pool_w, LANES), F32),
        ],
        compiler_params=_params(("arbitrary", "arbitrary")),
        name="proj",
    )(m_bound, x, norm_g[None, :], wT, _col(b_f, BF16_ROWS), tri, _col(fox_q_g), _col(fox_k_g),
      wp_bd, _col(pool_scale), km, vm, _col(mem_q_g))

    return pl.pallas_call(
        functools.partial(_fox_out_kernel, pool_w=pool_w),
        grid=(bsz, seq // tq),
        in_specs=[
            pl.BlockSpec(memory_space=pltpu.SMEM),
            pl.BlockSpec((1, n_fox, QK_ROWS, tq), lambda b, i: (b, 0, 0, i)),
            pl.BlockSpec((1, n_fox, seq, QK_ROWS), lambda b, i: (b, 0, 0, 0)),
            pl.BlockSpec((1, n_fox, seq // tk, HEAD_DIM, tk), lambda b, i: (b, 0, 0, 0, 0)),
            pl.BlockSpec((1, fox_w, tq), lambda b, i: (b, 0, i)),
            pl.BlockSpec((1, pool_w + mem_w, tq), lambda b, i: (b, 0, i)),
            pl.BlockSpec((1, tq, dm), lambda b, i: (b, i, 0)),
            pl.BlockSpec((pool_w + fox_w + mem_w, dm), const2),
        ],
        out_specs=pl.BlockSpec((1, tq, dm), lambda b, i: (b, i, 0)),
        out_shape=jax.ShapeDtypeStruct((bsz, seq, dm), x.dtype),
        scratch_shapes=[pltpu.VMEM((n_fox, HEAD_DIM, tq), F32), pltpu.VMEM((n_fox, SUBLANES, tq), F32)],
        compiler_params=_params(("arbitrary", "arbitrary")),
        name="fox_out",
    )(fast, qT, kp, vT, gbT, mam, x, wo)


def kernel(x, mem, norm_g, w_in, b_f, w_pool, pool_scale, fox_q_g, fox_k_g, mem_norm_g,
           w_mem_kv, mem_q_g, mem_k_g, w_out):
    for l in range(norm_g.shape[0]):
        x = _layer(x, mem, norm_g[l], w_in[l], b_f[l], w_pool[l], pool_scale[l], fox_q_g[l],
                   fox_k_g[l], mem_norm_g[l], w_mem_kv[l], mem_q_g[l], mem_k_g[l], w_out[l])
    return x
```

```python
import functools
import math

import jax
import jax.numpy as jnp
from jax import lax
from jax.experimental import pallas as pl
from jax.experimental.pallas import tpu as pltpu

F32 = jnp.float32
BF16 = jnp.bfloat16

HEAD_DIM = 64
EPS = 1e-6
POOL_WINDOWS = (2, 4, 8, 16)
POOL_GROUP_DIM = 64
MEM_HEADS = 4
SCALE = 1.0 / math.sqrt(HEAD_DIM)
LOG2E = math.log2(math.e)
MASK_VALUE = -1e30

LANES = 128
SUBLANES = 8
BF16_ROWS = 16
QK_ROWS = 128
AUG_ROWS = BF16_ROWS
MEM_V_ROWS = HEAD_DIM + BF16_ROWS

SEQ_TILE = 1024
Q_TILE = 512
KV_TILE = 256
FAST_TILES_PER_TRIP = 6
FAST_LOOKAHEAD = 3
FAST_MAX_SHIFT = 50.0
BOUND_SLACK = 1.01
VMEM_LIMIT_BYTES = 56 * 1024 * 1024

NT_DIMS = (((1,), (1,)), ((), ()))
TN_DIMS = (((0,), (0,)), ((), ()))


def _dot(a, b):
    return jnp.dot(a, b, preferred_element_type=F32)


def _dot_nt(a, b):
    return lax.dot_general(a, b, NT_DIMS, preferred_element_type=F32)


def _dot_tn(a, b):
    return lax.dot_general(a, b, TN_DIMS, preferred_element_type=F32)


def _silu(g):
    return g * jax.nn.sigmoid(g)


def _log_sigmoid(z):
    return jnp.minimum(z, 0.0) - jnp.log1p(jnp.exp(-jnp.abs(z)))


def _head_rms_scale(t):
    return lax.rsqrt(jnp.mean(t * t, axis=0, keepdims=True) + EPS)


def _ones_row_block(n):
    row = lax.broadcasted_iota(jnp.int32, (BF16_ROWS, n), 0)
    return (row == 0).astype(F32)


def _mem_kv(mem_ref, g_ref, wkvT_ref, gk_ref, km_ref, vm_ref):
    nb, n_mem, dm = mem_ref.shape
    x = mem_ref[...].reshape(nb * n_mem, dm)
    h = (x * lax.rsqrt(jnp.mean(x * x, axis=-1, keepdims=True) + EPS) * g_ref[...]).astype(BF16)
    kvT = _dot_nt(wkvT_ref[...], h)
    width = MEM_HEADS * HEAD_DIM
    ones_blk = _ones_row_block(n_mem)
    zero_pad = jnp.zeros((QK_ROWS - HEAD_DIM, n_mem), F32)
    for hm in range(MEM_HEADS):
        kh = kvT[hm * HEAD_DIM:(hm + 1) * HEAD_DIM]
        kn = kh * _head_rms_scale(kh) * gk_ref[...]
        vh = kvT[width + hm * HEAD_DIM:width + (hm + 1) * HEAD_DIM]
        for bi in range(nb):
            cols = slice(bi * n_mem, (bi + 1) * n_mem)
            km_ref[bi, hm] = jnp.concatenate([kn[:, cols], zero_pad], axis=0).T.astype(BF16)
            vm_ref[bi, hm] = jnp.concatenate([vh[:, cols], ones_blk], axis=0).astype(BF16)


def _proj_kernel(mb_ref, x_ref, ng_ref, wT_ref, bf_ref, tri_ref, gq_ref, gk_ref, wp_ref, ps_ref,
                 mem_ref, mg_ref, wkvT_ref, gmk_ref, gmq_ref,
                 qT_ref, kp_ref, vT_ref, gbT_ref, mam_ref,
                 fcarry_ref, halo_ref, km_ref, vm_ref, *, offs, n_fox, pool_w, fox_w, mem_w):
    ts = x_ref.shape[1]
    s_idx = pl.program_id(1)

    @pl.when(s_idx == 0)
    def _():
        fcarry_ref[...] = jnp.zeros_like(fcarry_ref)
        halo_ref[...] = jnp.zeros_like(halo_ref)
        _mem_kv(mem_ref, mg_ref, wkvT_ref, gmk_ref, km_ref, vm_ref)

    x = x_ref[0]
    h = (x * lax.rsqrt(jnp.mean(x * x, axis=-1, keepdims=True) + EPS) * ng_ref[...]).astype(BF16)

    def proj_t(lo, n):
        return _dot_nt(wT_ref[lo:lo + n, :], h)

    tk = vT_ref.shape[4]
    gd = POOL_GROUP_DIM

    sec_m = proj_t(offs["qm"], 2 * mem_w)
    qm = sec_m[0:mem_w]
    gm = sec_m[mem_w:2 * mem_w]
    sec_a = proj_t(offs["ua"], 2 * pool_w + BF16_ROWS)
    u = sec_a[0:pool_w]
    ga = sec_a[pool_w:2 * pool_w]
    z = sec_a[2 * pool_w:2 * pool_w + BF16_ROWS] + bf_ref[...]

    zero_pad = jnp.zeros((QK_ROWS - HEAD_DIM, ts), F32)
    mq_gain = gmq_ref[...] * (SCALE * LOG2E)
    lgs = []
    for hm in range(MEM_HEADS):
        qh = qm[hm * HEAD_DIM:(hm + 1) * HEAD_DIM]
        qn = qh * (_head_rms_scale(qh) * mq_gain)
        qpad = jnp.concatenate([qn, zero_pad], axis=0).astype(BF16)
        lgs.append(_dot(km_ref[0, hm], qpad))

    qT = proj_t(offs["q"], fox_w)

    logf = _log_sigmoid(z)
    hi = logf.astype(BF16)
    r1 = logf - hi.astype(F32)
    mid = r1.astype(BF16)
    lo = (r1 - mid.astype(F32)).astype(BF16)
    cs = _dot(jnp.concatenate([hi, mid, lo], axis=0), tri_ref[...])

    kT = proj_t(offs["k"], fox_w)
    vT = proj_t(offs["v"], fox_w)

    for hm in range(MEM_HEADS):
        lg = lgs[hm]
        p = jnp.exp2(lg - jnp.max(lg, axis=0, keepdims=True)).astype(BF16)
        yv = _dot(vm_ref[0, hm], p)
        y = yv[0:HEAD_DIM] / yv[HEAD_DIM:HEAD_DIM + 1]
        mm = y * _silu(gm[hm * HEAD_DIM:(hm + 1) * HEAD_DIM])
        mam_ref[0, pool_w + hm * HEAD_DIM:pool_w + (hm + 1) * HEAD_DIM, :] = mm.astype(BF16)

    uext = jnp.concatenate([halo_ref[...], u], axis=1)
    halo_ref[...] = u[:, ts - LANES:]
    pos1 = (s_idx * ts + lax.broadcasted_iota(jnp.int32, (1, ts), 1) + 1).astype(F32)
    acc = uext
    pooled = []
    shift = 1
    for g, w in enumerate(POOL_WINDOWS):
        while shift < w:
            acc = acc + pltpu.roll(acc, shift, 1)
            shift *= 2
        pooled.append(acc[0:gd, LANES:] / jnp.minimum(pos1, float(w)))
        acc = acc[gd:]
    d = (jnp.concatenate(pooled, axis=0) - u).astype(BF16)
    ya = _dot(wp_ref[...], d) * ps_ref[...]
    mam_ref[0, 0:pool_w, :] = (ya * _silu(ga)).astype(BF16)

    gbT_ref[0] = proj_t(offs["gb"], fox_w)

    fcum = (cs[0:BF16_ROWS] + cs[BF16_ROWS:2 * BF16_ROWS] + cs[2 * BF16_ROWS:3 * BF16_ROWS]
            + fcarry_ref[:, LANES - 1:LANES])
    fcarry_ref[...] = fcum[:, ts - LANES:]
    f2 = fcum * LOG2E

    arow = lax.broadcasted_iota(jnp.int32, (AUG_ROWS, ts), 0)

    def split3(v):
        v_hi = v.astype(BF16).astype(F32)
        r = v - v_hi
        v_mid = r.astype(BF16).astype(F32)
        return v_hi, v_mid, (r - v_mid).astype(BF16).astype(F32)

    def rows(*vals):
        out = jnp.zeros((AUG_ROWS, ts), F32)
        for r, v in enumerate(vals):
            out = jnp.where(arow == r, v, out)
        return out

    zero_rows = jnp.zeros((QK_ROWS - HEAD_DIM - AUG_ROWS, ts), F32)
    q_gain = gq_ref[...] * (SCALE * LOG2E)
    for hd in range(n_fox):
        qh = qT[hd * HEAD_DIM:(hd + 1) * HEAD_DIM]
        qn = qh * (_head_rms_scale(qh) * q_gain)
        f3 = split3(f2[hd:hd + 1])
        qT_ref[0, hd, 0:HEAD_DIM, :] = qn.astype(BF16)
        qT_ref[0, hd, HEAD_DIM:HEAD_DIM + AUG_ROWS, :] = rows(
            f3[0], f3[1], f3[2], 1.0, 1.0, 1.0, -mb_ref[0]).astype(BF16)
        qT_ref[0, hd, HEAD_DIM + AUG_ROWS:, :] = zero_rows.astype(BF16)

        kh = kT[hd * HEAD_DIM:(hd + 1) * HEAD_DIM]
        kn = kh * (_head_rms_scale(kh) * gk_ref[...])
        k_aug = rows(1.0, 1.0, 1.0, -f3[0], -f3[1], -f3[2], 1.0)
        kfull = jnp.concatenate([kn, k_aug, zero_rows], axis=0)
        kp_ref[0, hd] = kfull.T.astype(BF16)

    for hd in range(n_fox):
        va = vT[hd * HEAD_DIM:(hd + 1) * HEAD_DIM].astype(BF16)
        for c in range(ts // tk):
            vT_ref[0, hd, c] = va[:, c * tk:(c + 1) * tk]


def _fox_out_kernel(fast_ref, qT_ref, kp_ref, vT_ref, gbT_ref, mam_ref, x_ref, wo_ref, o_ref,
                    acc_ref, den_ref, *, pool_w):
    hg = qT_ref.shape[1]
    tq = qT_ref.shape[3]
    tk = vT_ref.shape[4]
    n_diag = tq // tk
    i = pl.program_id(1)

    acc_ref[...] = jnp.zeros_like(acc_ref)
    den_ref[...] = jnp.zeros_like(den_ref)
    causal = (lax.broadcasted_iota(jnp.int32, (tk, tq), 0)
              <= lax.broadcasted_iota(jnp.int32, (tk, tq), 1))

    def head_scores(hd, j, lo=0):
        start = pl.multiple_of(j * tk, tk)
        return _dot(kp_ref[0, hd, pl.ds(start, tk), :], qT_ref[0, hd, :, lo:])

    def mask(s, diag, lo):
        return s if diag is None else jnp.where(causal[:, :tq - lo], s, MASK_VALUE)

    def fast_tiles(tiles):
        units = [(j, diag, hd) for (j, diag) in tiles for hd in range(hg)]
        scores = {}

        def issue(u):
            j, diag, hd = units[u]
            scores[u] = head_scores(hd, j, 0 if diag is None else diag * tk)

        for u in range(min(FAST_LOOKAHEAD, len(units))):
            issue(u)
        for u, (j, diag, hd) in enumerate(units):
            lo = 0 if diag is None else diag * tk
            p = jnp.exp2(mask(scores.pop(u), diag, lo))
            den_ref[hd, :, lo:] += jnp.sum(p, axis=0, keepdims=True)
            acc_ref[hd, :, lo:] += _dot(vT_ref[0, hd, j], p.astype(BF16))
            if u + FAST_LOOKAHEAD < len(units):
                issue(u + FAST_LOOKAHEAD)

    def safe_step(j, ms, diag=None):
        lo = 0 if diag is None else diag * tk
        scores = [head_scores(hd, j, lo) for hd in range(hg)]
        out = []
        for hd in range(hg):
            s = mask(scores[hd], diag, lo)
            m_old = ms[hd][:, lo:]
            m_new = jnp.maximum(m_old, jnp.max(s, axis=0, keepdims=True))
            p = jnp.exp2(s - m_new)
            alpha = jnp.exp2(m_old - m_new)
            den_ref[hd, :, lo:] = alpha * den_ref[hd, :, lo:] + jnp.sum(p, axis=0, keepdims=True)
            acc_ref[hd, :, lo:] = alpha * acc_ref[hd, :, lo:] + _dot(vT_ref[0, hd, j], p.astype(BF16))
            out.append(m_new if lo == 0 else jnp.concatenate([ms[hd][:, :lo], m_new], axis=1))
        return tuple(out)

    def finish():
        mam = mam_ref[0]
        fox_w = hg * HEAD_DIM
        part = (_dot_tn(mam[0:pool_w], wo_ref[0:pool_w, :])
                + _dot_tn(mam[pool_w:], wo_ref[pool_w + fox_w:, :]))
        ys = []
        for hd in range(hg):
            ys.append(acc_ref[hd] / den_ref[hd, 0:1, :])
        yb = (jnp.concatenate(ys, axis=0) * _silu(gbT_ref[0])).astype(BF16)
        o_ref[0] = x_ref[0] + part + _dot_tn(yb, wo_ref[pool_w:pool_w + fox_w, :])

    @pl.when(fast_ref[0] != 0)
    def _():
        def below_diagonal(first, count):
            return [(first + d, None) for d in range(count)]

        def body(c, carry):
            fast_tiles(below_diagonal(c * FAST_TILES_PER_TRIP, FAST_TILES_PER_TRIP))
            return carry

        n_below = i * n_diag
        n_trips = n_below // FAST_TILES_PER_TRIP
        lax.fori_loop(0, n_trips, body, 0)
        for rest in range(0, FAST_TILES_PER_TRIP, n_diag):
            @pl.when(n_below - n_trips * FAST_TILES_PER_TRIP == rest)
            def _():
                fast_tiles(below_diagonal(n_trips * FAST_TILES_PER_TRIP, rest)
                           + [(n_below + c, c) for c in range(n_diag)])
                finish()

    @pl.when(fast_ref[0] == 0)
    def _():
        ms = tuple(jnp.full((1, tq), MASK_VALUE, F32) for _ in range(hg))
        ms = lax.fori_loop(0, i * n_diag, lambda j, c: safe_step(j, c), ms)
        for c in range(n_diag):
            ms = safe_step(i * n_diag + c, ms, diag=c)
        finish()


def _col(v, rows=None):
    v = v.astype(F32)
    if rows is not None and rows > v.shape[0]:
        v = jnp.concatenate([v, jnp.zeros((rows - v.shape[0],), F32)])
    return v[:, None]


def _params(sem):
    return pltpu.CompilerParams(dimension_semantics=sem, vmem_limit_bytes=VMEM_LIMIT_BYTES)


def _layer(x, mem, norm_g, w_in, b_f, w_pool, pool_scale, fox_q_g, fox_k_g,
           mem_norm_g, w_mem_kv, mem_q_g, mem_k_g, w_out):
    bsz, seq, dm = x.shape
    n_mem = mem.shape[1]
    n_fox = b_f.shape[0]
    pool_w = pool_scale.shape[0]
    fox_w = n_fox * HEAD_DIM
    mem_w = MEM_HEADS * HEAD_DIM
    assert n_fox <= BF16_ROWS and pool_w == len(POOL_WINDOWS) * POOL_GROUP_DIM
    assert w_in.shape[1] == 2 * pool_w + 4 * fox_w + n_fox + 2 * mem_w
    assert w_out.shape[0] == pool_w + fox_w + mem_w
    ts, tq, tk = SEQ_TILE, Q_TILE, KV_TILE
    assert seq % ts == 0 and seq % tq == 0 and tq % tk == 0 and ts % tk == 0
    assert FAST_TILES_PER_TRIP % (tq // tk) == 0

    sizes = (pool_w, pool_w, fox_w, fox_w, fox_w, n_fox, fox_w, mem_w, mem_w)
    w_in = w_in.astype(BF16)
    parts, off = [], 0
    for w in sizes:
        parts.append(w_in[:, off:off + w])
        off += w
    w_ua, w_ga, w_q, w_k, w_v, w_f, w_gb, w_qm, w_gm = parts
    w_f = jnp.concatenate([w_f, jnp.zeros((dm, BF16_ROWS - n_fox), w_in.dtype)], axis=1)
    order = [("ua", w_ua), ("ga", w_ga), ("f", w_f), ("q", w_q), ("k", w_k), ("v", w_v),
             ("gb", w_gb), ("qm", w_qm), ("gm", w_gm)]
    offs, off = {}, 0
    for name, w in order:
        offs[name] = off
        off += w.shape[1]
    wT = jnp.concatenate([w for _, w in order], axis=1).T
    n_rows = wT.shape[0]
    tri = (lax.broadcasted_iota(jnp.int32, (ts, ts), 0)
           <= lax.broadcasted_iota(jnp.int32, (ts, ts), 1)).astype(BF16)
    wp_bd = jax.scipy.linalg.block_diag(*[w_pool[g].T for g in range(len(POOL_WINDOWS))]).astype(BF16)
    wkvT = w_mem_kv.T.astype(BF16)
    wo = w_out.astype(BF16)

    const2 = lambda b, s: (0, 0)
    m_bound = (HEAD_DIM * SCALE * LOG2E * BOUND_SLACK
               * jnp.max(jnp.abs(fox_q_g)) * jnp.max(jnp.abs(fox_k_g))).astype(F32).reshape(1)
    fast = (m_bound < FAST_MAX_SHIFT).astype(jnp.int32)
    qT, kp, vT, gbT, mam = pl.pallas_call(
        functools.partial(_proj_kernel, offs=offs, n_fox=n_fox, pool_w=pool_w,
                          fox_w=fox_w, mem_w=mem_w),
        grid=(bsz, seq // ts),
        in_specs=[
            pl.BlockSpec(memory_space=pltpu.SMEM),
            pl.BlockSpec((1, ts, dm), lambda b, s: (b, s, 0)),
            pl.BlockSpec((1, dm), const2),
            pl.BlockSpec((n_rows, dm), const2),
            pl.BlockSpec((BF16_ROWS, 1), const2),
            pl.BlockSpec((ts, ts), const2),
            pl.BlockSpec((HEAD_DIM, 1), const2),
            pl.BlockSpec((HEAD_DIM, 1), const2),
            pl.BlockSpec((pool_w, pool_w), const2),
            pl.BlockSpec((pool_w, 1), const2),
            pl.BlockSpec((1, n_mem, dm), lambda b, s: (b, 0, 0)),
            pl.BlockSpec((1, dm), const2),
            pl.BlockSpec((2 * mem_w, dm), const2),
            pl.BlockSpec((HEAD_DIM, 1), const2),
            pl.BlockSpec((HEAD_DIM, 1), const2),
        ],
        out_specs=[
            pl.BlockSpec((1, n_fox, QK_ROWS, ts), lambda b, s: (b, 0, 0, s)),
            pl.BlockSpec((1, n_fox, ts, QK_ROWS), lambda b, s: (b, 0, s, 0)),
            pl.BlockSpec((1, n_fox, ts // tk, HEAD_DIM, tk), lambda b, s: (b, 0, s, 0, 0)),
            pl.BlockSpec((1, fox_w, ts), lambda b, s: (b, 0, s)),
            pl.BlockSpec((1, pool_w + mem_w, ts), lambda b, s: (b, 0, s)),
        ],
        out_shape=[
            jax.ShapeDtypeStruct((bsz, n_fox, QK_ROWS, seq), BF16),
            jax.ShapeDtypeStruct((bsz, n_fox, seq, QK_ROWS), BF16),
            jax.ShapeDtypeStruct((bsz, n_fox, seq // tk, HEAD_DIM, tk), BF16),
            jax.ShapeDtypeStruct((bsz, fox_w, seq), F32),
            jax.ShapeDtypeStruct((bsz, pool_w + mem_w, seq), BF16),
        ],
        scratch_shapes=[
            pltpu.VMEM((BF16_ROWS, LANES), F32),
            pltpu.VMEM((pool_w, LANES), F32),
            pltpu.VMEM((1, MEM_HEADS, n_mem, QK_ROWS), BF16),
            pltpu.VMEM((1, MEM_HEADS, MEM_V_ROWS, n_mem), BF16),
        ],
        compiler_params=_params(("arbitrary", "arbitrary")),
        name="proj",
    )(m_bound, x, norm_g[None, :], wT, _col(b_f, BF16_ROWS), tri, _col(fox_q_g), _col(fox_k_g),
      wp_bd, _col(pool_scale), mem, mem_norm_g[None, :], wkvT, _col(mem_k_g), _col(mem_q_g))

    return pl.pallas_call(
        functools.partial(_fox_out_kernel, pool_w=pool_w),
        grid=(bsz, seq // tq),
        in_specs=[
            pl.BlockSpec(memory_space=pltpu.SMEM),
            pl.BlockSpec((1, n_fox, QK_ROWS, tq), lambda b, i: (b, 0, 0, i)),
            pl.BlockSpec((1, n_fox, seq, QK_ROWS), lambda b, i: (b, 0, 0, 0)),
            pl.BlockSpec((1, n_fox, seq // tk, HEAD_DIM, tk), lambda b, i: (b, 0, 0, 0, 0)),
            pl.BlockSpec((1, fox_w, tq), lambda b, i: (b, 0, i)),
            pl.BlockSpec((1, pool_w + mem_w, tq), lambda b, i: (b, 0, i)),
            pl.BlockSpec((1, tq, dm), lambda b, i: (b, i, 0)),
            pl.BlockSpec((pool_w + fox_w + mem_w, dm), const2),
        ],
        out_specs=pl.BlockSpec((1, tq, dm), lambda b, i: (b, i, 0)),
        out_shape=jax.ShapeDtypeStruct((bsz, seq, dm), x.dtype),
        scratch_shapes=[pltpu.VMEM((n_fox, HEAD_DIM, tq), F32), pltpu.VMEM((n_fox, SUBLANES, tq), F32)],
        compiler_params=_params(("arbitrary", "arbitrary")),
        name="fox_out",
    )(fast, qT, kp, vT, gbT, mam, x, wo)


def kernel(x, mem, norm_g, w_in, b_f, w_pool, pool_scale, fox_q_g, fox_k_g, mem_norm_g,
           w_mem_kv, mem_q_g, mem_k_g, w_out):
    for l in range(norm_g.shape[0]):
        x = _layer(x, mem, norm_g[l], w_in[l], b_f[l], w_pool[l], pool_scale[l], fox_q_g[l],
                   fox_k_g[l], mem_norm_g[l], w_mem_kv[l], mem_q_g[l], mem_k_g[l], w_out[l])
    return x
```

```python
import functools
import math

import jax
import jax.numpy as jnp
from jax import lax
from jax.experimental import pallas as pl
from jax.experimental.pallas import tpu as pltpu

F32 = jnp.float32
BF16 = jnp.bfloat16

HEAD_DIM = 64
EPS = 1e-6
POOL_WINDOWS = (2, 4, 8, 16)
POOL_GROUP_DIM = 64
MEM_HEADS = 4
SCALE = 1.0 / math.sqrt(HEAD_DIM)
LOG2E = math.log2(math.e)
MASK_VALUE = -1e30

LANES = 128
SUBLANES = 8
BF16_ROWS = 16
QK_ROWS = 128
AUG_ROWS = BF16_ROWS
MEM_V_ROWS = HEAD_DIM + BF16_ROWS

MEM_BATCH_ROWS_PER_STEP = 4
SEQ_TILE = 1024
CUMSUM_BLOCK = 256
Q_TILE = 512
KV_TILE = 256
FAST_TILES_PER_TRIP = 6
FAST_LOOKAHEAD = 3
FAST_MAX_SHIFT = 50.0
BOUND_SLACK = 1.01
VMEM_LIMIT_BYTES = 56 * 1024 * 1024

NT_DIMS = (((1,), (1,)), ((), ()))
TN_DIMS = (((0,), (0,)), ((), ()))


def _dot(a, b):
    return jnp.dot(a, b, preferred_element_type=F32)


def _dot_nt(a, b):
    return lax.dot_general(a, b, NT_DIMS, preferred_element_type=F32)


def _dot_tn(a, b):
    return lax.dot_general(a, b, TN_DIMS, preferred_element_type=F32)


def _silu(g):
    return g * jax.nn.sigmoid(g)


def _log_sigmoid(z):
    return jnp.minimum(z, 0.0) - jnp.log1p(jnp.exp(-jnp.abs(z)))


def _head_rms_scale(t):
    return lax.rsqrt(jnp.mean(t * t, axis=0, keepdims=True) + EPS)


def _ones_row_block(n):
    row = lax.broadcasted_iota(jnp.int32, (BF16_ROWS, n), 0)
    return (row == 0).astype(F32)


def _mem_kv_kernel(mem_ref, g_ref, wkvT_ref, gk_ref, km_ref, vm_ref):
    nb, n_mem, dm = mem_ref.shape
    x = mem_ref[...].reshape(nb * n_mem, dm)
    h = (x * lax.rsqrt(jnp.mean(x * x, axis=-1, keepdims=True) + EPS) * g_ref[...]).astype(BF16)
    kvT = _dot_nt(wkvT_ref[...], h)
    width = MEM_HEADS * HEAD_DIM
    ones_blk = _ones_row_block(n_mem)
    zero_pad = jnp.zeros((QK_ROWS - HEAD_DIM, n_mem), F32)
    for hm in range(MEM_HEADS):
        kh = kvT[hm * HEAD_DIM:(hm + 1) * HEAD_DIM]
        kn = kh * _head_rms_scale(kh) * gk_ref[...]
        vh = kvT[width + hm * HEAD_DIM:width + (hm + 1) * HEAD_DIM]
        for bi in range(nb):
            cols = slice(bi * n_mem, (bi + 1) * n_mem)
            km_ref[bi, hm] = jnp.concatenate([kn[:, cols], zero_pad], axis=0).T.astype(BF16)
            vm_ref[bi, hm] = jnp.concatenate([vh[:, cols], ones_blk], axis=0).astype(BF16)


def _proj_kernel(mb_ref, x_ref, ng_ref, wT_ref, bf_ref, tri_ref, gq_ref, gk_ref, wp_ref, ps_ref,
                 km_ref, vm_ref, gmq_ref,
                 qT_ref, kp_ref, vT_ref, gbT_ref, mam_ref,
                 fcarry_ref, halo_ref, *, offs, n_fox, pool_w, fox_w, mem_w):
    ts = x_ref.shape[1]
    s_idx = pl.program_id(1)

    @pl.when(s_idx == 0)
    def _():
        fcarry_ref[...] = jnp.zeros_like(fcarry_ref)
        halo_ref[...] = jnp.zeros_like(halo_ref)

    x = x_ref[0]
    h = (x * lax.rsqrt(jnp.mean(x * x, axis=-1, keepdims=True) + EPS) * ng_ref[...]).astype(BF16)

    def proj_t(lo, n):
        return _dot_nt(wT_ref[lo:lo + n, :], h)

    tk = vT_ref.shape[4]
    gd = POOL_GROUP_DIM

    sec_m = proj_t(offs["qm"], 2 * mem_w)
    qm = sec_m[0:mem_w]
    gm = sec_m[mem_w:2 * mem_w]
    sec_a = proj_t(offs["ua"], 2 * pool_w + BF16_ROWS)
    u = sec_a[0:pool_w]
    ga = sec_a[pool_w:2 * pool_w]
    z = sec_a[2 * pool_w:2 * pool_w + BF16_ROWS] + bf_ref[...]

    zero_pad = jnp.zeros((QK_ROWS - HEAD_DIM, ts), F32)
    mq_gain = gmq_ref[...] * (SCALE * LOG2E)
    lgs = []
    for hm in range(MEM_HEADS):
        qh = qm[hm * HEAD_DIM:(hm + 1) * HEAD_DIM]
        qn = qh * (_head_rms_scale(qh) * mq_gain)
        qpad = jnp.concatenate([qn, zero_pad], axis=0).astype(BF16)
        lgs.append(_dot(km_ref[0, hm], qpad))

    qT = proj_t(offs["q"], fox_w)

    logf = _log_sigmoid(z)
    hi = logf.astype(BF16)
    r1 = logf - hi.astype(F32)
    mid = r1.astype(BF16)
    lo = (r1 - mid.astype(F32)).astype(BF16)
    parts = jnp.concatenate([hi, mid, lo], axis=0)
    cb = tri_ref.shape[0]
    cs = [_dot(parts[:, c:c + cb], tri_ref[...]) for c in range(0, ts, cb)]

    kT = proj_t(offs["k"], fox_w)
    vT = proj_t(offs["v"], fox_w)

    for hm in range(MEM_HEADS):
        lg = lgs[hm]
        p = jnp.exp2(lg - jnp.max(lg, axis=0, keepdims=True)).astype(BF16)
        yv = _dot(vm_ref[0, hm], p)
        y = yv[0:HEAD_DIM] / yv[HEAD_DIM:HEAD_DIM + 1]
        mm = y * _silu(gm[hm * HEAD_DIM:(hm + 1) * HEAD_DIM])
        mam_ref[0, pool_w + hm * HEAD_DIM:pool_w + (hm + 1) * HEAD_DIM, :] = mm.astype(BF16)

    uext = jnp.concatenate([halo_ref[...], u], axis=1)
    halo_ref[...] = u[:, ts - LANES:]
    pos1 = (s_idx * ts + lax.broadcasted_iota(jnp.int32, (1, ts), 1) + 1).astype(F32)
    acc = uext
    pooled = []
    shift = 1
    for g, w in enumerate(POOL_WINDOWS):
        while shift < w:
            acc = acc + pltpu.roll(acc, shift, 1)
            shift *= 2
        pooled.append(acc[0:gd, LANES:] / jnp.minimum(pos1, float(w)))
        acc = acc[gd:]
    d = (jnp.concatenate(pooled, axis=0) - u).astype(BF16)
    ya = _dot(wp_ref[...], d) * ps_ref[...]
    mam_ref[0, 0:pool_w, :] = (ya * _silu(ga)).astype(BF16)

    gbT_ref[0] = proj_t(offs["gb"], fox_w)

    run = fcarry_ref[:, LANES - 1:LANES]
    blocks = []
    for c in cs:
        blocks.append(c[0:BF16_ROWS] + c[BF16_ROWS:2 * BF16_ROWS] + c[2 * BF16_ROWS:3 * BF16_ROWS] + run)
        run = blocks[-1][:, cb - 1:cb]
    fcum = jnp.concatenate(blocks, axis=1)
    fcarry_ref[...] = fcum[:, ts - LANES:]
    f2 = fcum * LOG2E

    arow = lax.broadcasted_iota(jnp.int32, (AUG_ROWS, ts), 0)

    def split3(v):
        v_hi = v.astype(BF16).astype(F32)
        r = v - v_hi
        v_mid = r.astype(BF16).astype(F32)
        return v_hi, v_mid, (r - v_mid).astype(BF16).astype(F32)

    def rows(*vals):
        out = jnp.zeros((AUG_ROWS, ts), F32)
        for r, v in enumerate(vals):
            out = jnp.where(arow == r, v, out)
        return out

    zero_rows = jnp.zeros((QK_ROWS - HEAD_DIM - AUG_ROWS, ts), F32)
    q_gain = gq_ref[...] * (SCALE * LOG2E)
    for hd in range(n_fox):
        qh = qT[hd * HEAD_DIM:(hd + 1) * HEAD_DIM]
        qn = qh * (_head_rms_scale(qh) * q_gain)
        f3 = split3(f2[hd:hd + 1])
        qT_ref[0, hd, 0:HEAD_DIM, :] = qn.astype(BF16)
        qT_ref[0, hd, HEAD_DIM:HEAD_DIM + AUG_ROWS, :] = rows(
            f3[0], f3[1], f3[2], 1.0, 1.0, 1.0, -mb_ref[0]).astype(BF16)
        qT_ref[0, hd, HEAD_DIM + AUG_ROWS:, :] = zero_rows.astype(BF16)

        kh = kT[hd * HEAD_DIM:(hd + 1) * HEAD_DIM]
        kn = kh * (_head_rms_scale(kh) * gk_ref[...])
        k_aug = rows(1.0, 1.0, 1.0, -f3[0], -f3[1], -f3[2], 1.0)
        kfull = jnp.concatenate([kn, k_aug, zero_rows], axis=0)
        kp_ref[0, hd] = kfull.T.astype(BF16)

    for hd in range(n_fox):
        va = vT[hd * HEAD_DIM:(hd + 1) * HEAD_DIM].astype(BF16)
        for c in range(ts // tk):
            vT_ref[0, hd, c] = va[:, c * tk:(c + 1) * tk]


def _fox_out_kernel(fast_ref, qT_ref, kp_ref, vT_ref, gbT_ref, mam_ref, x_ref, wo_ref, o_ref,
                    acc_ref, den_ref, *, pool_w):
    hg = qT_ref.shape[1]
    tq = qT_ref.shape[3]
    tk = vT_ref.shape[4]
    n_diag = tq // tk
    i = pl.program_id(1)

    acc_ref[...] = jnp.zeros_like(acc_ref)
    den_ref[...] = jnp.zeros_like(den_ref)
    causal = (lax.broadcasted_iota(jnp.int32, (tk, tq), 0)
              <= lax.broadcasted_iota(jnp.int32, (tk, tq), 1))

    def head_scores(hd, j, lo=0):
        start = pl.multiple_of(j * tk, tk)
        return _dot(kp_ref[0, hd, pl.ds(start, tk), :], qT_ref[0, hd, :, lo:])

    def mask(s, diag, lo):
        return s if diag is None else jnp.where(causal[:, :tq - lo], s, MASK_VALUE)

    def fast_tiles(tiles):
        units = [(j, diag, hd) for (j, diag) in tiles for hd in range(hg)]
        scores = {}

        def issue(u):
            j, diag, hd = units[u]
            scores[u] = head_scores(hd, j, 0 if diag is None else diag * tk)

        for u in range(min(FAST_LOOKAHEAD, len(units))):
            issue(u)
        for u, (j, diag, hd) in enumerate(units):
            lo = 0 if diag is None else diag * tk
            p = jnp.exp2(mask(scores.pop(u), diag, lo))
            den_ref[hd, :, lo:] += jnp.sum(p, axis=0, keepdims=True)
            acc_ref[hd, :, lo:] += _dot(vT_ref[0, hd, j], p.astype(BF16))
            if u + FAST_LOOKAHEAD < len(units):
                issue(u + FAST_LOOKAHEAD)

    def safe_step(j, ms, diag=None):
        lo = 0 if diag is None else diag * tk
        scores = [head_scores(hd, j, lo) for hd in range(hg)]
        out = []
        for hd in range(hg):
            s = mask(scores[hd], diag, lo)
            m_old = ms[hd][:, lo:]
            m_new = jnp.maximum(m_old, jnp.max(s, axis=0, keepdims=True))
            p = jnp.exp2(s - m_new)
            alpha = jnp.exp2(m_old - m_new)
            den_ref[hd, :, lo:] = alpha * den_ref[hd, :, lo:] + jnp.sum(p, axis=0, keepdims=True)
            acc_ref[hd, :, lo:] = alpha * acc_ref[hd, :, lo:] + _dot(vT_ref[0, hd, j], p.astype(BF16))
            out.append(m_new if lo == 0 else jnp.concatenate([ms[hd][:, :lo], m_new], axis=1))
        return tuple(out)

    def finish():
        mam = mam_ref[0]
        fox_w = hg * HEAD_DIM
        part = (_dot_tn(mam[0:pool_w], wo_ref[0:pool_w, :])
                + _dot_tn(mam[pool_w:], wo_ref[pool_w + fox_w:, :]))
        ys = []
        for hd in range(hg):
            ys.append(acc_ref[hd] / den_ref[hd, 0:1, :])
        yb = (jnp.concatenate(ys, axis=0) * _silu(gbT_ref[0])).astype(BF16)
        o_ref[0] = x_ref[0] + part + _dot_tn(yb, wo_ref[pool_w:pool_w + fox_w, :])

    @pl.when(fast_ref[0] != 0)
    def _():
        def below_diagonal(first, count):
            return [(first + d, None) for d in range(count)]

        def body(c, carry):
            fast_tiles(below_diagonal(c * FAST_TILES_PER_TRIP, FAST_TILES_PER_TRIP))
            return carry

        n_below = i * n_diag
        n_trips = n_below // FAST_TILES_PER_TRIP
        lax.fori_loop(0, n_trips, body, 0)
        for rest in range(0, FAST_TILES_PER_TRIP, n_diag):
            @pl.when(n_below - n_trips * FAST_TILES_PER_TRIP == rest)
            def _():
                fast_tiles(below_diagonal(n_trips * FAST_TILES_PER_TRIP, rest)
                           + [(n_below + c, c) for c in range(n_diag)])
                finish()

    @pl.when(fast_ref[0] == 0)
    def _():
        ms = tuple(jnp.full((1, tq), MASK_VALUE, F32) for _ in range(hg))
        ms = lax.fori_loop(0, i * n_diag, lambda j, c: safe_step(j, c), ms)
        for c in range(n_diag):
            ms = safe_step(i * n_diag + c, ms, diag=c)
        finish()


def _col(v, rows=None):
    v = v.astype(F32)
    if rows is not None and rows > v.shape[0]:
        v = jnp.concatenate([v, jnp.zeros((rows - v.shape[0],), F32)])
    return v[:, None]


def _params(sem):
    return pltpu.CompilerParams(dimension_semantics=sem, vmem_limit_bytes=VMEM_LIMIT_BYTES)


def _layer(x, mem, norm_g, w_in, b_f, w_pool, pool_scale, fox_q_g, fox_k_g,
           mem_norm_g, w_mem_kv, mem_q_g, mem_k_g, w_out):
    bsz, seq, dm = x.shape
    n_mem = mem.shape[1]
    n_fox = b_f.shape[0]
    pool_w = pool_scale.shape[0]
    fox_w = n_fox * HEAD_DIM
    mem_w = MEM_HEADS * HEAD_DIM
    assert n_fox <= BF16_ROWS and pool_w == len(POOL_WINDOWS) * POOL_GROUP_DIM
    assert w_in.shape[1] == 2 * pool_w + 4 * fox_w + n_fox + 2 * mem_w
    assert w_out.shape[0] == pool_w + fox_w + mem_w
    ts, tq, tk = SEQ_TILE, Q_TILE, KV_TILE
    assert seq % ts == 0 and seq % tq == 0 and tq % tk == 0 and ts % tk == 0 and ts % CUMSUM_BLOCK == 0
    assert FAST_TILES_PER_TRIP % (tq // tk) == 0

    sizes = (pool_w, pool_w, fox_w, fox_w, fox_w, n_fox, fox_w, mem_w, mem_w)
    w_in = w_in.astype(BF16)
    parts, off = [], 0
    for w in sizes:
        parts.append(w_in[:, off:off + w])
        off += w
    w_ua, w_ga, w_q, w_k, w_v, w_f, w_gb, w_qm, w_gm = parts
    w_f = jnp.concatenate([w_f, jnp.zeros((dm, BF16_ROWS - n_fox), w_in.dtype)], axis=1)
    order = [("ua", w_ua), ("ga", w_ga), ("f", w_f), ("q", w_q), ("k", w_k), ("v", w_v),
             ("gb", w_gb), ("qm", w_qm), ("gm", w_gm)]
    offs, off = {}, 0
    for name, w in order:
        offs[name] = off
        off += w.shape[1]
    wT = jnp.concatenate([w for _, w in order], axis=1).T
    n_rows = wT.shape[0]
    tri = (lax.broadcasted_iota(jnp.int32, (CUMSUM_BLOCK, CUMSUM_BLOCK), 0)
           <= lax.broadcasted_iota(jnp.int32, (CUMSUM_BLOCK, CUMSUM_BLOCK), 1)).astype(BF16)
    wp_bd = jax.scipy.linalg.block_diag(*[w_pool[g].T for g in range(len(POOL_WINDOWS))]).astype(BF16)
    wkvT = w_mem_kv.T.astype(BF16)
    wo = w_out.astype(BF16)

    nb = math.gcd(MEM_BATCH_ROWS_PER_STEP, bsz)
    km, vm = pl.pallas_call(
        _mem_kv_kernel,
        grid=(bsz // nb,),
        in_specs=[
            pl.BlockSpec((nb, n_mem, dm), lambda b: (b, 0, 0)),
            pl.BlockSpec((1, dm), lambda b: (0, 0)),
            pl.BlockSpec((2 * mem_w, dm), lambda b: (0, 0)),
            pl.BlockSpec((HEAD_DIM, 1), lambda b: (0, 0)),
        ],
        out_specs=[
            pl.BlockSpec((nb, MEM_HEADS, n_mem, QK_ROWS), lambda b: (b, 0, 0, 0)),
            pl.BlockSpec((nb, MEM_HEADS, MEM_V_ROWS, n_mem), lambda b: (b, 0, 0, 0)),
        ],
        out_shape=[
            jax.ShapeDtypeStruct((bsz, MEM_HEADS, n_mem, QK_ROWS), BF16),
            jax.ShapeDtypeStruct((bsz, MEM_HEADS, MEM_V_ROWS, n_mem), BF16),
        ],
        compiler_params=_params(("arbitrary",)),
        name="mem_kv",
    )(mem, mem_norm_g[None, :], wkvT, _col(mem_k_g))

    const2 = lambda b, s: (0, 0)
    m_bound = (HEAD_DIM * SCALE * LOG2E * BOUND_SLACK
               * jnp.max(jnp.abs(fox_q_g)) * jnp.max(jnp.abs(fox_k_g))).astype(F32).reshape(1)
    fast = (m_bound < FAST_MAX_SHIFT).astype(jnp.int32)
    qT, kp, vT, gbT, mam = pl.pallas_call(
        functools.partial(_proj_kernel, offs=offs, n_fox=n_fox, pool_w=pool_w,
                          fox_w=fox_w, mem_w=mem_w),
        grid=(bsz, seq // ts),
        in_specs=[
            pl.BlockSpec(memory_space=pltpu.SMEM),
            pl.BlockSpec((1, ts, dm), lambda b, s: (b, s, 0)),
            pl.BlockSpec((1, dm), const2),
            pl.BlockSpec((n_rows, dm), const2),
            pl.BlockSpec((BF16_ROWS, 1), const2),
            pl.BlockSpec((CUMSUM_BLOCK, CUMSUM_BLOCK), const2),
            pl.BlockSpec((HEAD_DIM, 1), const2),
            pl.BlockSpec((HEAD_DIM, 1), const2),
            pl.BlockSpec((pool_w, pool_w), const2),
            pl.BlockSpec((pool_w, 1), const2),
            pl.BlockSpec((1, MEM_HEADS, n_mem, QK_ROWS), lambda b, s: (b, 0, 0, 0)),
            pl.BlockSpec((1, MEM_HEADS, MEM_V_ROWS, n_mem), lambda b, s: (b, 0, 0, 0)),
            pl.BlockSpec((HEAD_DIM, 1), const2),
        ],
        out_specs=[
            pl.BlockSpec((1, n_fox, QK_ROWS, ts), lambda b, s: (b, 0, 0, s)),
            pl.BlockSpec((1, n_fox, ts, QK_ROWS), lambda b, s: (b, 0, s, 0)),
            pl.BlockSpec((1, n_fox, ts // tk, HEAD_DIM, tk), lambda b, s: (b, 0, s, 0, 0)),
            pl.BlockSpec((1, fox_w, ts), lambda b, s: (b, 0, s)),
            pl.BlockSpec((1, pool_w + mem_w, ts), lambda b, s: (b, 0, s)),
        ],
        out_shape=[
            jax.ShapeDtypeStruct((bsz, n_fox, QK_ROWS, seq), BF16),
            jax.ShapeDtypeStruct((bsz, n_fox, seq, QK_ROWS), BF16),
            jax.ShapeDtypeStruct((bsz, n_fox, seq // tk, HEAD_DIM, tk), BF16),
            jax.ShapeDtypeStruct((bsz, fox_w, seq), F32),
            jax.ShapeDtypeStruct((bsz, pool_w + mem_w, seq), BF16),
        ],
        scratch_shapes=[
            pltpu.VMEM((BF16_ROWS, LANES), F32),
            pltpu.VMEM((pool_w, LANES), F32),
        ],
        compiler_params=_params(("arbitrary", "arbitrary")),
        name="proj",
    )(m_bound, x, norm_g[None, :], wT, _col(b_f, BF16_ROWS), tri, _col(fox_q_g), _col(fox_k_g),
      wp_bd, _col(pool_scale), km, vm, _col(mem_q_g))

    return pl.pallas_call(
        functools.partial(_fox_out_kernel, pool_w=pool_w),
        grid=(bsz, seq // tq),
        in_specs=[
            pl.BlockSpec(memory_space=pltpu.SMEM),
            pl.BlockSpec((1, n_fox, QK_ROWS, tq), lambda b, i: (b, 0, 0, i)),
            pl.BlockSpec((1, n_fox, seq, QK_ROWS), lambda b, i: (b, 0, 0, 0)),
            pl.BlockSpec((1, n_fox, seq // tk, HEAD_DIM, tk), lambda b, i: (b, 0, 0, 0, 0)),
            pl.BlockSpec((1, fox_w, tq), lambda b, i: (b, 0, i)),
            pl.BlockSpec((1, pool_w + mem_w, tq), lambda b, i: (b, 0, i)),
            pl.BlockSpec((1, tq, dm), lambda b, i: (b, i, 0)),
            pl.BlockSpec((pool_w + fox_w + mem_w, dm), const2),
        ],
        out_specs=pl.BlockSpec((1, tq, dm), lambda b, i: (b, i, 0)),
        out_shape=jax.ShapeDtypeStruct((bsz, seq, dm), x.dtype),
        scratch_shapes=[pltpu.VMEM((n_fox, HEAD_DIM, tq), F32), pltpu.VMEM((n_fox, SUBLANES, tq), F32)],
        compiler_params=_params(("arbitrary", "arbitrary")),
        name="fox_out",
    )(fast, qT, kp, vT, gbT, mam, x, wo)


def kernel(x, mem, norm_g, w_in, b_f, w_pool, pool_scale, fox_q_g, fox_k_g, mem_norm_g,
           w_mem_kv, mem_q_g, mem_k_g, w_out):
    for l in range(norm_g.shape[0]):
        x = _layer(x, mem, norm_g[l], w_in[l], b_f[l], w_pool[l], pool_scale[l], fox_q_g[l],
                   fox_k_g[l], mem_norm_g[l], w_mem_kv[l], mem_q_g[l], mem_k_g[l], w_out[l])
    return x
```

```python
import functools
import math

import jax
import jax.numpy as jnp
from jax import lax
from jax.experimental import pallas as pl
from jax.experimental.pallas import tpu as pltpu

F32 = jnp.float32
BF16 = jnp.bfloat16

HEAD_DIM = 64
EPS = 1e-6
POOL_WINDOWS = (2, 4, 8, 16)
POOL_GROUP_DIM = 64
MEM_HEADS = 4
SCALE = 1.0 / math.sqrt(HEAD_DIM)
LOG2E = math.log2(math.e)
MASK_VALUE = -1e30

LANES = 128
SUBLANES = 8
BF16_ROWS = 16
QK_ROWS = 128
AUG_ROWS = BF16_ROWS
MEM_V_ROWS = HEAD_DIM + BF16_ROWS

MEM_BATCH_ROWS_PER_STEP = 4
SEQ_TILE = 1024
CUMSUM_BLOCK = 256
Q_TILE = 512
KV_TILE = 256
FAST_TILES_PER_TRIP = 6
FAST_LOOKAHEAD = 3
FAST_MAX_SHIFT = 50.0
BOUND_SLACK = 1.01
VMEM_LIMIT_BYTES = 56 * 1024 * 1024

NT_DIMS = (((1,), (1,)), ((), ()))
TN_DIMS = (((0,), (0,)), ((), ()))


def _dot(a, b):
    return jnp.dot(a, b, preferred_element_type=F32)


def _dot_nt(a, b):
    return lax.dot_general(a, b, NT_DIMS, preferred_element_type=F32)


def _dot_tn(a, b):
    return lax.dot_general(a, b, TN_DIMS, preferred_element_type=F32)


def _silu(g):
    return g * jax.nn.sigmoid(g)


def _log_sigmoid(z):
    return jnp.minimum(z, 0.0) - jnp.log1p(jnp.exp(-jnp.abs(z)))


def _head_rms_scale(t):
    return lax.rsqrt(jnp.mean(t * t, axis=0, keepdims=True) + EPS)


def _ones_row_block(n):
    row = lax.broadcasted_iota(jnp.int32, (BF16_ROWS, n), 0)
    return (row == 0).astype(F32)


def _mem_kv_kernel(mem_ref, g_ref, wkvT_ref, gk_ref, km_ref, vm_ref):
    nb, n_mem, dm = mem_ref.shape
    x = mem_ref[...].reshape(nb * n_mem, dm)
    h = (x * lax.rsqrt(jnp.mean(x * x, axis=-1, keepdims=True) + EPS) * g_ref[...]).astype(BF16)
    kvT = _dot_nt(wkvT_ref[...], h)
    width = MEM_HEADS * HEAD_DIM
    ones_blk = _ones_row_block(n_mem)
    zero_pad = jnp.zeros((QK_ROWS - HEAD_DIM, n_mem), F32)
    for hm in range(MEM_HEADS):
        kh = kvT[hm * HEAD_DIM:(hm + 1) * HEAD_DIM]
        kn = kh * _head_rms_scale(kh) * gk_ref[...]
        vh = kvT[width + hm * HEAD_DIM:width + (hm + 1) * HEAD_DIM]
        for bi in range(nb):
            cols = slice(bi * n_mem, (bi + 1) * n_mem)
            km_ref[bi, hm] = jnp.concatenate([kn[:, cols], zero_pad], axis=0).T.astype(BF16)
            vm_ref[bi, hm] = jnp.concatenate([vh[:, cols], ones_blk], axis=0).astype(BF16)


def _proj_kernel(mb_ref, x_ref, ng_ref, wT_ref, bf_ref, tri_ref, gq_ref, gk_ref, wp_ref, ps_ref,
                 km_ref, vm_ref, gmq_ref,
                 qT_ref, kp_ref, vT_ref, gbT_ref, mam_ref,
                 fcarry_ref, halo_ref, *, offs, n_fox, pool_w, fox_w, mem_w):
    ts = x_ref.shape[1]
    s_idx = pl.program_id(1)

    @pl.when(s_idx == 0)
    def _():
        fcarry_ref[...] = jnp.zeros_like(fcarry_ref)
        halo_ref[...] = jnp.zeros_like(halo_ref)

    x = x_ref[0]
    h = (x * lax.rsqrt(jnp.mean(x * x, axis=-1, keepdims=True) + EPS) * ng_ref[...]).astype(BF16)

    def proj_t(lo, n):
        return _dot_nt(wT_ref[lo:lo + n, :], h)

    tk = vT_ref.shape[4]
    gd = POOL_GROUP_DIM

    sec_m = proj_t(offs["qm"], 2 * mem_w)
    qm = sec_m[0:mem_w]
    gm = sec_m[mem_w:2 * mem_w]
    sec_a = proj_t(offs["ua"], 2 * pool_w + BF16_ROWS)
    u = sec_a[0:pool_w]
    ga = sec_a[pool_w:2 * pool_w]
    z = sec_a[2 * pool_w:2 * pool_w + BF16_ROWS] + bf_ref[...]

    qT = proj_t(offs["q"], fox_w)

    zero_pad = jnp.zeros((QK_ROWS - HEAD_DIM, ts), F32)
    mq_gain = gmq_ref[...] * (SCALE * LOG2E)
    lgs = []
    for hm in range(MEM_HEADS):
        qh = qm[hm * HEAD_DIM:(hm + 1) * HEAD_DIM]
        qn = qh * (_head_rms_scale(qh) * mq_gain)
        qpad = jnp.concatenate([qn, zero_pad], axis=0).astype(BF16)
        lgs.append(_dot(km_ref[0, hm], qpad))

    logf = _log_sigmoid(z)
    hi = logf.astype(BF16)
    r1 = logf - hi.astype(F32)
    mid = r1.astype(BF16)
    lo = (r1 - mid.astype(F32)).astype(BF16)
    parts = jnp.concatenate([hi, mid, lo], axis=0)
    cb = tri_ref.shape[0]
    kT = proj_t(offs["k"], fox_w)
    cs = [_dot(parts[:, c:c + cb], tri_ref[...]) for c in range(0, ts, cb)]
    vT = proj_t(offs["v"], fox_w)

    for hm in range(MEM_HEADS):
        lg = lgs[hm]
        p = jnp.exp2(lg - jnp.max(lg, axis=0, keepdims=True)).astype(BF16)
        yv = _dot(vm_ref[0, hm], p)
        y = yv[0:HEAD_DIM] / yv[HEAD_DIM:HEAD_DIM + 1]
        mm = y * _silu(gm[hm * HEAD_DIM:(hm + 1) * HEAD_DIM])
        mam_ref[0, pool_w + hm * HEAD_DIM:pool_w + (hm + 1) * HEAD_DIM, :] = mm.astype(BF16)

    uext = jnp.concatenate([halo_ref[...], u], axis=1)
    halo_ref[...] = u[:, ts - LANES:]
    pos1 = (s_idx * ts + lax.broadcasted_iota(jnp.int32, (1, ts), 1) + 1).astype(F32)
    acc = uext
    pooled = []
    shift = 1
    for g, w in enumerate(POOL_WINDOWS):
        while shift < w:
            acc = acc + pltpu.roll(acc, shift, 1)
            shift *= 2
        pooled.append(acc[0:gd, LANES:] / jnp.minimum(pos1, float(w)))
        acc = acc[gd:]
    d = (jnp.concatenate(pooled, axis=0) - u).astype(BF16)
    ya = _dot(wp_ref[...], d) * ps_ref[...]
    mam_ref[0, 0:pool_w, :] = (ya * _silu(ga)).astype(BF16)

    gbT_ref[0] = proj_t(offs["gb"], fox_w)

    run = fcarry_ref[:, LANES - 1:LANES]
    blocks = []
    for c in cs:
        blocks.append(c[0:BF16_ROWS] + c[BF16_ROWS:2 * BF16_ROWS] + c[2 * BF16_ROWS:3 * BF16_ROWS] + run)
        run = blocks[-1][:, cb - 1:cb]
    fcum = jnp.concatenate(blocks, axis=1)
    fcarry_ref[...] = fcum[:, ts - LANES:]
    f2 = fcum * LOG2E

    arow = lax.broadcasted_iota(jnp.int32, (AUG_ROWS, ts), 0)

    def split3(v):
        v_hi = v.astype(BF16).astype(F32)
        r = v - v_hi
        v_mid = r.astype(BF16).astype(F32)
        return v_hi, v_mid, (r - v_mid).astype(BF16).astype(F32)

    def rows(*vals):
        out = jnp.zeros((AUG_ROWS, ts), F32)
        for r, v in enumerate(vals):
            out = jnp.where(arow == r, v, out)
        return out

    zero_rows = jnp.zeros((QK_ROWS - HEAD_DIM - AUG_ROWS, ts), F32)
    q_gain = gq_ref[...] * (SCALE * LOG2E)
    for hd in range(n_fox):
        qh = qT[hd * HEAD_DIM:(hd + 1) * HEAD_DIM]
        qn = qh * (_head_rms_scale(qh) * q_gain)
        f3 = split3(f2[hd:hd + 1])
        qT_ref[0, hd, 0:HEAD_DIM, :] = qn.astype(BF16)
        qT_ref[0, hd, HEAD_DIM:HEAD_DIM + AUG_ROWS, :] = rows(
            f3[0], f3[1], f3[2], 1.0, 1.0, 1.0, -mb_ref[0]).astype(BF16)
        qT_ref[0, hd, HEAD_DIM + AUG_ROWS:, :] = zero_rows.astype(BF16)

        kh = kT[hd * HEAD_DIM:(hd + 1) * HEAD_DIM]
        kn = kh * (_head_rms_scale(kh) * gk_ref[...])
        k_aug = rows(1.0, 1.0, 1.0, -f3[0], -f3[1], -f3[2], 1.0)
        kfull = jnp.concatenate([kn, k_aug, zero_rows], axis=0)
        kp_ref[0, hd] = kfull.T.astype(BF16)

    for hd in range(n_fox):
        va = vT[hd * HEAD_DIM:(hd + 1) * HEAD_DIM].astype(BF16)
        for c in range(ts // tk):
            vT_ref[0, hd, c] = va[:, c * tk:(c + 1) * tk]


def _fox_out_kernel(fast_ref, qT_ref, kp_ref, vT_ref, gbT_ref, mam_ref, x_ref, wo_ref, o_ref,
                    acc_ref, den_ref, *, pool_w):
    hg = qT_ref.shape[1]
    tq = qT_ref.shape[3]
    tk = vT_ref.shape[4]
    n_diag = tq // tk
    i = pl.program_id(1)

    acc_ref[...] = jnp.zeros_like(acc_ref)
    den_ref[...] = jnp.zeros_like(den_ref)
    causal = (lax.broadcasted_iota(jnp.int32, (tk, tq), 0)
              <= lax.broadcasted_iota(jnp.int32, (tk, tq), 1))

    def head_scores(hd, j, lo=0):
        start = pl.multiple_of(j * tk, tk)
        return _dot(kp_ref[0, hd, pl.ds(start, tk), :], qT_ref[0, hd, :, lo:])

    def mask(s, diag, lo):
        return s if diag is None else jnp.where(causal[:, :tq - lo], s, MASK_VALUE)

    def fast_tiles(tiles):
        units = [(j, diag, hd) for (j, diag) in tiles for hd in range(hg)]
        scores = {}

        def issue(u):
            j, diag, hd = units[u]
            scores[u] = head_scores(hd, j, 0 if diag is None else diag * tk)

        for u in range(min(FAST_LOOKAHEAD, len(units))):
            issue(u)
        for u, (j, diag, hd) in enumerate(units):
            lo = 0 if diag is None else diag * tk
            p = jnp.exp2(mask(scores.pop(u), diag, lo))
            den_ref[hd, :, lo:] += jnp.sum(p, axis=0, keepdims=True)
            acc_ref[hd, :, lo:] += _dot(vT_ref[0, hd, j], p.astype(BF16))
            if u + FAST_LOOKAHEAD < len(units):
                issue(u + FAST_LOOKAHEAD)

    def safe_step(j, ms, diag=None):
        lo = 0 if diag is None else diag * tk
        scores = [head_scores(hd, j, lo) for hd in range(hg)]
        out = []
        for hd in range(hg):
            s = mask(scores[hd], diag, lo)
            m_old = ms[hd][:, lo:]
            m_new = jnp.maximum(m_old, jnp.max(s, axis=0, keepdims=True))
            p = jnp.exp2(s - m_new)
            alpha = jnp.exp2(m_old - m_new)
            den_ref[hd, :, lo:] = alpha * den_ref[hd, :, lo:] + jnp.sum(p, axis=0, keepdims=True)
            acc_ref[hd, :, lo:] = alpha * acc_ref[hd, :, lo:] + _dot(vT_ref[0, hd, j], p.astype(BF16))
            out.append(m_new if lo == 0 else jnp.concatenate([ms[hd][:, :lo], m_new], axis=1))
        return tuple(out)

    def finish():
        mam = mam_ref[0]
        fox_w = hg * HEAD_DIM
        part = (_dot_tn(mam[0:pool_w], wo_ref[0:pool_w, :])
                + _dot_tn(mam[pool_w:], wo_ref[pool_w + fox_w:, :]))
        ys = []
        for hd in range(hg):
            ys.append(acc_ref[hd] / den_ref[hd, 0:1, :])
        yb = (jnp.concatenate(ys, axis=0) * _silu(gbT_ref[0])).astype(BF16)
        o_ref[0] = x_ref[0] + part + _dot_tn(yb, wo_ref[pool_w:pool_w + fox_w, :])

    @pl.when(fast_ref[0] != 0)
    def _():
        def below_diagonal(first, count):
            return [(first + d, None) for d in range(count)]

        def body(c, carry):
            fast_tiles(below_diagonal(c * FAST_TILES_PER_TRIP, FAST_TILES_PER_TRIP))
            return carry

        n_below = i * n_diag
        n_trips = n_below // FAST_TILES_PER_TRIP
        lax.fori_loop(0, n_trips, body, 0)
        for rest in range(0, FAST_TILES_PER_TRIP, n_diag):
            @pl.when(n_below - n_trips * FAST_TILES_PER_TRIP == rest)
            def _():
                fast_tiles(below_diagonal(n_trips * FAST_TILES_PER_TRIP, rest)
                           + [(n_below + c, c) for c in range(n_diag)])
                finish()

    @pl.when(fast_ref[0] == 0)
    def _():
        ms = tuple(jnp.full((1, tq), MASK_VALUE, F32) for _ in range(hg))
        ms = lax.fori_loop(0, i * n_diag, lambda j, c: safe_step(j, c), ms)
        for c in range(n_diag):
            ms = safe_step(i * n_diag + c, ms, diag=c)
        finish()


def _col(v, rows=None):
    v = v.astype(F32)
    if rows is not None and rows > v.shape[0]:
        v = jnp.concatenate([v, jnp.zeros((rows - v.shape[0],), F32)])
    return v[:, None]


def _params(sem):
    return pltpu.CompilerParams(dimension_semantics=sem, vmem_limit_bytes=VMEM_LIMIT_BYTES)


def _layer(x, mem, norm_g, w_in, b_f, w_pool, pool_scale, fox_q_g, fox_k_g,
           mem_norm_g, w_mem_kv, mem_q_g, mem_k_g, w_out):
    bsz, seq, dm = x.shape
    n_mem = mem.shape[1]
    n_fox = b_f.shape[0]
    pool_w = pool_scale.shape[0]
    fox_w = n_fox * HEAD_DIM
    mem_w = MEM_HEADS * HEAD_DIM
    assert n_fox <= BF16_ROWS and pool_w == len(POOL_WINDOWS) * POOL_GROUP_DIM
    assert w_in.shape[1] == 2 * pool_w + 4 * fox_w + n_fox + 2 * mem_w
    assert w_out.shape[0] == pool_w + fox_w + mem_w
    ts, tq, tk = SEQ_TILE, Q_TILE, KV_TILE
    assert seq % ts == 0 and seq % tq == 0 and tq % tk == 0 and ts % tk == 0 and ts % CUMSUM_BLOCK == 0
    assert FAST_TILES_PER_TRIP % (tq // tk) == 0

    sizes = (pool_w, pool_w, fox_w, fox_w, fox_w, n_fox, fox_w, mem_w, mem_w)
    w_in = w_in.astype(BF16)
    parts, off = [], 0
    for w in sizes:
        parts.append(w_in[:, off:off + w])
        off += w
    w_ua, w_ga, w_q, w_k, w_v, w_f, w_gb, w_qm, w_gm = parts
    w_f = jnp.concatenate([w_f, jnp.zeros((dm, BF16_ROWS - n_fox), w_in.dtype)], axis=1)
    order = [("ua", w_ua), ("ga", w_ga), ("f", w_f), ("q", w_q), ("k", w_k), ("v", w_v),
             ("gb", w_gb), ("qm", w_qm), ("gm", w_gm)]
    offs, off = {}, 0
    for name, w in order:
        offs[name] = off
        off += w.shape[1]
    wT = jnp.concatenate([w for _, w in order], axis=1).T
    n_rows = wT.shape[0]
    tri = (lax.broadcasted_iota(jnp.int32, (CUMSUM_BLOCK, CUMSUM_BLOCK), 0)
           <= lax.broadcasted_iota(jnp.int32, (CUMSUM_BLOCK, CUMSUM_BLOCK), 1)).astype(BF16)
    wp_bd = jax.scipy.linalg.block_diag(*[w_pool[g].T for g in range(len(POOL_WINDOWS))]).astype(BF16)
    wkvT = w_mem_kv.T.astype(BF16)
    wo = w_out.astype(BF16)

    nb = math.gcd(MEM_BATCH_ROWS_PER_STEP, bsz)
    km, vm = pl.pallas_call(
        _mem_kv_kernel,
        grid=(bsz // nb,),
        in_specs=[
            pl.BlockSpec((nb, n_mem, dm), lambda b: (b, 0, 0)),
            pl.BlockSpec((1, dm), lambda b: (0, 0)),
            pl.BlockSpec((2 * mem_w, dm), lambda b: (0, 0)),
            pl.BlockSpec((HEAD_DIM, 1), lambda b: (0, 0)),
        ],
        out_specs=[
            pl.BlockSpec((nb, MEM_HEADS, n_mem, QK_ROWS), lambda b: (b, 0, 0, 0)),
            pl.BlockSpec((nb, MEM_HEADS, MEM_V_ROWS, n_mem), lambda b: (b, 0, 0, 0)),
        ],
        out_shape=[
            jax.ShapeDtypeStruct((bsz, MEM_HEADS, n_mem, QK_ROWS), BF16),
            jax.ShapeDtypeStruct((bsz, MEM_HEADS, MEM_V_ROWS, n_mem), BF16),
        ],
        compiler_params=_params(("arbitrary",)),
        name="mem_kv",
    )(mem, mem_norm_g[None, :], wkvT, _col(mem_k_g))

    const2 = lambda b, s: (0, 0)
    m_bound = (HEAD_DIM * SCALE * LOG2E * BOUND_SLACK
               * jnp.max(jnp.abs(fox_q_g)) * jnp.max(jnp.abs(fox_k_g))).astype(F32).reshape(1)
    fast = (m_bound < FAST_MAX_SHIFT).astype(jnp.int32)
    qT, kp, vT, gbT, mam = pl.pallas_call(
        functools.partial(_proj_kernel, offs=offs, n_fox=n_fox, pool_w=pool_w,
                          fox_w=fox_w, mem_w=mem_w),
        grid=(bsz, seq // ts),
        in_specs=[
            pl.BlockSpec(memory_space=pltpu.SMEM),
            pl.BlockSpec((1, ts, dm), lambda b, s: (b, s, 0)),
            pl.BlockSpec((1, dm), const2),
            pl.BlockSpec((n_rows, dm), const2),
            pl.BlockSpec((BF16_ROWS, 1), const2),
            pl.BlockSpec((CUMSUM_BLOCK, CUMSUM_BLOCK), const2),
            pl.BlockSpec((HEAD_DIM, 1), const2),
            pl.BlockSpec((HEAD_DIM, 1), const2),
            pl.BlockSpec((pool_w, pool_w), const2),
            pl.BlockSpec((pool_w, 1), const2),
            pl.BlockSpec((1, MEM_HEADS, n_mem, QK_ROWS), lambda b, s: (b, 0, 0, 0)),
            pl.BlockSpec((1, MEM_HEADS, MEM_V_ROWS, n_mem), lambda b, s: (b, 0, 0, 0)),
            pl.BlockSpec((HEAD_DIM, 1), const2),
        ],
        out_specs=[
            pl.BlockSpec((1, n_fox, QK_ROWS, ts), lambda b, s: (b, 0, 0, s)),
            pl.BlockSpec((1, n_fox, ts, QK_ROWS), lambda b, s: (b, 0, s, 0)),
            pl.BlockSpec((1, n_fox, ts // tk, HEAD_DIM, tk), lambda b, s: (b, 0, s, 0, 0)),
            pl.BlockSpec((1, fox_w, ts), lambda b, s: (b, 0, s)),
            pl.BlockSpec((1, pool_w + mem_w, ts), lambda b, s: (b, 0, s)),
        ],
        out_shape=[
            jax.ShapeDtypeStruct((bsz, n_fox, QK_ROWS, seq), BF16),
            jax.ShapeDtypeStruct((bsz, n_fox, seq, QK_ROWS), BF16),
            jax.ShapeDtypeStruct((bsz, n_fox, seq // tk, HEAD_DIM, tk), BF16),
            jax.ShapeDtypeStruct((bsz, fox_w, seq), F32),
            jax.ShapeDtypeStruct((bsz, pool_w + mem_w, seq), BF16),
        ],
        scratch_shapes=[
            pltpu.VMEM((BF16_ROWS, LANES), F32),
            pltpu.VMEM((pool_w, LANES), F32),
        ],
        compiler_params=_params(("arbitrary", "arbitrary")),
        name="proj",
    )(m_bound, x, norm_g[None, :], wT, _col(b_f, BF16_ROWS), tri, _col(fox_q_g), _col(fox_k_g),
      wp_bd, _col(pool_scale), km, vm, _col(mem_q_g))

    return pl.pallas_call(
        functools.partial(_fox_out_kernel, pool_w=pool_w),
        grid=(bsz, seq // tq),
        in_specs=[
            pl.BlockSpec(memory_space=pltpu.SMEM),
            pl.BlockSpec((1, n_fox, QK_ROWS, tq), lambda b, i: (b, 0, 0, i)),
            pl.BlockSpec((1, n_fox, seq, QK_ROWS), lambda b, i: (b, 0, 0, 0)),
            pl.BlockSpec((1, n_fox, seq // tk, HEAD_DIM, tk), lambda b, i: (b, 0, 0, 0, 0)),
            pl.BlockSpec((1, fox_w, tq), lambda b, i: (b, 0, i)),
            pl.BlockSpec((1, pool_w + mem_w, tq), lambda b, i: (b, 0, i)),
            pl.BlockSpec((1, tq, dm), lambda b, i: (b, i, 0)),
            pl.BlockSpec((pool_w + fox_w + mem_w, dm), const2),
        ],
        out_specs=pl.BlockSpec((1, tq, dm), lambda b, i: (b, i, 0)),
        out_shape=jax.ShapeDtypeStruct((bsz, seq, dm), x.dtype),
        scratch_shapes=[pltpu.VMEM((n_fox, HEAD_DIM, tq), F32), pltpu.VMEM((n_fox, SUBLANES, tq), F32)],
        compiler_params=_params(("arbitrary", "arbitrary")),
        name="fox_out",
    )(fast, qT, kp, vT, gbT, mam, x, wo)


def kernel(x, mem, norm_g, w_in, b_f, w_pool, pool_scale, fox_q_g, fox_k_g, mem_norm_g,
           w_mem_kv, mem_q_g, mem_k_g, w_out):
    for l in range(norm_g.shape[0]):
        x = _layer(x, mem, norm_g[l], w_in[l], b_f[l], w_pool[l], pool_scale[l], fox_q_g[l],
                   fox_k_g[l], mem_norm_g[l], w_mem_kv[l], mem_q_g[l], mem_k_g[l], w_out[l])
    return x
```

```python
import functools
import math

import jax
import jax.numpy as jnp
from jax import lax
from jax.experimental import pallas as pl
from jax.experimental.pallas import tpu as pltpu

F32 = jnp.float32
BF16 = jnp.bfloat16

HEAD_DIM = 64
EPS = 1e-6
POOL_WINDOWS = (2, 4, 8, 16)
POOL_GROUP_DIM = 64
MEM_HEADS = 4
SCALE = 1.0 / math.sqrt(HEAD_DIM)
LOG2E = math.log2(math.e)
MASK_VALUE = -1e30

LANES = 128
SUBLANES = 8
BF16_ROWS = 16
QK_ROWS = 128
AUG_ROWS = BF16_ROWS
MEM_V_ROWS = HEAD_DIM + BF16_ROWS

MEM_BATCH_ROWS_PER_STEP = 4
SEQ_TILE = 1024
CUMSUM_BLOCK = 256
Q_TILE = 512
KV_TILE = 256
FAST_TILES_PER_TRIP = 6
FAST_LOOKAHEAD = 3
FAST_MAX_SHIFT = 50.0
BOUND_SLACK = 1.01
VMEM_LIMIT_BYTES = 56 * 1024 * 1024

NT_DIMS = (((1,), (1,)), ((), ()))
TN_DIMS = (((0,), (0,)), ((), ()))


def _dot(a, b):
    return jnp.dot(a, b, preferred_element_type=F32)


def _dot_nt(a, b):
    return lax.dot_general(a, b, NT_DIMS, preferred_element_type=F32)


def _dot_tn(a, b):
    return lax.dot_general(a, b, TN_DIMS, preferred_element_type=F32)


def _silu(g):
    return g * jax.nn.sigmoid(g)


def _log_sigmoid(z):
    return jnp.minimum(z, 0.0) - jnp.log1p(jnp.exp(-jnp.abs(z)))


def _head_rms_scale(t):
    return lax.rsqrt(jnp.mean(t * t, axis=0, keepdims=True) + EPS)


def _ones_row_block(n):
    row = lax.broadcasted_iota(jnp.int32, (BF16_ROWS, n), 0)
    return (row == 0).astype(F32)


def _mem_kv_kernel(mem_ref, g_ref, wkvT_ref, gk_ref, km_ref, vm_ref):
    nb, n_mem, dm = mem_ref.shape
    x = mem_ref[...].reshape(nb * n_mem, dm)
    h = (x * lax.rsqrt(jnp.mean(x * x, axis=-1, keepdims=True) + EPS) * g_ref[...]).astype(BF16)
    kvT = _dot_nt(wkvT_ref[...], h)
    width = MEM_HEADS * HEAD_DIM
    ones_blk = _ones_row_block(n_mem)
    zero_pad = jnp.zeros((QK_ROWS - HEAD_DIM, n_mem), F32)
    for hm in range(MEM_HEADS):
        kh = kvT[hm * HEAD_DIM:(hm + 1) * HEAD_DIM]
        kn = kh * _head_rms_scale(kh) * gk_ref[...]
        vh = kvT[width + hm * HEAD_DIM:width + (hm + 1) * HEAD_DIM]
        for bi in range(nb):
            cols = slice(bi * n_mem, (bi + 1) * n_mem)
            km_ref[bi, hm] = jnp.concatenate([kn[:, cols], zero_pad], axis=0).T.astype(BF16)
            vm_ref[bi, hm] = jnp.concatenate([vh[:, cols], ones_blk], axis=0).astype(BF16)


def _proj_kernel(mb_ref, x_ref, ng_ref, wT_ref, bf_ref, tri_ref, gq_ref, gk_ref, wp_ref, ps_ref,
                 km_ref, vm_ref, gmq_ref,
                 qT_ref, kp_ref, vT_ref, gbT_ref, mam_ref,
                 fcarry_ref, halo_ref, *, offs, n_fox, pool_w, fox_w, mem_w):
    ts = x_ref.shape[1]
    s_idx = pl.program_id(1)

    @pl.when(s_idx == 0)
    def _():
        fcarry_ref[...] = jnp.zeros_like(fcarry_ref)
        halo_ref[...] = jnp.zeros_like(halo_ref)

    x = x_ref[0]
    h = (x * lax.rsqrt(jnp.mean(x * x, axis=-1, keepdims=True) + EPS) * ng_ref[...]).astype(BF16)

    def proj_t(lo, n):
        return _dot_nt(wT_ref[lo:lo + n, :], h)

    tk = vT_ref.shape[4]
    gd = POOL_GROUP_DIM

    sec_m = proj_t(offs["qm"], 2 * mem_w)
    qm = sec_m[0:mem_w]
    gm = sec_m[mem_w:2 * mem_w]
    sec_a = proj_t(offs["ua"], 2 * pool_w + BF16_ROWS)
    u = sec_a[0:pool_w]
    ga = sec_a[pool_w:2 * pool_w]
    z = sec_a[2 * pool_w:2 * pool_w + BF16_ROWS] + bf_ref[...]

    qT = proj_t(offs["q"], fox_w)

    zero_pad = jnp.zeros((QK_ROWS - HEAD_DIM, ts), F32)
    mq_gain = gmq_ref[...] * (SCALE * LOG2E)
    lgs = []
    for hm in range(MEM_HEADS):
        qh = qm[hm * HEAD_DIM:(hm + 1) * HEAD_DIM]
        qn = qh * (_head_rms_scale(qh) * mq_gain)
        qpad = jnp.concatenate([qn, zero_pad], axis=0).astype(BF16)
        lgs.append(_dot(km_ref[0, hm], qpad))

    logf = _log_sigmoid(z)
    hi = logf.astype(BF16)
    r1 = logf - hi.astype(F32)
    mid = r1.astype(BF16)
    lo = (r1 - mid.astype(F32)).astype(BF16)
    parts = jnp.concatenate([hi, mid, lo], axis=0)
    cb = tri_ref.shape[0]
    kT = proj_t(offs["k"], fox_w)
    cs = [_dot(parts[:, c:c + cb], tri_ref[...]) for c in range(0, ts, cb)]
    vT = proj_t(offs["v"], fox_w)

    for hm in range(MEM_HEADS):
        lg = lgs[hm]
        p = jnp.exp2(lg - jnp.max(lg, axis=0, keepdims=True)).astype(BF16)
        yv = _dot(vm_ref[0, hm], p)
        y = yv[0:HEAD_DIM] / yv[HEAD_DIM:HEAD_DIM + 1]
        mm = y * _silu(gm[hm * HEAD_DIM:(hm + 1) * HEAD_DIM])
        mam_ref[0, pool_w + hm * HEAD_DIM:pool_w + (hm + 1) * HEAD_DIM, :] = mm.astype(BF16)

    uext = jnp.concatenate([halo_ref[...], u], axis=1)
    halo_ref[...] = u[:, ts - LANES:]
    pos1 = (s_idx * ts + lax.broadcasted_iota(jnp.int32, (1, ts), 1) + 1).astype(F32)
    acc = uext
    pooled = []
    shift = 1
    for g, w in enumerate(POOL_WINDOWS):
        while shift < w:
            acc = acc + pltpu.roll(acc, shift, 1)
            shift *= 2
        pooled.append(acc[0:gd, LANES:] / jnp.minimum(pos1, float(w)))
        acc = acc[gd:]
    d = (jnp.concatenate(pooled, axis=0) - u).astype(BF16)
    ya = _dot(wp_ref[...], d) * ps_ref[...]
    mam_ref[0, 0:pool_w, :] = (ya * _silu(ga)).astype(BF16)

    gbT_ref[0] = proj_t(offs["gb"], fox_w)

    run = fcarry_ref[:, LANES - 1:LANES]
    blocks = []
    for c in cs:
        blocks.append(c[0:BF16_ROWS] + c[BF16_ROWS:2 * BF16_ROWS] + c[2 * BF16_ROWS:3 * BF16_ROWS] + run)
        run = blocks[-1][:, cb - 1:cb]
    fcum = jnp.concatenate(blocks, axis=1)
    fcarry_ref[...] = fcum[:, ts - LANES:]
    f2 = fcum * LOG2E

    arow = lax.broadcasted_iota(jnp.int32, (AUG_ROWS, ts), 0)

    def split3(v):
        v_hi = v.astype(BF16).astype(F32)
        r = v - v_hi
        v_mid = r.astype(BF16).astype(F32)
        return v_hi, v_mid, (r - v_mid).astype(BF16).astype(F32)

    def rows(*vals):
        out = jnp.zeros((AUG_ROWS, ts), F32)
        for r, v in enumerate(vals):
            out = jnp.where(arow == r, v, out)
        return out

    zero_rows = jnp.zeros((QK_ROWS - HEAD_DIM - AUG_ROWS, ts), F32)
    q_gain = gq_ref[...] * (SCALE * LOG2E)
    for hd in range(n_fox):
        qh = qT[hd * HEAD_DIM:(hd + 1) * HEAD_DIM]
        qn = qh * (_head_rms_scale(qh) * q_gain)
        f3 = split3(f2[hd:hd + 1])
        qT_ref[0, hd, 0:HEAD_DIM, :] = qn.astype(BF16)
        qT_ref[0, hd, HEAD_DIM:HEAD_DIM + AUG_ROWS, :] = rows(
            f3[0], f3[1], f3[2], 1.0, 1.0, 1.0, -mb_ref[0]).astype(BF16)
        qT_ref[0, hd, HEAD_DIM + AUG_ROWS:, :] = zero_rows.astype(BF16)

        kh = kT[hd * HEAD_DIM:(hd + 1) * HEAD_DIM]
        kn = kh * (_head_rms_scale(kh) * gk_ref[...])
        k_aug = rows(1.0, 1.0, 1.0, -f3[0], -f3[1], -f3[2], 1.0)
        kfull = jnp.concatenate([kn, k_aug, zero_rows], axis=0)
        kp_ref[0, hd] = kfull.T.astype(BF16)

    for hd in range(n_fox):
        va = vT[hd * HEAD_DIM:(hd + 1) * HEAD_DIM].astype(BF16)
        for c in range(ts // tk):
            vT_ref[0, hd, c] = va[:, c * tk:(c + 1) * tk]


def _fox_out_kernel(fast_ref, qT_ref, kp_ref, vT_ref, gbT_ref, mam_ref, x_ref, wo_ref, o_ref,
                    acc_ref, den_ref, *, pool_w):
    hg = qT_ref.shape[1]
    tq = qT_ref.shape[3]
    tk = vT_ref.shape[4]
    n_diag = tq // tk
    i = pl.program_id(1)

    acc_ref[...] = jnp.zeros_like(acc_ref)
    den_ref[...] = jnp.zeros_like(den_ref)
    causal = (lax.broadcasted_iota(jnp.int32, (tk, tq), 0)
              <= lax.broadcasted_iota(jnp.int32, (tk, tq), 1))

    def head_scores(hd, j, lo=0):
        start = pl.multiple_of(j * tk, tk)
        return _dot(kp_ref[0, hd, pl.ds(start, tk), :], qT_ref[0, hd, :, lo:])

    def mask(s, diag, lo):
        return s if diag is None else jnp.where(causal[:, :tq - lo], s, MASK_VALUE)

    def fast_tiles(tiles):
        units = [(j, diag, hd) for (j, diag) in tiles for hd in range(hg)]
        scores = {}

        def issue(u):
            j, diag, hd = units[u]
            scores[u] = head_scores(hd, j, 0 if diag is None else diag * tk)

        for u in range(min(FAST_LOOKAHEAD, len(units))):
            issue(u)
        for u, (j, diag, hd) in enumerate(units):
            lo = 0 if diag is None else diag * tk
            p = jnp.exp2(mask(scores.pop(u), diag, lo))
            den_ref[hd, :, lo:] += jnp.sum(p, axis=0, keepdims=True)
            acc_ref[hd, :, lo:] += _dot(vT_ref[0, hd, j], p.astype(BF16))
            if u + FAST_LOOKAHEAD < len(units):
                issue(u + FAST_LOOKAHEAD)

    def safe_step(j, ms, diag=None):
        lo = 0 if diag is None else diag * tk
        scores = [head_scores(hd, j, lo) for hd in range(hg)]
        out = []
        for hd in range(hg):
            s = mask(scores[hd], diag, lo)
            m_old = ms[hd][:, lo:]
            m_new = jnp.maximum(m_old, jnp.max(s, axis=0, keepdims=True))
            p = jnp.exp2(s - m_new)
            alpha = jnp.exp2(m_old - m_new)
            den_ref[hd, :, lo:] = alpha * den_ref[hd, :, lo:] + jnp.sum(p, axis=0, keepdims=True)
            acc_ref[hd, :, lo:] = alpha * acc_ref[hd, :, lo:] + _dot(vT_ref[0, hd, j], p.astype(BF16))
            out.append(m_new if lo == 0 else jnp.concatenate([ms[hd][:, :lo], m_new], axis=1))
        return tuple(out)

    def finish():
        mam = mam_ref[0]
        fox_w = hg * HEAD_DIM
        part = (_dot_tn(mam[0:pool_w], wo_ref[0:pool_w, :])
                + _dot_tn(mam[pool_w:], wo_ref[pool_w + fox_w:, :]))
        ys = []
        for hd in range(hg):
            ys.append(acc_ref[hd] / den_ref[hd, 0:1, :])
        yb = (jnp.concatenate(ys, axis=0) * _silu(gbT_ref[0])).astype(BF16)
        o_ref[0] = x_ref[0] + part + _dot_tn(yb, wo_ref[pool_w:pool_w + fox_w, :])

    @pl.when(fast_ref[0] != 0)
    def _():
        def below_diagonal(first, count):
            return [(first + d, None) for d in range(count)]

        def body(c, carry):
            fast_tiles(below_diagonal(c * FAST_TILES_PER_TRIP, FAST_TILES_PER_TRIP))
            return carry

        n_below = i * n_diag
        n_trips = n_below // FAST_TILES_PER_TRIP
        lax.fori_loop(0, n_trips, body, 0)
        for rest in range(0, FAST_TILES_PER_TRIP, n_diag):
            @pl.when(n_below - n_trips * FAST_TILES_PER_TRIP == rest)
            def _():
                fast_tiles(below_diagonal(n_trips * FAST_TILES_PER_TRIP, rest)
                           + [(n_below + c, c) for c in range(n_diag)])
                finish()

    @pl.when(fast_ref[0] == 0)
    def _():
        ms = tuple(jnp.full((1, tq), MASK_VALUE, F32) for _ in range(hg))
        ms = lax.fori_loop(0, i * n_diag, lambda j, c: safe_step(j, c), ms)
        for c in range(n_diag):
            ms = safe_step(i * n_diag + c, ms, diag=c)
        finish()


def _col(v, rows=None):
    v = v.astype(F32)
    if rows is not None and rows > v.shape[0]:
        v = jnp.concatenate([v, jnp.zeros((rows - v.shape[0],), F32)])
    return v[:, None]


def _params(sem):
    return pltpu.CompilerParams(dimension_semantics=sem, vmem_limit_bytes=VMEM_LIMIT_BYTES)


def _layer(x, mem, norm_g, w_in, b_f, w_pool, pool_scale, fox_q_g, fox_k_g,
           mem_norm_g, w_mem_kv, mem_q_g, mem_k_g, w_out):
    bsz, seq, dm = x.shape
    n_mem = mem.shape[1]
    n_fox = b_f.shape[0]
    pool_w = pool_scale.shape[0]
    fox_w = n_fox * HEAD_DIM
    mem_w = MEM_HEADS * HEAD_DIM
    assert n_fox <= BF16_ROWS and pool_w == len(POOL_WINDOWS) * POOL_GROUP_DIM
    assert w_in.shape[1] == 2 * pool_w + 4 * fox_w + n_fox + 2 * mem_w
    assert w_out.shape[0] == pool_w + fox_w + mem_w
    ts, tq, tk = SEQ_TILE, Q_TILE, KV_TILE
    assert seq % ts == 0 and seq % tq == 0 and tq % tk == 0 and ts % tk == 0 and ts % CUMSUM_BLOCK == 0
    assert FAST_TILES_PER_TRIP % (tq // tk) == 0

    sizes = (pool_w, pool_w, fox_w, fox_w, fox_w, n_fox, fox_w, mem_w, mem_w)
    w_in_t = w_in.T.astype(BF16)
    parts, off = [], 0
    for w in sizes:
        parts.append(w_in_t[off:off + w])
        off += w
    w_ua, w_ga, w_q, w_k, w_v, w_f, w_gb, w_qm, w_gm = parts
    w_f = jnp.concatenate([w_f, jnp.zeros((BF16_ROWS - n_fox, dm), BF16)], axis=0)
    order = [("ua", w_ua), ("ga", w_ga), ("f", w_f), ("q", w_q), ("k", w_k), ("v", w_v),
             ("gb", w_gb), ("qm", w_qm), ("gm", w_gm)]
    offs, off = {}, 0
    for name, w in order:
        offs[name] = off
        off += w.shape[0]
    wT = jnp.concatenate([w for _, w in order], axis=0)
    n_rows = wT.shape[0]
    tri = (lax.broadcasted_iota(jnp.int32, (CUMSUM_BLOCK, CUMSUM_BLOCK), 0)
           <= lax.broadcasted_iota(jnp.int32, (CUMSUM_BLOCK, CUMSUM_BLOCK), 1)).astype(BF16)
    wp_bd = jax.scipy.linalg.block_diag(*[w_pool[g].T for g in range(len(POOL_WINDOWS))]).astype(BF16)
    wkvT = w_mem_kv.T.astype(BF16)
    wo = w_out.astype(BF16)

    nb = math.gcd(MEM_BATCH_ROWS_PER_STEP, bsz)
    km, vm = pl.pallas_call(
        _mem_kv_kernel,
        grid=(bsz // nb,),
        in_specs=[
            pl.BlockSpec((nb, n_mem, dm), lambda b: (b, 0, 0)),
            pl.BlockSpec((1, dm), lambda b: (0, 0)),
            pl.BlockSpec((2 * mem_w, dm), lambda b: (0, 0)),
            pl.BlockSpec((HEAD_DIM, 1), lambda b: (0, 0)),
        ],
        out_specs=[
            pl.BlockSpec((nb, MEM_HEADS, n_mem, QK_ROWS), lambda b: (b, 0, 0, 0)),
            pl.BlockSpec((nb, MEM_HEADS, MEM_V_ROWS, n_mem), lambda b: (b, 0, 0, 0)),
        ],
        out_shape=[
            jax.ShapeDtypeStruct((bsz, MEM_HEADS, n_mem, QK_ROWS), BF16),
            jax.ShapeDtypeStruct((bsz, MEM_HEADS, MEM_V_ROWS, n_mem), BF16),
        ],
        compiler_params=_params(("arbitrary",)),
        name="mem_kv",
    )(mem, mem_norm_g[None, :], wkvT, _col(mem_k_g))

    const2 = lambda b, s: (0, 0)
    m_bound = (HEAD_DIM * SCALE * LOG2E * BOUND_SLACK
               * jnp.max(jnp.abs(fox_q_g)) * jnp.max(jnp.abs(fox_k_g))).astype(F32).reshape(1)
    fast = (m_bound < FAST_MAX_SHIFT).astype(jnp.int32)
    qT, kp, vT, gbT, mam = pl.pallas_call(
        functools.partial(_proj_kernel, offs=offs, n_fox=n_fox, pool_w=pool_w,
                          fox_w=fox_w, mem_w=mem_w),
        grid=(bsz, seq // ts),
        in_specs=[
            pl.BlockSpec(memory_space=pltpu.SMEM),
            pl.BlockSpec((1, ts, dm), lambda b, s: (b, s, 0)),
            pl.BlockSpec((1, dm), const2),
            pl.BlockSpec((n_rows, dm), const2),
            pl.BlockSpec((BF16_ROWS, 1), const2),
            pl.BlockSpec((CUMSUM_BLOCK, CUMSUM_BLOCK), const2),
            pl.BlockSpec((HEAD_DIM, 1), const2),
            pl.BlockSpec((HEAD_DIM, 1), const2),
            pl.BlockSpec((pool_w, pool_w), const2),
            pl.BlockSpec((pool_w, 1), const2),
            pl.BlockSpec((1, MEM_HEADS, n_mem, QK_ROWS), lambda b, s: (b, 0, 0, 0)),
            pl.BlockSpec((1, MEM_HEADS, MEM_V_ROWS, n_mem), lambda b, s: (b, 0, 0, 0)),
            pl.BlockSpec((HEAD_DIM, 1), const2),
        ],
        out_specs=[
            pl.BlockSpec((1, n_fox, QK_ROWS, ts), lambda b, s: (b, 0, 0, s)),
            pl.BlockSpec((1, n_fox, ts, QK_ROWS), lambda b, s: (b, 0, s, 0)),
            pl.BlockSpec((1, n_fox, ts // tk, HEAD_DIM, tk), lambda b, s: (b, 0, s, 0, 0)),
            pl.BlockSpec((1, fox_w, ts), lambda b, s: (b, 0, s)),
            pl.BlockSpec((1, pool_w + mem_w, ts), lambda b, s: (b, 0, s)),
        ],
        out_shape=[
            jax.ShapeDtypeStruct((bsz, n_fox, QK_ROWS, seq), BF16),
            jax.ShapeDtypeStruct((bsz, n_fox, seq, QK_ROWS), BF16),
            jax.ShapeDtypeStruct((bsz, n_fox, seq // tk, HEAD_DIM, tk), BF16),
            jax.ShapeDtypeStruct((bsz, fox_w, seq), F32),
            jax.ShapeDtypeStruct((bsz, pool_w + mem_w, seq), BF16),
        ],
        scratch_shapes=[
            pltpu.VMEM((BF16_ROWS, LANES), F32),
            pltpu.VMEM((pool_w, LANES), F32),
        ],
        compiler_params=_params(("arbitrary", "arbitrary")),
        name="proj",
    )(m_bound, x, norm_g[None, :], wT, _col(b_f, BF16_ROWS), tri, _col(fox_q_g), _col(fox_k_g),
      wp_bd, _col(pool_scale), km, vm, _col(mem_q_g))

    return pl.pallas_call(
        functools.partial(_fox_out_kernel, pool_w=pool_w),
        grid=(bsz, seq // tq),
        in_specs=[
            pl.BlockSpec(memory_space=pltpu.SMEM),
            pl.BlockSpec((1, n_fox, QK_ROWS, tq), lambda b, i: (b, 0, 0, i)),
            pl.BlockSpec((1, n_fox, seq, QK_ROWS), lambda b, i: (b, 0, 0, 0)),
            pl.BlockSpec((1, n_fox, seq // tk, HEAD_DIM, tk), lambda b, i: (b, 0, 0, 0, 0)),
            pl.BlockSpec((1, fox_w, tq), lambda b, i: (b, 0, i)),
            pl.BlockSpec((1, pool_w + mem_w, tq), lambda b, i: (b, 0, i)),
            pl.BlockSpec((1, tq, dm), lambda b, i: (b, i, 0)),
            pl.BlockSpec((pool_w + fox_w + mem_w, dm), const2),
        ],
        out_specs=pl.BlockSpec((1, tq, dm), lambda b, i: (b, i, 0)),
        out_shape=jax.ShapeDtypeStruct((bsz, seq, dm), x.dtype),
        scratch_shapes=[pltpu.VMEM((n_fox, HEAD_DIM, tq), F32), pltpu.VMEM((n_fox, SUBLANES, tq), F32)],
        compiler_params=_params(("arbitrary", "arbitrary")),
        name="fox_out",
    )(fast, qT, kp, vT, gbT, mam, x, wo)


def kernel(x, mem, norm_g, w_in, b_f, w_pool, pool_scale, fox_q_g, fox_k_g, mem_norm_g,
           w_mem_kv, mem_q_g, mem_k_g, w_out):
    for l in range(norm_g.shape[0]):
        x = _layer(x, mem, norm_g[l], w_in[l], b_f[l], w_pool[l], pool_scale[l], fox_q_g[l],
                   fox_k_g[l], mem_norm_g[l], w_mem_kv[l], mem_q_g[l], mem_k_g[l], w_out[l])
    return x
```

```python
import functools
import math

import jax
import jax.numpy as jnp
from jax import lax
from jax.experimental import pallas as pl
from jax.experimental.pallas import tpu as pltpu

F32 = jnp.float32
BF16 = jnp.bfloat16

HEAD_DIM = 64
EPS = 1e-6
POOL_WINDOWS = (2, 4, 8, 16)
POOL_GROUP_DIM = 64
MEM_HEADS = 4
SCALE = 1.0 / math.sqrt(HEAD_DIM)
LOG2E = math.log2(math.e)
MASK_VALUE = -1e30

LANES = 128
SUBLANES = 8
BF16_ROWS = 16
QK_ROWS = 128
AUG_ROWS = BF16_ROWS
MEM_V_ROWS = HEAD_DIM + BF16_ROWS

MEM_BATCH_ROWS_PER_STEP = 4
SEQ_TILE = 1024
CUMSUM_BLOCK = 256
Q_TILE = 512
KV_TILE = 256
FAST_TILES_PER_TRIP = 6
FAST_LOOKAHEAD = 3
FAST_MAX_SHIFT = 50.0
BOUND_SLACK = 1.01
VMEM_LIMIT_BYTES = 56 * 1024 * 1024

NT_DIMS = (((1,), (1,)), ((), ()))
TN_DIMS = (((0,), (0,)), ((), ()))


def _dot(a, b):
    return jnp.dot(a, b, preferred_element_type=F32)


def _dot_nt(a, b):
    return lax.dot_general(a, b, NT_DIMS, preferred_element_type=F32)


def _dot_tn(a, b):
    return lax.dot_general(a, b, TN_DIMS, preferred_element_type=F32)


def _silu(g):
    return g * jax.nn.sigmoid(g)


def _log_sigmoid(z):
    return jnp.minimum(z, 0.0) - jnp.log1p(jnp.exp(-jnp.abs(z)))


def _head_rms_scale(t):
    return lax.rsqrt(jnp.mean(t * t, axis=0, keepdims=True) + EPS)


def _ones_row_block(n):
    row = lax.broadcasted_iota(jnp.int32, (BF16_ROWS, n), 0)
    return (row == 0).astype(F32)


def _mem_kv_kernel(mem_ref, g_ref, wkvT_ref, gk_ref, km_ref, vm_ref):
    nb, n_mem, dm = mem_ref.shape
    x = mem_ref[...].reshape(nb * n_mem, dm)
    h = (x * lax.rsqrt(jnp.mean(x * x, axis=-1, keepdims=True) + EPS) * g_ref[...]).astype(BF16)
    kvT = _dot_nt(wkvT_ref[...], h)
    width = MEM_HEADS * HEAD_DIM
    ones_blk = _ones_row_block(n_mem)
    zero_pad = jnp.zeros((QK_ROWS - HEAD_DIM, n_mem), F32)
    for hm in range(MEM_HEADS):
        kh = kvT[hm * HEAD_DIM:(hm + 1) * HEAD_DIM]
        kn = kh * _head_rms_scale(kh) * gk_ref[...]
        vh = kvT[width + hm * HEAD_DIM:width + (hm + 1) * HEAD_DIM]
        for bi in range(nb):
            cols = slice(bi * n_mem, (bi + 1) * n_mem)
            km_ref[bi, hm] = jnp.concatenate([kn[:, cols], zero_pad], axis=0).T.astype(BF16)
            vm_ref[bi, hm] = jnp.concatenate([vh[:, cols], ones_blk], axis=0).astype(BF16)


def _proj_kernel(mb_ref, x_ref, ng_ref, wT_ref, bf_ref, tri_ref, gq_ref, gk_ref, wp_ref, ps_ref,
                 km_ref, vm_ref, gmq_ref,
                 qT_ref, kp_ref, vT_ref, gbT_ref, mam_ref,
                 fcarry_ref, halo_ref, *, offs, n_fox, pool_w, fox_w, mem_w):
    ts = x_ref.shape[1]
    s_idx = pl.program_id(1)

    @pl.when(s_idx == 0)
    def _():
        fcarry_ref[...] = jnp.zeros_like(fcarry_ref)
        halo_ref[...] = jnp.zeros_like(halo_ref)

    x = x_ref[0]
    h = (x * lax.rsqrt(jnp.mean(x * x, axis=-1, keepdims=True) + EPS) * ng_ref[...]).astype(BF16)

    def proj_t(lo, n):
        return _dot_nt(wT_ref[lo:lo + n, :], h)

    tk = vT_ref.shape[4]
    gd = POOL_GROUP_DIM

    sec_m = proj_t(offs["qm"], 2 * mem_w)
    qm = sec_m[0:mem_w]
    gm = sec_m[mem_w:2 * mem_w]
    sec_a = proj_t(offs["ua"], 2 * pool_w + BF16_ROWS)
    u = sec_a[0:pool_w]
    ga = sec_a[pool_w:2 * pool_w]
    z = sec_a[2 * pool_w:2 * pool_w + BF16_ROWS] + bf_ref[...]

    qT = proj_t(offs["q"], fox_w)

    zero_pad = jnp.zeros((QK_ROWS - HEAD_DIM, ts), F32)
    mq_gain = gmq_ref[...] * (SCALE * LOG2E)
    lgs = []
    for hm in range(MEM_HEADS):
        qh = qm[hm * HEAD_DIM:(hm + 1) * HEAD_DIM]
        qn = qh * (_head_rms_scale(qh) * mq_gain)
        qpad = jnp.concatenate([qn, zero_pad], axis=0).astype(BF16)
        lgs.append(_dot(km_ref[0, hm], qpad))

    logf = _log_sigmoid(z)
    hi = logf.astype(BF16)
    r1 = logf - hi.astype(F32)
    mid = r1.astype(BF16)
    lo = (r1 - mid.astype(F32)).astype(BF16)
    parts = jnp.concatenate([hi, mid, lo], axis=0)
    cb = tri_ref.shape[0]
    kT = proj_t(offs["k"], fox_w)
    cs = [_dot(parts[:, c:c + cb], tri_ref[...]) for c in range(0, ts, cb)]
    vT = proj_t(offs["v"], fox_w)

    for hm in range(MEM_HEADS):
        lg = lgs[hm]
        p = jnp.exp2(lg - jnp.max(lg, axis=0, keepdims=True)).astype(BF16)
        yv = _dot(vm_ref[0, hm], p)
        y = yv[0:HEAD_DIM] / yv[HEAD_DIM:HEAD_DIM + 1]
        mm = y * _silu(gm[hm * HEAD_DIM:(hm + 1) * HEAD_DIM])
        mam_ref[0, pool_w + hm * HEAD_DIM:pool_w + (hm + 1) * HEAD_DIM, :] = mm.astype(BF16)

    uext = jnp.concatenate([halo_ref[...], u], axis=1)
    halo_ref[...] = u[:, ts - LANES:]
    pos1 = (s_idx * ts + lax.broadcasted_iota(jnp.int32, (1, ts), 1) + 1).astype(F32)
    acc = uext
    pooled = []
    shift = 1
    for g, w in enumerate(POOL_WINDOWS):
        while shift < w:
            acc = acc + pltpu.roll(acc, shift, 1)
            shift *= 2
        pooled.append(acc[0:gd, LANES:] / jnp.minimum(pos1, float(w)))
        acc = acc[gd:]
    d = (jnp.concatenate(pooled, axis=0) - u).astype(BF16)
    ya = _dot(wp_ref[...], d) * ps_ref[...]
    mam_ref[0, 0:pool_w, :] = (ya * _silu(ga)).astype(BF16)

    gbT_ref[0] = proj_t(offs["gb"], fox_w)

    run = fcarry_ref[:, LANES - 1:LANES]
    blocks = []
    for c in cs:
        blocks.append(c[0:BF16_ROWS] + c[BF16_ROWS:2 * BF16_ROWS] + c[2 * BF16_ROWS:3 * BF16_ROWS] + run)
        run = blocks[-1][:, cb - 1:cb]
    fcum = jnp.concatenate(blocks, axis=1)
    fcarry_ref[...] = fcum[:, ts - LANES:]
    f2 = fcum * LOG2E

    arow = lax.broadcasted_iota(jnp.int32, (AUG_ROWS, ts), 0)

    def split3(v):
        v_hi = v.astype(BF16).astype(F32)
        r = v - v_hi
        v_mid = r.astype(BF16).astype(F32)
        return v_hi, v_mid, (r - v_mid).astype(BF16).astype(F32)

    def rows(*vals):
        out = jnp.zeros((AUG_ROWS, ts), F32)
        for r, v in enumerate(vals):
            out = jnp.where(arow == r, v, out)
        return out

    zero_rows = jnp.zeros((QK_ROWS - HEAD_DIM - AUG_ROWS, ts), F32)
    q_gain = gq_ref[...] * (SCALE * LOG2E)
    for hd in range(n_fox):
        qh = qT[hd * HEAD_DIM:(hd + 1) * HEAD_DIM]
        qn = qh * (_head_rms_scale(qh) * q_gain)
        f3 = split3(f2[hd:hd + 1])
        qT_ref[0, hd, 0:HEAD_DIM, :] = qn.astype(BF16)
        qT_ref[0, hd, HEAD_DIM:HEAD_DIM + AUG_ROWS, :] = rows(
            f3[0], f3[1], f3[2], 1.0, 1.0, 1.0, -mb_ref[0]).astype(BF16)
        qT_ref[0, hd, HEAD_DIM + AUG_ROWS:, :] = zero_rows.astype(BF16)

        kh = kT[hd * HEAD_DIM:(hd + 1) * HEAD_DIM]
        kn = kh * (_head_rms_scale(kh) * gk_ref[...])
        k_aug = rows(1.0, 1.0, 1.0, -f3[0], -f3[1], -f3[2], 1.0)
        kfull = jnp.concatenate([kn, k_aug, zero_rows], axis=0)
        kp_ref[0, hd] = kfull.T.astype(BF16)

    for hd in range(n_fox):
        va = vT[hd * HEAD_DIM:(hd + 1) * HEAD_DIM].astype(BF16)
        for c in range(ts // tk):
            vT_ref[0, hd, c] = va[:, c * tk:(c + 1) * tk]


def _fox_out_kernel(fast_ref, qT_ref, kp_ref, vT_ref, gbT_ref, mam_ref, x_ref, wo_ref, o_ref,
                    acc_ref, den_ref, *, pool_w):
    hg = qT_ref.shape[1]
    tq = qT_ref.shape[3]
    tk = vT_ref.shape[4]
    n_diag = tq // tk
    i = pl.program_id(1)

    acc_ref[...] = jnp.zeros_like(acc_ref)
    den_ref[...] = jnp.zeros_like(den_ref)
    causal = (lax.broadcasted_iota(jnp.int32, (tk, tq), 0)
              <= lax.broadcasted_iota(jnp.int32, (tk, tq), 1))

    def head_scores(hd, j, lo=0):
        start = pl.multiple_of(j * tk, tk)
        return _dot(kp_ref[0, hd, pl.ds(start, tk), :], qT_ref[0, hd, :, lo:])

    def mask(s, diag, lo):
        return s if diag is None else jnp.where(causal[:, :tq - lo], s, MASK_VALUE)

    def fast_tiles(tiles):
        units = [(j, diag, hd) for (j, diag) in tiles for hd in range(hg)]
        scores = {}

        def issue(u):
            j, diag, hd = units[u]
            scores[u] = head_scores(hd, j, 0 if diag is None else diag * tk)

        for u in range(min(FAST_LOOKAHEAD, len(units))):
            issue(u)
        for u, (j, diag, hd) in enumerate(units):
            lo = 0 if diag is None else diag * tk
            p = jnp.exp2(mask(scores.pop(u), diag, lo))
            den_ref[hd, :, lo:] += jnp.sum(p, axis=0, keepdims=True)
            acc_ref[hd, :, lo:] += _dot(vT_ref[0, hd, j], p.astype(BF16))
            if u + FAST_LOOKAHEAD < len(units):
                issue(u + FAST_LOOKAHEAD)

    def safe_step(j, ms, diag=None):
        lo = 0 if diag is None else diag * tk
        scores = [head_scores(hd, j, lo) for hd in range(hg)]
        out = []
        for hd in range(hg):
            s = mask(scores[hd], diag, lo)
            m_old = ms[hd][:, lo:]
            m_new = jnp.maximum(m_old, jnp.max(s, axis=0, keepdims=True))
            p = jnp.exp2(s - m_new)
            alpha = jnp.exp2(m_old - m_new)
            den_ref[hd, :, lo:] = alpha * den_ref[hd, :, lo:] + jnp.sum(p, axis=0, keepdims=True)
            acc_ref[hd, :, lo:] = alpha * acc_ref[hd, :, lo:] + _dot(vT_ref[0, hd, j], p.astype(BF16))
            out.append(m_new if lo == 0 else jnp.concatenate([ms[hd][:, :lo], m_new], axis=1))
        return tuple(out)

    def finish():
        mam = mam_ref[0]
        fox_w = hg * HEAD_DIM
        part = (_dot_tn(mam[0:pool_w], wo_ref[0:pool_w, :])
                + _dot_tn(mam[pool_w:], wo_ref[pool_w + fox_w:, :]))
        ys = []
        for hd in range(hg):
            ys.append(acc_ref[hd] / den_ref[hd, 0:1, :])
        yb = (jnp.concatenate(ys, axis=0) * _silu(gbT_ref[0])).astype(BF16)
        o_ref[0] = x_ref[0] + part + _dot_tn(yb, wo_ref[pool_w:pool_w + fox_w, :])

    @pl.when(fast_ref[0] != 0)
    def _():
        def below_diagonal(first, count):
            return [(first + d, None) for d in range(count)]

        def body(c, carry):
            fast_tiles(below_diagonal(c * FAST_TILES_PER_TRIP, FAST_TILES_PER_TRIP))
            return carry

        n_below = i * n_diag
        n_trips = n_below // FAST_TILES_PER_TRIP
        lax.fori_loop(0, n_trips, body, 0)
        for rest in range(0, FAST_TILES_PER_TRIP, n_diag):
            @pl.when(n_below - n_trips * FAST_TILES_PER_TRIP == rest)
            def _():
                fast_tiles(below_diagonal(n_trips * FAST_TILES_PER_TRIP, rest)
                           + [(n_below + c, c) for c in range(n_diag)])
                finish()

    @pl.when(fast_ref[0] == 0)
    def _():
        ms = tuple(jnp.full((1, tq), MASK_VALUE, F32) for _ in range(hg))
        ms = lax.fori_loop(0, i * n_diag, lambda j, c: safe_step(j, c), ms)
        for c in range(n_diag):
            ms = safe_step(i * n_diag + c, ms, diag=c)
        finish()


def _col(v, rows=None):
    v = v.astype(F32)
    if rows is not None and rows > v.shape[0]:
        v = jnp.concatenate([v, jnp.zeros((rows - v.shape[0],), F32)])
    return v[:, None]


def _params(sem):
    return pltpu.CompilerParams(dimension_semantics=sem, vmem_limit_bytes=VMEM_LIMIT_BYTES)


def _layer(x, mem, norm_g, w_in, b_f, w_pool, pool_scale, fox_q_g, fox_k_g,
           mem_norm_g, w_mem_kv, mem_q_g, mem_k_g, w_out):
    bsz, seq, dm = x.shape
    n_mem = mem.shape[1]
    n_fox = b_f.shape[0]
    pool_w = pool_scale.shape[0]
    fox_w = n_fox * HEAD_DIM
    mem_w = MEM_HEADS * HEAD_DIM
    assert n_fox <= BF16_ROWS and pool_w == len(POOL_WINDOWS) * POOL_GROUP_DIM
    assert w_in.shape[1] == 2 * pool_w + 4 * fox_w + n_fox + 2 * mem_w
    assert w_out.shape[0] == pool_w + fox_w + mem_w
    ts, tq, tk = SEQ_TILE, Q_TILE, KV_TILE
    assert seq % ts == 0 and seq % tq == 0 and tq % tk == 0 and ts % tk == 0 and ts % CUMSUM_BLOCK == 0
    assert FAST_TILES_PER_TRIP % (tq // tk) == 0

    sizes = (pool_w, pool_w, fox_w, fox_w, fox_w, n_fox, fox_w, mem_w, mem_w)
    w_in = w_in.astype(BF16)
    parts, off = [], 0
    for w in sizes:
        parts.append(w_in[:, off:off + w])
        off += w
    w_ua, w_ga, w_q, w_k, w_v, w_f, w_gb, w_qm, w_gm = parts
    w_f = jnp.concatenate([w_f, jnp.zeros((dm, BF16_ROWS - n_fox), w_in.dtype)], axis=1)
    order = [("ua", w_ua), ("ga", w_ga), ("f", w_f), ("q", w_q), ("k", w_k), ("v", w_v),
             ("gb", w_gb), ("qm", w_qm), ("gm", w_gm)]
    offs, off = {}, 0
    for name, w in order:
        offs[name] = off
        off += w.shape[1]
    wT = jnp.concatenate([w for _, w in order], axis=1).T
    n_rows = wT.shape[0]
    tri = (lax.broadcasted_iota(jnp.int32, (CUMSUM_BLOCK, CUMSUM_BLOCK), 0)
           <= lax.broadcasted_iota(jnp.int32, (CUMSUM_BLOCK, CUMSUM_BLOCK), 1)).astype(BF16)
    wp_bd = jax.scipy.linalg.block_diag(*[w_pool[g].T for g in range(len(POOL_WINDOWS))]).astype(BF16)
    wkvT = w_mem_kv.T.astype(BF16)
    wo = w_out.astype(BF16)

    nb = math.gcd(MEM_BATCH_ROWS_PER_STEP, bsz)
    km, vm = pl.pallas_call(
        _mem_kv_kernel,
        grid=(bsz // nb,),
        in_specs=[
            pl.BlockSpec((nb, n_mem, dm), lambda b: (b, 0, 0)),
            pl.BlockSpec((1, dm), lambda b: (0, 0)),
            pl.BlockSpec((2 * mem_w, dm), lambda b: (0, 0)),
            pl.BlockSpec((HEAD_DIM, 1), lambda b: (0, 0)),
        ],
        out_specs=[
            pl.BlockSpec((nb, MEM_HEADS, n_mem, QK_ROWS), lambda b: (b, 0, 0, 0)),
            pl.BlockSpec((nb, MEM_HEADS, MEM_V_ROWS, n_mem), lambda b: (b, 0, 0, 0)),
        ],
        out_shape=[
            jax.ShapeDtypeStruct((bsz, MEM_HEADS, n_mem, QK_ROWS), BF16),
            jax.ShapeDtypeStruct((bsz, MEM_HEADS, MEM_V_ROWS, n_mem), BF16),
        ],
        compiler_params=_params(("arbitrary",)),
        name="mem_kv",
    )(mem, mem_norm_g[None, :], wkvT, _col(mem_k_g))

    const2 = lambda b, s: (0, 0)
    m_bound = (HEAD_DIM * SCALE * LOG2E * BOUND_SLACK
               * jnp.max(jnp.abs(fox_q_g)) * jnp.max(jnp.abs(fox_k_g))).astype(F32).reshape(1)
    fast = (m_bound < FAST_MAX_SHIFT).astype(jnp.int32)
    qT, kp, vT, gbT, mam = pl.pallas_call(
        functools.partial(_proj_kernel, offs=offs, n_fox=n_fox, pool_w=pool_w,
                          fox_w=fox_w, mem_w=mem_w),
        grid=(bsz, seq // ts),
        in_specs=[
            pl.BlockSpec(memory_space=pltpu.SMEM),
            pl.BlockSpec((1, ts, dm), lambda b, s: (b, s, 0)),
            pl.BlockSpec((1, dm), const2),
            pl.BlockSpec((n_rows, dm), const2),
            pl.BlockSpec((BF16_ROWS, 1), const2),
            pl.BlockSpec((CUMSUM_BLOCK, CUMSUM_BLOCK), const2),
            pl.BlockSpec((HEAD_DIM, 1), const2),
            pl.BlockSpec((HEAD_DIM, 1), const2),
            pl.BlockSpec((pool_w, pool_w), const2),
            pl.BlockSpec((pool_w, 1), const2),
            pl.BlockSpec((1, MEM_HEADS, n_mem, QK_ROWS), lambda b, s: (b, 0, 0, 0)),
            pl.BlockSpec((1, MEM_HEADS, MEM_V_ROWS, n_mem), lambda b, s: (b, 0, 0, 0)),
            pl.BlockSpec((HEAD_DIM, 1), const2),
        ],
        out_specs=[
            pl.BlockSpec((1, n_fox, QK_ROWS, ts), lambda b, s: (b, 0, 0, s)),
            pl.BlockSpec((1, n_fox, ts, QK_ROWS), lambda b, s: (b, 0, s, 0)),
            pl.BlockSpec((1, n_fox, ts // tk, HEAD_DIM, tk), lambda b, s: (b, 0, s, 0, 0)),
            pl.BlockSpec((1, fox_w, ts), lambda b, s: (b, 0, s)),
            pl.BlockSpec((1, pool_w + mem_w, ts), lambda b, s: (b, 0, s)),
        ],
        out_shape=[
            jax.ShapeDtypeStruct((bsz, n_fox, QK_ROWS, seq), BF16),
            jax.ShapeDtypeStruct((bsz, n_fox, seq, QK_ROWS), BF16),
            jax.ShapeDtypeStruct((bsz, n_fox, seq // tk, HEAD_DIM, tk), BF16),
            jax.ShapeDtypeStruct((bsz, fox_w, seq), F32),
            jax.ShapeDtypeStruct((bsz, pool_w + mem_w, seq), BF16),
        ],
        scratch_shapes=[
            pltpu.VMEM((BF16_ROWS, LANES), F32),
            pltpu.VMEM((pool_w, LANES), F32),
        ],
        compiler_params=_params(("arbitrary", "arbitrary")),
        name="proj",
    )(m_bound, x, norm_g[None, :], wT, _col(b_f, BF16_ROWS), tri, _col(fox_q_g), _col(fox_k_g),
      wp_bd, _col(pool_scale), km, vm, _col(mem_q_g))

    return pl.pallas_call(
        functools.partial(_fox_out_kernel, pool_w=pool_w),
        grid=(bsz, seq // tq),
        in_specs=[
            pl.BlockSpec(memory_space=pltpu.SMEM),
            pl.BlockSpec((1, n_fox, QK_ROWS, tq), lambda b, i: (b, 0, 0, i)),
            pl.BlockSpec((1, n_fox, seq, QK_ROWS), lambda b, i: (b, 0, 0, 0)),
            pl.BlockSpec((1, n_fox, seq // tk, HEAD_DIM, tk), lambda b, i: (b, 0, 0, 0, 0)),
            pl.BlockSpec((1, fox_w, tq), lambda b, i: (b, 0, i)),
            pl.BlockSpec((1, pool_w + mem_w, tq), lambda b, i: (b, 0, i)),
            pl.BlockSpec((1, tq, dm), lambda b, i: (b, i, 0)),
            pl.BlockSpec((pool_w + fox_w + mem_w, dm), const2),
        ],
        out_specs=pl.BlockSpec((1, tq, dm), lambda b, i: (b, i, 0)),
        out_shape=jax.ShapeDtypeStruct((bsz, seq, dm), x.dtype),
        scratch_shapes=[pltpu.VMEM((n_fox, HEAD_DIM, tq), F32), pltpu.VMEM((n_fox, SUBLANES, tq), F32)],
        compiler_params=pltpu.CompilerParams(
            dimension_semantics=("arbitrary", "arbitrary"), vmem_limit_bytes=VMEM_LIMIT_BYTES,
            allow_input_fusion=[False] * 7 + [True]),
        name="fox_out",
    )(fast, qT, kp, vT, gbT, mam, x, wo)


def kernel(x, mem, norm_g, w_in, b_f, w_pool, pool_scale, fox_q_g, fox_k_g, mem_norm_g,
           w_mem_kv, mem_q_g, mem_k_g, w_out):
    for l in range(norm_g.shape[0]):
        x = _layer(x, mem, norm_g[l], w_in[l], b_f[l], w_pool[l], pool_scale[l], fox_q_g[l],
                   fox_k_g[l], mem_norm_g[l], w_mem_kv[l], mem_q_g[l], mem_k_g[l], w_out[l])
    return x
```
